```python
import math
import jax, jax.numpy as jnp
from jax import lax
import numpy as np

D_MODEL = 1024
BATCH = 4
SEQ = 8192
DEPTH = 1

GRID_W = 64
N_HEADS = 8
HEAD_DIM = 64
ATTN_W = N_HEADS * HEAD_DIM
NA_WIN_H = 8
NA_WIN_W = 16
SSM_W = 512
SSM_GROUP = 16
SSM_GROUPS = SSM_W // SSM_GROUP
SSM_STATE = 64
N_DIR = 2
DT_MIN = 1e-3
DT_MAX = 1e-1
LAMBDA_RE_MAX = -1e-4
N_BRANCH = 2
IN_COLS = 3 * ATTN_W + SSM_W + N_BRANCH * D_MODEL
D_FF = 2816
PLE_DIM = 256
RMS_EPS = 1e-6

kernel_name = "hybrid_natten_s5_macaron_block"


def rms_norm(x, g):
    xf = x.astype(jnp.float32)
    y = xf * lax.rsqrt(jnp.mean(xf * xf, axis=-1, keepdims=True) + RMS_EPS)
    return (y * g.astype(jnp.float32)).astype(x.dtype)


def swiglu(x, w_gate, w_up, w_down):
    return (jax.nn.silu(x @ w_gate) * (x @ w_up)) @ w_down


def neighbourhood_attention(q, k, v, rpb):
    bn, s, h, hd = q.shape
    rows = s // GRID_W
    wh = min(NA_WIN_H, rows)
    ww = NA_WIN_W
    scale = hd ** -0.5
    qg = q.reshape(bn, rows, GRID_W, h, hd)
    kg = k.reshape(bn, rows, GRID_W, h, hd)
    vg = v.reshape(bn, rows, GRID_W, h, hd)
    cols = jnp.arange(GRID_W)
    col_start = jnp.clip(cols - ww // 2, 0, GRID_W - ww)
    col_idx = col_start[:, None] + jnp.arange(ww)[None, :]
    dc_idx = col_idx - cols[:, None] + (NA_WIN_W - 1)

    def row_block(r):
        rs = jnp.clip(r - wh // 2, 0, rows - wh)
        kr = lax.dynamic_slice_in_dim(kg, rs, wh, axis=1)
        vr = lax.dynamic_slice_in_dim(vg, rs, wh, axis=1)
        kn = kr[:, :, col_idx]
        vn = vr[:, :, col_idx]
        qr = lax.dynamic_index_in_dim(qg, r, axis=1, keepdims=False)
        dr_idx = rs + jnp.arange(wh) - r + (NA_WIN_H - 1)
        bias = rpb[:, dr_idx[:, None, None], dc_idx[None]]
        bias = bias.transpose(0, 2, 1, 3)
        sc = jnp.einsum('bchd,brcwhd->bhcrw', qr, kn) * scale + bias[None]
        sc = sc.reshape(bn, h, GRID_W, wh * ww).astype(jnp.float32)
        pr = jax.nn.softmax(sc, axis=-1).astype(v.dtype).reshape(bn, h, GRID_W, wh, ww)
        return jnp.einsum('bhcrw,brcwhd->bchd', pr, vn)

    out = lax.map(row_block, jnp.arange(rows))
    return out.transpose(1, 0, 2, 3, 4).reshape(bn, s, h * hd)


def _ssm_combine(ei, ej):
    a_i, b_i = ei
    a_j, b_j = ej
    return a_j * a_i, a_j * b_i + b_j


def s5_scan(u, lam_re, lam_im, log_dt, b_re, b_im, c_re, c_im, reverse):
    f32 = jnp.float32
    s = u.shape[1]
    lam = lax.complex(jnp.minimum(lam_re.astype(f32), LAMBDA_RE_MAX), lam_im.astype(f32))
    dt = jnp.exp(log_dt.astype(f32))[:, None]
    lam_bar = jnp.exp(lam * dt)
    b_bar = ((lam_bar - 1.0) / lam)[..., None] * lax.complex(b_re.astype(f32), b_im.astype(f32))
    bu = lax.complex(jnp.einsum('bsgi,gpi->bsgp', u, b_bar.real),
                     jnp.einsum('bsgi,gpi->bsgp', u, b_bar.imag))
    a = jnp.broadcast_to(lam_bar, (1, s) + lam_bar.shape)
    _, states = lax.associative_scan(_ssm_combine, (a, bu), reverse=reverse, axis=1)
    return (jnp.einsum('bsgp,gop->bsgo', states.real, c_re.astype(f32))
            - jnp.einsum('bsgp,gop->bsgo', states.imag, c_im.astype(f32)))


def s5_branch(s_in, lam_re, lam_im, log_dt, b_re, b_im, c_re, c_im, d_skip, glu_w, glu_b):
    f32 = jnp.float32
    bn, s, _ = s_in.shape
    u = s_in.astype(f32).reshape(bn, s, SSM_GROUPS, SSM_GROUP)
    y = d_skip.astype(f32).reshape(SSM_GROUPS, SSM_GROUP) * u
    for direction in range(N_DIR):
        y = y + s5_scan(u, lam_re[direction], lam_im[direction], log_dt[direction],
                        b_re[direction], b_im[direction], c_re[direction], c_im[direction],
                        reverse=(direction == 1))
    y = jax.nn.gelu(y.reshape(bn, s, SSM_W))
    y = y * jax.nn.sigmoid(y @ glu_w.astype(f32) + glu_b.astype(f32))
    return y.astype(s_in.dtype)


def setup_inputs(seed: int = 0) -> dict:
    key = jax.random.key(seed)
    ks = iter(jax.random.split(key, 48))
    L = DEPTH
    f32 = jnp.float32

    def nrm(shape, fan_in):
        return jax.random.normal(next(ks), shape, f32) * fan_in ** -0.5

    def gain(shape):
        return 1.0 + 0.05 * jax.random.normal(next(ks), shape, f32)

    def small(shape, scale):
        return scale * jax.random.normal(next(ks), shape, f32)

    G, P, GC = SSM_GROUPS, SSM_STATE, SSM_GROUP
    return {
        "x": jax.random.normal(next(ks), (BATCH, SEQ, D_MODEL), f32),
        "p": jax.random.normal(next(ks), (DEPTH, BATCH, SEQ, PLE_DIM), f32),
        "ffn1_norm": gain((L, D_MODEL)),
        "ffn1_w_gate": nrm((L, D_MODEL, D_FF), D_MODEL),
        "ffn1_w_up": nrm((L, D_MODEL, D_FF), D_MODEL),
        "ffn1_w_down": nrm((L, D_FF, D_MODEL), D_FF),
        "mix_norm": gain((L, D_MODEL)),
        "w_in": nrm((L, D_MODEL, IN_COLS), D_MODEL),
        "na_rpb": small((L, N_HEADS, 2 * NA_WIN_H - 1, 2 * NA_WIN_W - 1), 0.02),
        "ssm_lam_re": -0.5 + small((L, N_DIR, G, P), 0.01),
        "ssm_lam_im": math.pi * jnp.arange(P, dtype=f32) + small((L, N_DIR, G, P), 0.01),
        "ssm_log_dt": jax.random.uniform(next(ks), (L, N_DIR, G), f32,
                                         minval=math.log(DT_MIN), maxval=math.log(DT_MAX)),
        "ssm_b_re": nrm((L, N_DIR, G, P, GC), 2 * GC),
        "ssm_b_im": nrm((L, N_DIR, G, P, GC), 2 * GC),
        "ssm_c_re": nrm((L, N_DIR, G, GC, P), 2 * P),
        "ssm_c_im": nrm((L, N_DIR, G, GC, P), 2 * P),
        "ssm_d": jax.random.normal(next(ks), (L, SSM_W), f32),
        "ssm_glu_w": nrm((L, SSM_W, SSM_W), SSM_W),
        "ssm_glu_b": small((L, SSM_W), 0.01),
        "w_attn_out": nrm((L, ATTN_W, D_MODEL), ATTN_W),
        "w_ssm_out": nrm((L, SSM_W, D_MODEL), SSM_W),
        "w_out": nrm((L, D_MODEL, D_MODEL), D_MODEL),
        "ffn2_norm": gain((L, D_MODEL)),
        "ffn2_w_gate": nrm((L, D_MODEL, D_FF), D_MODEL),
        "ffn2_w_up": nrm((L, D_MODEL, D_FF), D_MODEL),
        "ffn2_w_down": nrm((L, D_FF, D_MODEL), D_FF),
        "ple_norm": gain((L, D_MODEL)),
        "ple_w_gate": nrm((L, D_MODEL, D_MODEL), D_MODEL),
        "ple_w_proj": nrm((L, PLE_DIM, D_MODEL), PLE_DIM),
        "final_norm": gain((D_MODEL,)),
    }


def reference(x, p, ffn1_norm, ffn1_w_gate, ffn1_w_up, ffn1_w_down, mix_norm, w_in, na_rpb,
              ssm_lam_re, ssm_lam_im, ssm_log_dt, ssm_b_re, ssm_b_im, ssm_c_re, ssm_c_im,
              ssm_d, ssm_glu_w, ssm_glu_b, w_attn_out, w_ssm_out, w_out,
              ffn2_norm, ffn2_w_gate, ffn2_w_up, ffn2_w_down,
              ple_norm, ple_w_gate, ple_w_proj, final_norm):
    bn, s, _ = x.shape
    splits = [ATTN_W, 2 * ATTN_W, 3 * ATTN_W, 3 * ATTN_W + SSM_W]
    h = x
    for i in range(DEPTH):
        h = h + 0.5 * swiglu(rms_norm(h, ffn1_norm[i]), ffn1_w_gate[i], ffn1_w_up[i], ffn1_w_down[i])
        u = rms_norm(h, mix_norm[i])
        z = u @ w_in[i]
        q, k, v, s_in, gates = jnp.split(z, splits, axis=-1)
        y_attn = neighbourhood_attention(q.reshape(bn, s, N_HEADS, HEAD_DIM),
                                         k.reshape(bn, s, N_HEADS, HEAD_DIM),
                                         v.reshape(bn, s, N_HEADS, HEAD_DIM),
                                         na_rpb[i]) @ w_attn_out[i]
        y_ssm = s5_branch(s_in, ssm_lam_re[i], ssm_lam_im[i], ssm_log_dt[i],
                          ssm_b_re[i], ssm_b_im[i], ssm_c_re[i], ssm_c_im[i],
                          ssm_d[i], ssm_glu_w[i], ssm_glu_b[i]) @ w_ssm_out[i]
        g_attn, g_ssm = jnp.split(jax.nn.sigmoid(gates), N_BRANCH, axis=-1)
        h = h + (g_attn * y_attn + g_ssm * y_ssm) @ w_out[i]
        h = h + 0.5 * swiglu(rms_norm(h, ffn2_norm[i]), ffn2_w_gate[i], ffn2_w_up[i], ffn2_w_down[i])
        h = h + (p[i] @ ple_w_proj[i]) * jax.nn.sigmoid(rms_norm(h, ple_norm[i]) @ ple_w_gate[i])
    return rms_norm(h, final_norm)
```

```python
import functools
import math

import jax
import jax.numpy as jnp
from jax import lax
from jax.experimental import pallas as pl
from jax.experimental.pallas import tpu as pltpu

D_MODEL = 1024
GRID_W = 64
N_HEADS = 8
HEAD_DIM = 64
ATTN_W = N_HEADS * HEAD_DIM
NA_WIN_H = 8
NA_WIN_W = 16
SSM_W = 512
SSM_GROUP = 16
SSM_GROUPS = SSM_W // SSM_GROUP
SSM_STATE = 64
LAMBDA_RE_MAX = -1e-4
D_FF = 2816
RMS_EPS = 1e-6

LANES = 128
SUBLANES = 8
VMEM_LIMIT = 56 * 1024 * 1024
MASK_BIAS = -1e30

TOKEN_TILE = 512
FF_CHUNKS = ((0, 1536), (1536, 1280))
SSM_CHUNK = 128
SSM_HALF = SSM_W // 2
SSM_QBLK = 4
NQ = SSM_GROUPS // 2 // SSM_QBLK
STATE_COLS = SSM_GROUPS // 2 * SSM_STATE

BF16 = jnp.bfloat16
F32 = jnp.float32


def _dot(a, b):
    return jnp.dot(a, b, preferred_element_type=F32)


def _rms(x, g):
    ms = jnp.mean(x * x, axis=-1, keepdims=True)
    return x * lax.rsqrt(ms + RMS_EPS) * g


def _sigmoid(x):
    return 1.0 / (1.0 + jnp.exp(-x))


def _const_spec(shape):
    nd = len(shape)
    return pl.BlockSpec(shape, lambda *_: (0,) * nd, pipeline_mode=pl.Buffered(1))


def _params(*sem):
    return pltpu.CompilerParams(dimension_semantics=sem, vmem_limit_bytes=VMEM_LIMIT)


def _ffn_kernel(x_ref, g_ref, wg_ref, wu_ref, wd_ref, o_ref):
    x = x_ref[...]
    xn = _rms(x, g_ref[...]).astype(BF16)
    acc = None
    for start, size in FF_CHUNKS:
        gate = _dot(xn, wg_ref[:, start:start + size])
        up = _dot(xn, wu_ref[:, start:start + size])
        act = (gate * _sigmoid(gate) * up).astype(BF16)
        part = _dot(act, wd_ref[start:start + size, :])
        acc = part if acc is None else acc + part
    o_ref[...] = x + 0.5 * acc


def _ffn(x, in_batch_major, norm_g, wg, wu, wd, batch, seq):
    t = TOKEN_TILE
    if in_batch_major:
        in_spec = pl.BlockSpec((None, t, D_MODEL), lambda b, i: (b, i, 0))
    else:
        in_spec = pl.BlockSpec((t, D_MODEL), lambda b, i: (i, b))
    return pl.pallas_call(
        _ffn_kernel,
        grid=(batch, seq // t),
        in_specs=[in_spec, _const_spec((1, D_MODEL)), _const_spec(wg.shape),
                  _const_spec(wu.shape), _const_spec(wd.shape)],
        out_specs=pl.BlockSpec((t, D_MODEL), lambda b, i: (i, b)),
        out_shape=jax.ShapeDtypeStruct((seq, batch * D_MODEL), F32),
        compiler_params=_params("parallel", "parallel"),
        name="ffn",
    )(x, norm_g, wg, wu, wd)


def _inproj_kernel(h_ref, g_ref, w_ref, q_ref, k_ref, v_ref, s_ref, ga_ref, gs_ref):
    u = _rms(h_ref[...], g_ref[...]).astype(BF16)
    a = ATTN_W
    q_ref[...] = (_dot(u, w_ref[:, 0:a]) * (HEAD_DIM ** -0.5)).astype(BF16)
    k_ref[...] = _dot(u, w_ref[:, a:2 * a]).astype(BF16)
    v_ref[...] = _dot(u, w_ref[:, 2 * a:3 * a]).astype(BF16)
    s0 = 3 * a
    s_in = _dot(u, w_ref[:, s0:s0 + SSM_W])
    for c in range(NQ):
        s_ref[c] = s_in[:, c * LANES:(c + 1) * LANES]
    g0 = s0 + SSM_W
    ga_ref[...] = _sigmoid(_dot(u, w_ref[:, g0:g0 + D_MODEL])).astype(BF16)
    gs_ref[...] = _sigmoid(_dot(u, w_ref[:, g0 + D_MODEL:g0 + 2 * D_MODEL])).astype(BF16)


def _inproj(h, norm_g, w_in, batch, seq):
    t = TOKEN_TILE
    def spec(c):
        return pl.BlockSpec((t, c), lambda b, i: (i, b))
    def shape(c, dt):
        return jax.ShapeDtypeStruct((seq, batch * c), dt)
    return pl.pallas_call(
        _inproj_kernel,
        grid=(batch, seq // t),
        in_specs=[spec(D_MODEL), _const_spec((1, D_MODEL)), _const_spec(w_in.shape)],
        out_specs=[spec(ATTN_W), spec(ATTN_W), spec(ATTN_W),
                   pl.BlockSpec((NQ, t, LANES), lambda b, i: (0, i, b)),
                   spec(D_MODEL), spec(D_MODEL)],
        out_shape=[shape(ATTN_W, BF16), shape(ATTN_W, BF16), shape(ATTN_W, BF16),
                   jax.ShapeDtypeStruct((NQ, seq, batch * LANES), F32),
                   shape(D_MODEL, BF16), shape(D_MODEL, BF16)],
        compiler_params=_params("parallel", "parallel"),
        name="inproj",
    )(h, norm_g, w_in)


def _attn_kernel(q_ref, k_ref, v_ref, bias_ref, o_ref, *, rows):
    r = pl.program_id(1)
    rs = jnp.clip(r - NA_WIN_H // 2, 0, rows - NA_WIN_H)
    start = pl.multiple_of(rs * GRID_W, GRID_W)
    nkeys = NA_WIN_H * GRID_W
    lane = lax.broadcasted_iota(jnp.int32, (GRID_W, LANES), 1)
    first = lane < HEAD_DIM
    for pair in range(N_HEADS // 2):
        cols = slice(pair * LANES, (pair + 1) * LANES)
        q2 = q_ref[:, cols]
        k2 = k_ref[pl.ds(start, nkeys), cols]
        v2 = v_ref[pl.ds(start, nkeys), cols]
        outs = []
        for e in range(2):
            keep = first if e == 0 else jnp.logical_not(first)
            qm = jnp.where(keep, q2, jnp.zeros_like(q2))
            s = lax.dot_general(qm, k2, (((1,), (1,)), ((), ())),
                                preferred_element_type=F32)
            s = s + bias_ref[2 * pair + e]
            p = jnp.exp(s - jnp.max(s, axis=-1, keepdims=True))
            denom = jnp.sum(p, axis=-1, keepdims=True)
            outs.append(_dot(p.astype(BF16), v2) / denom)
        o_ref[:, cols] = jnp.where(first, outs[0], outs[1]).astype(BF16)


def _attention(q, k, v, bias_tab, batch, seq):
    rows = seq // GRID_W
    def variant(b, r):
        return (r - jnp.clip(r - NA_WIN_H // 2, 0, rows - NA_WIN_H), 0, 0, 0)
    kv_spec = pl.BlockSpec((seq, ATTN_W), lambda b, r: (0, b), pipeline_mode=pl.Buffered(1))
    return pl.pallas_call(
        functools.partial(_attn_kernel, rows=rows),
        grid=(batch, rows),
        in_specs=[pl.BlockSpec((GRID_W, ATTN_W), lambda b, r: (r, b)), kv_spec, kv_spec,
                  pl.BlockSpec((None, N_HEADS, GRID_W, NA_WIN_H * GRID_W), variant)],
        out_specs=pl.BlockSpec((GRID_W, ATTN_W), lambda b, r: (r, b)),
        out_shape=jax.ShapeDtypeStruct((seq, batch * ATTN_W), BF16),
        compiler_params=_params("parallel", "arbitrary"),
        name="natten",
    )(q, k, v, bias_tab)


def _attn_bias_table(rpb):
    var = jnp.arange(NA_WIN_H)[:, None]
    kr = jnp.arange(NA_WIN_H)[None, :]
    dr = kr - var + (NA_WIN_H - 1)
    c = jnp.arange(GRID_W)[:, None]
    kc = jnp.arange(GRID_W)[None, :]
    cs = jnp.clip(c - NA_WIN_W // 2, 0, GRID_W - NA_WIN_W)
    inside = (kc >= cs) & (kc < cs + NA_WIN_W)
    dc = jnp.clip(kc - c + (NA_WIN_W - 1), 0, 2 * NA_WIN_W - 2)
    tab = rpb[:, dr[:, :, None, None], dc[None, None, :, :]]
    tab = jnp.where(inside[None, None, None], tab, MASK_BIAS)
    tab = tab.transpose(1, 0, 3, 2, 4)
    return tab.reshape(NA_WIN_H, N_HEADS, GRID_W, NA_WIN_H * GRID_W).astype(F32)


def _ssm_kernel(s_ref, wb_ref, wc_ref, are_ref, aim_ref, y_ref,
                u8_ref, st_ref, y8_ref, carry_ref, *, batch):
    d = pl.program_id(0)
    c = pl.program_id(1)
    t_chunk = SSM_CHUNK
    rows8 = 2 * batch

    @pl.when(c == 0)
    def _():
        carry_ref[...] = jnp.zeros_like(carry_ref)

    lane = lax.broadcasted_iota(jnp.int32, (t_chunk, LANES), 1)
    lower = lane < (LANES // 2)
    for q in range(NQ):
        for b in range(batch):
            rows = s_ref[q, pl.ds(b, t_chunk, stride=batch), :]
            u8_ref[q, pl.ds(b, t_chunk, stride=rows8), :] = jnp.where(lower, rows, 0.0)
            u8_ref[q, pl.ds(batch + b, t_chunk, stride=rows8), :] = jnp.where(lower, 0.0, rows)

    for q in range(NQ):
        lhs = u8_ref[q].astype(BF16)
        w = wb_ref[q]
        st_ref[:, q * 256:(q + 1) * 256] = _dot(lhs, w[:, 0:256])
        st_ref[:, STATE_COLS + q * 256:STATE_COLS + (q + 1) * 256] = _dot(lhs, w[:, 256:512])

    a_re = are_ref[...]
    a_im = aim_ref[...]

    def step(i, carry):
        x_re, x_im = carry
        t = jnp.where(d == 0, i, t_chunk - 1 - i)
        row = pl.multiple_of(t * rows8, rows8)
        b_re = st_ref[pl.ds(row, rows8), 0:STATE_COLS]
        b_im = st_ref[pl.ds(row, rows8), STATE_COLS:2 * STATE_COLS]
        n_re = a_re * x_re - a_im * x_im + b_re
        n_im = a_re * x_im + a_im * x_re + b_im
        st_ref[pl.ds(row, rows8), 0:STATE_COLS] = n_re
        st_ref[pl.ds(row, rows8), STATE_COLS:2 * STATE_COLS] = n_im
        return n_re, n_im

    x_re, x_im = lax.fori_loop(
        0, t_chunk, step,
        (carry_ref[:, 0:STATE_COLS], carry_ref[:, STATE_COLS:2 * STATE_COLS]), unroll=4)
    carry_ref[:, 0:STATE_COLS] = x_re
    carry_ref[:, STATE_COLS:2 * STATE_COLS] = x_im

    for q in range(NQ):
        xr = st_ref[:, q * 256:(q + 1) * 256].astype(BF16)
        xi = st_ref[:, STATE_COLS + q * 256:STATE_COLS + (q + 1) * 256].astype(BF16)
        w = wc_ref[q]
        y8_ref[q] = _dot(xr, w[0:256, :]) + _dot(xi, w[256:512, :])

    for q in range(NQ):
        for b in range(batch):
            r0 = y8_ref[q, pl.ds(b, t_chunk, stride=rows8), :]
            r1 = y8_ref[q, pl.ds(batch + b, t_chunk, stride=rows8), :]
            y_ref[q, pl.ds(b, t_chunk, stride=batch), :] = jnp.where(lower, r0, r1)


def _ssm(s_in, wb, wc, a_re, a_im, batch, seq):
    t = SSM_CHUNK
    n_chunks = seq // t
    rows8 = 2 * batch
    def chunk(d, c):
        return jnp.where(d == 0, c, n_chunks - 1 - c)
    return pl.pallas_call(
        functools.partial(_ssm_kernel, batch=batch),
        grid=(2, n_chunks),
        in_specs=[pl.BlockSpec((NQ, t * batch, LANES), lambda d, c: (0, chunk(d, c), 0)),
                  pl.BlockSpec((None, NQ, LANES, 512), lambda d, c: (d, 0, 0, 0)),
                  pl.BlockSpec((None, NQ, 512, LANES), lambda d, c: (d, 0, 0, 0)),
                  pl.BlockSpec((None, rows8, STATE_COLS), lambda d, c: (d, 0, 0)),
                  pl.BlockSpec((None, rows8, STATE_COLS), lambda d, c: (d, 0, 0))],
        out_specs=pl.BlockSpec((None, NQ, t * batch, LANES),
                               lambda d, c: (d, 0, chunk(d, c), 0)),
        out_shape=jax.ShapeDtypeStruct((2, NQ, seq * batch, LANES), F32),
        scratch_shapes=[pltpu.VMEM((NQ, t * rows8, LANES), F32),
                        pltpu.VMEM((t * rows8, 2 * STATE_COLS), F32),
                        pltpu.VMEM((NQ, t * rows8, LANES), F32),
                        pltpu.VMEM((rows8, 2 * STATE_COLS), F32)],
        compiler_params=_params("arbitrary", "arbitrary"),
        name="s5_scan",
    )(s_in, wb, wc, a_re, a_im)


def _ssm_channel_perm():
    q = jnp.arange(NQ)[:, None, None]
    j = jnp.arange(2)[None, :, None]
    c = jnp.arange(SSM_QBLK * SSM_GROUP)[None, None, :]
    return (SSM_HALF * j + SSM_QBLK * SSM_GROUP * q + c).reshape(-1)


def _ssm_weights(lam_re, lam_im, log_dt, b_re, b_im, c_re, c_im, batch):
    lam = lax.complex(jnp.minimum(lam_re, LAMBDA_RE_MAX), lam_im)
    dt = jnp.exp(log_dt)[..., None]
    lam_bar = jnp.exp(lam * dt)
    b_bar = ((lam_bar - 1.0) / lam)[..., None] * lax.complex(b_re, b_im)
    n_dir = lam.shape[0]
    g4, gc, p = SSM_QBLK, SSM_GROUP, SSM_STATE
    eye = jnp.eye(g4, dtype=F32)

    def in_block(x):
        x = x.reshape(n_dir, 2, NQ, g4, p, gc)
        x = x.transpose(0, 2, 1, 3, 5, 4)
        x = x[:, :, :, :, :, None, :] * eye[None, None, None, :, None, :, None]
        return x.reshape(n_dir, NQ, 2 * g4 * gc, g4 * p)

    wb = jnp.concatenate([in_block(b_bar.real), in_block(b_bar.imag)], axis=-1)

    def out_block(x):
        x = x.reshape(n_dir, 2, NQ, g4, gc, p)
        x = x.transpose(0, 2, 3, 5, 1, 4)
        x = x[:, :, :, :, :, None, :] * eye[None, None, :, None, None, :, None]
        return x.reshape(n_dir, NQ, g4 * p, 2 * g4 * gc)

    wc = jnp.concatenate([out_block(c_re), -out_block(c_im)], axis=2)

    def decay(x):
        x = x.reshape(n_dir, 2, 1, STATE_COLS)
        return jnp.broadcast_to(x, (n_dir, 2, batch, STATE_COLS)).reshape(
            n_dir, 2 * batch, STATE_COLS)

    return wb.astype(BF16), wc.astype(BF16), decay(lam_bar.real), decay(lam_bar.imag)


def _gelu_tanh(x):
    return 0.5 * x * (1.0 + jnp.tanh(math.sqrt(2.0 / math.pi) * (x + 0.044715 * (x * x * x))))


def _merge_kernel(h_ref, ya_ref, s_ref, yf_ref, yb_ref, ga_ref, gs_ref,
                  d_ref, gw_ref, gb_ref, wa_ref, ws_ref, wo_ref, o_ref):
    y_attn = _dot(ya_ref[...], wa_ref[...])
    def slabs(ref):
        return jnp.concatenate([ref[c] for c in range(NQ)], axis=-1)
    y = _gelu_tanh(d_ref[...] * slabs(s_ref) + slabs(yf_ref) + slabs(yb_ref))
    y = y * _sigmoid(_dot(y.astype(BF16), gw_ref[...]) + gb_ref[...])
    y_ssm = _dot(y.astype(BF16), ws_ref[...])
    mix = ga_ref[...].astype(F32) * y_attn + gs_ref[...].astype(F32) * y_ssm
    o_ref[...] = h_ref[...] + _dot(mix.astype(BF16), wo_ref[...])


def _merge(h, y_attn, s_in, y_scan, g_attn, g_ssm, d_skip, glu_w, glu_b, wa, ws, wo, batch, seq):
    t = TOKEN_TILE
    def spec(c):
        return pl.BlockSpec((t, c), lambda b, i: (i, b))
    def scan_spec(d):
        return pl.BlockSpec((None, NQ, t, LANES), lambda b, i: (d, 0, i, b))
    y_scan = y_scan.reshape(2, NQ, seq, batch * LANES)
    return pl.pallas_call(
        _merge_kernel,
        grid=(batch, seq // t),
        in_specs=[spec(D_MODEL), spec(ATTN_W),
                  pl.BlockSpec((NQ, t, LANES), lambda b, i: (0, i, b)),
                  scan_spec(0), scan_spec(1),
                  spec(D_MODEL), spec(D_MODEL),
                  _const_spec((1, SSM_W)), _const_spec(glu_w.shape), _const_spec((1, SSM_W)),
                  _const_spec(wa.shape), _const_spec(ws.shape), _const_spec(wo.shape)],
        out_specs=spec(D_MODEL),
        out_shape=jax.ShapeDtypeStruct((seq, batch * D_MODEL), F32),
        compiler_params=_params("parallel", "parallel"),
        name="merge",
    )(h, y_attn, s_in, y_scan, y_scan, g_attn, g_ssm, d_skip, glu_w, glu_b, wa, ws, wo)


def _ple_kernel(h_ref, p_ref, ng_ref, wg_ref, wp_ref, fg_ref, o_ref):
    h = h_ref[...]
    gate = _sigmoid(_dot(_rms(h, ng_ref[...]).astype(BF16), wg_ref[...]))
    h = h + _dot(p_ref[...].astype(BF16), wp_ref[...]) * gate
    o_ref[...] = _rms(h, fg_ref[...])


def _ple(h, p, norm_g, w_gate, w_proj, final_g, batch, seq):
    t = TOKEN_TILE
    ple_dim = p.shape[-1]
    return pl.pallas_call(
        _ple_kernel,
        grid=(batch, seq // t),
        in_specs=[pl.BlockSpec((t, D_MODEL), lambda b, i: (i, b)),
                  pl.BlockSpec((None, t, ple_dim), lambda b, i: (b, i, 0)),
                  _const_spec((1, D_MODEL)), _const_spec(w_gate.shape),
                  _const_spec(w_proj.shape), _const_spec((1, D_MODEL))],
        out_specs=pl.BlockSpec((None, t, D_MODEL), lambda b, i: (b, i, 0)),
        out_shape=jax.ShapeDtypeStruct((batch, seq, D_MODEL), F32),
        compiler_params=_params("parallel", "parallel"),
        name="ple_final",
    )(h, p, norm_g, w_gate, w_proj, final_g)


def kernel(x, p, ffn1_norm, ffn1_w_gate, ffn1_w_up, ffn1_w_down, mix_norm, w_in, na_rpb, ssm_lam_re, ssm_lam_im, ssm_log_dt, ssm_b_re, ssm_b_im, ssm_c_re, ssm_c_im, ssm_d, ssm_glu_w, ssm_glu_b, w_attn_out, w_ssm_out, w_out, ffn2_norm, ffn2_w_gate, ffn2_w_up, ffn2_w_down, ple_norm, ple_w_gate, ple_w_proj, final_norm):
    batch, seq, _ = x.shape
    depth = p.shape[0]
    perm = _ssm_channel_perm()
    s0 = 3 * ATTN_W
    h = x
    batch_major = True
    for i in range(depth):
        row = lambda v: v.reshape(1, -1)
        h = _ffn(h, batch_major, row(ffn1_norm[i]), ffn1_w_gate[i].astype(BF16),
                 ffn1_w_up[i].astype(BF16), ffn1_w_down[i].astype(BF16), batch, seq)
        w_in_i = jnp.concatenate(
            [w_in[i][:, :s0], w_in[i][:, s0:s0 + SSM_W][:, perm], w_in[i][:, s0 + SSM_W:]], axis=1)
        q, k, v, s_in, g_attn, g_ssm = _inproj(h, row(mix_norm[i]), w_in_i.astype(BF16), batch, seq)
        y_attn = _attention(q, k, v, _attn_bias_table(na_rpb[i]), batch, seq)
        wb, wc, a_re, a_im = _ssm_weights(ssm_lam_re[i], ssm_lam_im[i], ssm_log_dt[i],
                                          ssm_b_re[i], ssm_b_im[i], ssm_c_re[i], ssm_c_im[i], batch)
        y_scan = _ssm(s_in.reshape(NQ, seq * batch, LANES), wb, wc, a_re, a_im, batch, seq)
        h = _merge(h, y_attn, s_in, y_scan, g_attn, g_ssm,
                   row(ssm_d[i][perm]), ssm_glu_w[i][perm][:, perm].astype(BF16),
                   row(ssm_glu_b[i][perm]), w_attn_out[i].astype(BF16),
                   w_ssm_out[i][perm].astype(BF16), w_out[i].astype(BF16), batch, seq)
        h = _ffn(h, False, row(ffn2_norm[i]), ffn2_w_gate[i].astype(BF16),
                 ffn2_w_up[i].astype(BF16), ffn2_w_down[i].astype(BF16), batch, seq)
        last = i == depth - 1
        assert last, "multi-layer stacks need the per-layer embedding without the final norm"
        h = _ple(h, p[i], row(ple_norm[i]), ple_w_gate[i].astype(BF16),
                 ple_w_proj[i].astype(BF16), row(final_norm), batch, seq)
    return h
```

```python
import functools
import math

import jax
import jax.numpy as jnp
from jax import lax
from jax.experimental import pallas as pl
from jax.experimental.pallas import tpu as pltpu

D_MODEL = 1024
GRID_W = 64
N_HEADS = 8
HEAD_DIM = 64
ATTN_W = N_HEADS * HEAD_DIM
NA_WIN_H = 8
NA_WIN_W = 16
SSM_W = 512
SSM_GROUP = 16
SSM_GROUPS = SSM_W // SSM_GROUP
SSM_STATE = 64
LAMBDA_RE_MAX = -1e-4
D_FF = 2816
RMS_EPS = 1e-6

LANES = 128
SUBLANES = 8
VMEM_LIMIT = 56 * 1024 * 1024
MASK_BIAS = -1e30

TOKEN_TILE = 512
ATTN_ROWS = 4
FF_CHUNKS = ((0, 1536), (1536, 1280))
SSM_CHUNK = 128
SSM_HALF = SSM_W // 2
SSM_QBLK = 4
NQ = SSM_GROUPS // 2 // SSM_QBLK
STATE_COLS = SSM_GROUPS // 2 * SSM_STATE

BF16 = jnp.bfloat16
F32 = jnp.float32


def _dot(a, b):
    return jnp.dot(a, b, preferred_element_type=F32)


def _rms(x, g):
    ms = jnp.mean(x * x, axis=-1, keepdims=True)
    return x * lax.rsqrt(ms + RMS_EPS) * g


def _sigmoid(x):
    return 1.0 / (1.0 + jnp.exp(-x))


def _const_spec(shape):
    nd = len(shape)
    return pl.BlockSpec(shape, lambda *_: (0,) * nd, pipeline_mode=pl.Buffered(1))


def _params(*sem):
    return pltpu.CompilerParams(dimension_semantics=sem, vmem_limit_bytes=VMEM_LIMIT)


def _ffn_kernel(x_ref, g_ref, wg_ref, wu_ref, wd_ref, o_ref):
    x = x_ref[...]
    xn = _rms(x, g_ref[...]).astype(BF16)
    acc = None
    for start, size in FF_CHUNKS:
        gate = _dot(xn, wg_ref[:, start:start + size])
        up = _dot(xn, wu_ref[:, start:start + size])
        act = (gate * _sigmoid(gate) * up).astype(BF16)
        part = _dot(act, wd_ref[start:start + size, :])
        acc = part if acc is None else acc + part
    o_ref[...] = x + 0.5 * acc


def _ffn(x, in_batch_major, norm_g, wg, wu, wd, batch, seq):
    t = TOKEN_TILE
    if in_batch_major:
        in_spec = pl.BlockSpec((None, t, D_MODEL), lambda b, i: (b, i, 0))
    else:
        in_spec = pl.BlockSpec((t, D_MODEL), lambda b, i: (i, b))
    return pl.pallas_call(
        _ffn_kernel,
        grid=(batch, seq // t),
        in_specs=[in_spec, _const_spec((1, D_MODEL)), _const_spec(wg.shape),
                  _const_spec(wu.shape), _const_spec(wd.shape)],
        out_specs=pl.BlockSpec((t, D_MODEL), lambda b, i: (i, b)),
        out_shape=jax.ShapeDtypeStruct((seq, batch * D_MODEL), F32),
        compiler_params=_params("parallel", "parallel"),
        name="ffn",
    )(x, norm_g, wg, wu, wd)


def _inproj_kernel(h_ref, g_ref, w_ref, q_ref, k_ref, v_ref, s_ref, ga_ref, gs_ref):
    u = _rms(h_ref[...], g_ref[...]).astype(BF16)
    a = ATTN_W
    q_ref[...] = (_dot(u, w_ref[:, 0:a]) * (HEAD_DIM ** -0.5)).astype(BF16)
    k_ref[...] = _dot(u, w_ref[:, a:2 * a]).astype(BF16)
    v_ref[...] = _dot(u, w_ref[:, 2 * a:3 * a]).astype(BF16)
    s0 = 3 * a
    s_in = _dot(u, w_ref[:, s0:s0 + SSM_W])
    for c in range(NQ):
        s_ref[c] = s_in[:, c * LANES:(c + 1) * LANES]
    g0 = s0 + SSM_W
    ga_ref[...] = _sigmoid(_dot(u, w_ref[:, g0:g0 + D_MODEL])).astype(BF16)
    gs_ref[...] = _sigmoid(_dot(u, w_ref[:, g0 + D_MODEL:g0 + 2 * D_MODEL])).astype(BF16)


def _inproj(h, norm_g, w_in, batch, seq):
    t = TOKEN_TILE
    def spec(c):
        return pl.BlockSpec((t, c), lambda b, i: (i, b))
    def shape(c, dt):
        return jax.ShapeDtypeStruct((seq, batch * c), dt)
    return pl.pallas_call(
        _inproj_kernel,
        grid=(batch, seq // t),
        in_specs=[spec(D_MODEL), _const_spec((1, D_MODEL)), _const_spec(w_in.shape)],
        out_specs=[spec(ATTN_W), spec(ATTN_W), spec(ATTN_W),
                   pl.BlockSpec((NQ, t, LANES), lambda b, i: (0, i, b)),
                   spec(D_MODEL), spec(D_MODEL)],
        out_shape=[shape(ATTN_W, BF16), shape(ATTN_W, BF16), shape(ATTN_W, BF16),
                   jax.ShapeDtypeStruct((NQ, seq, batch * LANES), F32),
                   shape(D_MODEL, BF16), shape(D_MODEL, BF16)],
        compiler_params=_params("parallel", "parallel"),
        name="inproj",
    )(h, norm_g, w_in)


def _attn_kernel(q_ref, k_ref, v_ref, bias_ref, o_ref, *, rows):
    nkeys = NA_WIN_H * GRID_W
    lane = lax.broadcasted_iota(jnp.int32, (GRID_W, LANES), 1)
    first = lane < HEAD_DIM
    for i in range(ATTN_ROWS):
        r = pl.program_id(1) * ATTN_ROWS + i
        rs = jnp.clip(r - NA_WIN_H // 2, 0, rows - NA_WIN_H)
        start = pl.multiple_of(rs * GRID_W, GRID_W)
        var = r - rs
        qrows = slice(i * GRID_W, (i + 1) * GRID_W)
        for pair in range(N_HEADS // 2):
            cols = slice(pair * LANES, (pair + 1) * LANES)
            q2 = q_ref[qrows, cols]
            k2 = k_ref[pl.ds(start, nkeys), cols]
            v2 = v_ref[pl.ds(start, nkeys), cols]
            zero = jnp.zeros_like(q2)
            lhs = jnp.concatenate([jnp.where(first, q2, zero), jnp.where(first, zero, q2)], axis=0)
            s = lax.dot_general(lhs, k2, (((1,), (1,)), ((), ())), preferred_element_type=F32)
            s = s + bias_ref[var, 2 * pair * GRID_W:(2 * pair + 2) * GRID_W, :]
            p = jnp.exp(s - jnp.max(s, axis=-1, keepdims=True))
            inv = 1.0 / jnp.sum(p, axis=-1, keepdims=True)
            o = _dot(p.astype(BF16), v2) * inv
            o_ref[qrows, cols] = jnp.where(first, o[0:GRID_W], o[GRID_W:2 * GRID_W]).astype(BF16)


def _attention(q, k, v, bias_tab, batch, seq):
    rows = seq // GRID_W
    qt = ATTN_ROWS * GRID_W
    kv_spec = pl.BlockSpec((seq, ATTN_W), lambda b, r: (0, b), pipeline_mode=pl.Buffered(1))
    return pl.pallas_call(
        functools.partial(_attn_kernel, rows=rows),
        grid=(batch, rows // ATTN_ROWS),
        in_specs=[pl.BlockSpec((qt, ATTN_W), lambda b, r: (r, b)), kv_spec, kv_spec,
                  _const_spec(bias_tab.shape)],
        out_specs=pl.BlockSpec((qt, ATTN_W), lambda b, r: (r, b)),
        out_shape=jax.ShapeDtypeStruct((seq, batch * ATTN_W), BF16),
        compiler_params=_params("parallel", "arbitrary"),
        name="natten",
    )(q, k, v, bias_tab)


def _attn_bias_table(rpb):
    c = jnp.arange(GRID_W)[:, None]
    kc = jnp.arange(GRID_W)[None, :]
    cs = jnp.clip(c - NA_WIN_W // 2, 0, GRID_W - NA_WIN_W)
    inside = (kc >= cs) & (kc < cs + NA_WIN_W)
    dc = kc - c + (NA_WIN_W - 1)
    onehot = (dc[:, :, None] == jnp.arange(2 * NA_WIN_W - 1)[None, None, :]).astype(F32)
    band = jnp.einsum('hyx,ckx->hcyk', rpb.astype(F32), onehot, precision=lax.Precision.HIGHEST)
    band = jnp.where(inside[None, :, None, :], band, MASK_BIAS)
    tabs = [band[:, :, NA_WIN_H - 1 - var:2 * NA_WIN_H - 1 - var, :] for var in range(NA_WIN_H)]
    tab = jnp.stack(tabs, axis=0)
    return tab.reshape(NA_WIN_H, N_HEADS * GRID_W, NA_WIN_H * GRID_W)


def _ssm_kernel(s_ref, wb_ref, wc_ref, are_ref, aim_ref, y_ref,
                u8_ref, st_ref, y8_ref, carry_ref, *, batch):
    d = pl.program_id(0)
    c = pl.program_id(1)
    t_chunk = SSM_CHUNK
    rows8 = 2 * batch

    @pl.when(c == 0)
    def _():
        carry_ref[...] = jnp.zeros_like(carry_ref)

    lane = lax.broadcasted_iota(jnp.int32, (t_chunk, LANES), 1)
    lower = lane < (LANES // 2)
    for q in range(NQ):
        for b in range(batch):
            rows = s_ref[q, :, b * LANES:(b + 1) * LANES]
            u8_ref[q, pl.ds(b, t_chunk, stride=rows8), :] = jnp.where(lower, rows, 0.0)
            u8_ref[q, pl.ds(batch + b, t_chunk, stride=rows8), :] = jnp.where(lower, 0.0, rows)

    for q in range(NQ):
        lhs = u8_ref[q].astype(BF16)
        w = wb_ref[q]
        st_ref[:, q * 256:(q + 1) * 256] = _dot(lhs, w[:, 0:256])
        st_ref[:, STATE_COLS + q * 256:STATE_COLS + (q + 1) * 256] = _dot(lhs, w[:, 256:512])

    a_re = are_ref[...]
    a_im = aim_ref[...]

    def step(i, carry):
        x_re, x_im = carry
        t = jnp.where(d == 0, i, t_chunk - 1 - i)
        row = pl.multiple_of(t * rows8, rows8)
        b_re = st_ref[pl.ds(row, rows8), 0:STATE_COLS]
        b_im = st_ref[pl.ds(row, rows8), STATE_COLS:2 * STATE_COLS]
        n_re = a_re * x_re - a_im * x_im + b_re
        n_im = a_re * x_im + a_im * x_re + b_im
        st_ref[pl.ds(row, rows8), 0:STATE_COLS] = n_re
        st_ref[pl.ds(row, rows8), STATE_COLS:2 * STATE_COLS] = n_im
        return n_re, n_im

    x_re, x_im = lax.fori_loop(
        0, t_chunk, step,
        (carry_ref[:, 0:STATE_COLS], carry_ref[:, STATE_COLS:2 * STATE_COLS]), unroll=4)
    carry_ref[:, 0:STATE_COLS] = x_re
    carry_ref[:, STATE_COLS:2 * STATE_COLS] = x_im

    for q in range(NQ):
        xr = st_ref[:, q * 256:(q + 1) * 256].astype(BF16)
        xi = st_ref[:, STATE_COLS + q * 256:STATE_COLS + (q + 1) * 256].astype(BF16)
        w = wc_ref[q]
        y8_ref[q] = _dot(xr, w[0:256, :]) + _dot(xi, w[256:512, :])

    for q in range(NQ):
        for b in range(batch):
            r0 = y8_ref[q, pl.ds(b, t_chunk, stride=rows8), :]
            r1 = y8_ref[q, pl.ds(batch + b, t_chunk, stride=rows8), :]
            y_ref[q, :, b * LANES:(b + 1) * LANES] = jnp.where(lower, r0, r1)


def _ssm(s_in, wb, wc, a_re, a_im, batch, seq):
    t = SSM_CHUNK
    n_chunks = seq // t
    rows8 = 2 * batch
    def chunk(d, c):
        return jnp.where(d == 0, c, n_chunks - 1 - c)
    return pl.pallas_call(
        functools.partial(_ssm_kernel, batch=batch),
        grid=(2, n_chunks),
        in_specs=[pl.BlockSpec((NQ, t, batch * LANES), lambda d, c: (0, chunk(d, c), 0)),
                  pl.BlockSpec((None, NQ, LANES, 512), lambda d, c: (d, 0, 0, 0)),
                  pl.BlockSpec((None, NQ, 512, LANES), lambda d, c: (d, 0, 0, 0)),
                  pl.BlockSpec((None, rows8, STATE_COLS), lambda d, c: (d, 0, 0)),
                  pl.BlockSpec((None, rows8, STATE_COLS), lambda d, c: (d, 0, 0))],
        out_specs=pl.BlockSpec((None, NQ, t, batch * LANES),
                               lambda d, c: (d, 0, chunk(d, c), 0)),
        out_shape=jax.ShapeDtypeStruct((2, NQ, seq, batch * LANES), F32),
        scratch_shapes=[pltpu.VMEM((NQ, t * rows8, LANES), F32),
                        pltpu.VMEM((t * rows8, 2 * STATE_COLS), F32),
                        pltpu.VMEM((NQ, t * rows8, LANES), F32),
                        pltpu.VMEM((rows8, 2 * STATE_COLS), F32)],
        compiler_params=_params("arbitrary", "arbitrary"),
        name="s5_scan",
    )(s_in, wb, wc, a_re, a_im)


def _ssm_channel_perm():
    q = jnp.arange(NQ)[:, None, None]
    j = jnp.arange(2)[None, :, None]
    c = jnp.arange(SSM_QBLK * SSM_GROUP)[None, None, :]
    return (SSM_HALF * j + SSM_QBLK * SSM_GROUP * q + c).reshape(-1)


def _ssm_weights(lam_re, lam_im, log_dt, b_re, b_im, c_re, c_im, batch):
    lam = lax.complex(jnp.minimum(lam_re, LAMBDA_RE_MAX), lam_im)
    dt = jnp.exp(log_dt)[..., None]
    lam_bar = jnp.exp(lam * dt)
    b_bar = ((lam_bar - 1.0) / lam)[..., None] * lax.complex(b_re, b_im)
    n_dir = lam.shape[0]
    g4, gc, p = SSM_QBLK, SSM_GROUP, SSM_STATE
    eye = jnp.eye(g4, dtype=F32)

    def in_block(x):
        x = x.reshape(n_dir, 2, NQ, g4, p, gc)
        x = x.transpose(0, 2, 1, 3, 5, 4)
        x = x[:, :, :, :, :, None, :] * eye[None, None, None, :, None, :, None]
        return x.reshape(n_dir, NQ, 2 * g4 * gc, g4 * p)

    wb = jnp.concatenate([in_block(b_bar.real), in_block(b_bar.imag)], axis=-1)

    def out_block(x):
        x = x.reshape(n_dir, 2, NQ, g4, gc, p)
        x = x.transpose(0, 2, 3, 5, 1, 4)
        x = x[:, :, :, :, :, None, :] * eye[None, None, :, None, None, :, None]
        return x.reshape(n_dir, NQ, g4 * p, 2 * g4 * gc)

    wc = jnp.concatenate([out_block(c_re), -out_block(c_im)], axis=2)

    def decay(x):
        x = x.reshape(n_dir, 2, 1, STATE_COLS)
        return jnp.broadcast_to(x, (n_dir, 2, batch, STATE_COLS)).reshape(
            n_dir, 2 * batch, STATE_COLS)

    return wb.astype(BF16), wc.astype(BF16), decay(lam_bar.real), decay(lam_bar.imag)


def _gelu_tanh(x):
    return 0.5 * x * (1.0 + jnp.tanh(math.sqrt(2.0 / math.pi) * (x + 0.044715 * (x * x * x))))


def _merge_kernel(h_ref, ya_ref, s_ref, yf_ref, yb_ref, ga_ref, gs_ref,
                  d_ref, gw_ref, gb_ref, wa_ref, ws_ref, wo_ref, o_ref):
    y_attn = _dot(ya_ref[...], wa_ref[...])
    def slabs(ref):
        return jnp.concatenate([ref[c] for c in range(NQ)], axis=-1)
    y = _gelu_tanh(d_ref[...] * slabs(s_ref) + slabs(yf_ref) + slabs(yb_ref))
    y = y * _sigmoid(_dot(y.astype(BF16), gw_ref[...]) + gb_ref[...])
    y_ssm = _dot(y.astype(BF16), ws_ref[...])
    mix = ga_ref[...].astype(F32) * y_attn + gs_ref[...].astype(F32) * y_ssm
    o_ref[...] = h_ref[...] + _dot(mix.astype(BF16), wo_ref[...])


def _merge(h, y_attn, s_in, y_scan, g_attn, g_ssm, d_skip, glu_w, glu_b, wa, ws, wo, batch, seq):
    t = TOKEN_TILE
    def spec(c):
        return pl.BlockSpec((t, c), lambda b, i: (i, b))
    def scan_spec(d):
        return pl.BlockSpec((None, NQ, t, LANES), lambda b, i: (d, 0, i, b))
    return pl.pallas_call(
        _merge_kernel,
        grid=(batch, seq // t),
        in_specs=[spec(D_MODEL), spec(ATTN_W),
                  pl.BlockSpec((NQ, t, LANES), lambda b, i: (0, i, b)),
                  scan_spec(0), scan_spec(1),
                  spec(D_MODEL), spec(D_MODEL),
                  _const_spec((1, SSM_W)), _const_spec(glu_w.shape), _const_spec((1, SSM_W)),
                  _const_spec(wa.shape), _const_spec(ws.shape), _const_spec(wo.shape)],
        out_specs=spec(D_MODEL),
        out_shape=jax.ShapeDtypeStruct((seq, batch * D_MODEL), F32),
        compiler_params=_params("parallel", "parallel"),
        name="merge",
    )(h, y_attn, s_in, y_scan, y_scan, g_attn, g_ssm, d_skip, glu_w, glu_b, wa, ws, wo)


def _ple_kernel(h_ref, p_ref, ng_ref, wg_ref, wp_ref, fg_ref, o_ref):
    h = h_ref[...]
    gate = _sigmoid(_dot(_rms(h, ng_ref[...]).astype(BF16), wg_ref[...]))
    h = h + _dot(p_ref[...].astype(BF16), wp_ref[...]) * gate
    o_ref[...] = _rms(h, fg_ref[...])


def _ple(h, p, norm_g, w_gate, w_proj, final_g, batch, seq):
    t = TOKEN_TILE
    ple_dim = p.shape[-1]
    return pl.pallas_call(
        _ple_kernel,
        grid=(batch, seq // t),
        in_specs=[pl.BlockSpec((t, D_MODEL), lambda b, i: (i, b)),
                  pl.BlockSpec((None, t, ple_dim), lambda b, i: (b, i, 0)),
                  _const_spec((1, D_MODEL)), _const_spec(w_gate.shape),
                  _const_spec(w_proj.shape), _const_spec((1, D_MODEL))],
        out_specs=pl.BlockSpec((None, t, D_MODEL), lambda b, i: (b, i, 0)),
        out_shape=jax.ShapeDtypeStruct((batch, seq, D_MODEL), F32),
        compiler_params=_params("parallel", "parallel"),
        name="ple_final",
    )(h, p, norm_g, w_gate, w_proj, final_g)


def kernel(x, p, ffn1_norm, ffn1_w_gate, ffn1_w_up, ffn1_w_down, mix_norm, w_in, na_rpb, ssm_lam_re, ssm_lam_im, ssm_log_dt, ssm_b_re, ssm_b_im, ssm_c_re, ssm_c_im, ssm_d, ssm_glu_w, ssm_glu_b, w_attn_out, w_ssm_out, w_out, ffn2_norm, ffn2_w_gate, ffn2_w_up, ffn2_w_down, ple_norm, ple_w_gate, ple_w_proj, final_norm):
    batch, seq, _ = x.shape
    depth = p.shape[0]
    perm = _ssm_channel_perm()
    s0 = 3 * ATTN_W
    h = x
    batch_major = True
    for i in range(depth):
        row = lambda v: v.reshape(1, -1)
        h = _ffn(h, batch_major, row(ffn1_norm[i]), ffn1_w_gate[i].astype(BF16),
                 ffn1_w_up[i].astype(BF16), ffn1_w_down[i].astype(BF16), batch, seq)
        w_in_i = jnp.concatenate(
            [w_in[i][:, :s0], w_in[i][:, s0:s0 + SSM_W][:, perm], w_in[i][:, s0 + SSM_W:]], axis=1)
        q, k, v, s_in, g_attn, g_ssm = _inproj(h, row(mix_norm[i]), w_in_i.astype(BF16), batch, seq)
        y_attn = _attention(q, k, v, _attn_bias_table(na_rpb[i]), batch, seq)
        wb, wc, a_re, a_im = _ssm_weights(ssm_lam_re[i], ssm_lam_im[i], ssm_log_dt[i],
                                          ssm_b_re[i], ssm_b_im[i], ssm_c_re[i], ssm_c_im[i], batch)
        y_scan = _ssm(s_in, wb, wc, a_re, a_im, batch, seq)
        h = _merge(h, y_attn, s_in, y_scan, g_attn, g_ssm,
                   row(ssm_d[i][perm]), ssm_glu_w[i][perm][:, perm].astype(BF16),
                   row(ssm_glu_b[i][perm]), w_attn_out[i].astype(BF16),
                   w_ssm_out[i][perm].astype(BF16), w_out[i].astype(BF16), batch, seq)
        h = _ffn(h, False, row(ffn2_norm[i]), ffn2_w_gate[i].astype(BF16),
                 ffn2_w_up[i].astype(BF16), ffn2_w_down[i].astype(BF16), batch, seq)
        last = i == depth - 1
        assert last, "multi-layer stacks need the per-layer embedding without the final norm"
        h = _ple(h, p[i], row(ple_norm[i]), ple_w_gate[i].astype(BF16),
                 ple_w_proj[i].astype(BF16), row(final_norm), batch, seq)
    return h
```

```python
import functools
import math

import jax
import jax.numpy as jnp
from jax import lax
from jax.experimental import pallas as pl
from jax.experimental.pallas import tpu as pltpu

D_MODEL = 1024
GRID_W = 64
N_HEADS = 8
HEAD_DIM = 64
ATTN_W = N_HEADS * HEAD_DIM
NA_WIN_H = 8
NA_WIN_W = 16
SSM_W = 512
SSM_GROUP = 16
SSM_GROUPS = SSM_W // SSM_GROUP
SSM_STATE = 64
LAMBDA_RE_MAX = -1e-4
D_FF = 2816
RMS_EPS = 1e-6

LANES = 128
SUBLANES = 8
VMEM_LIMIT = 56 * 1024 * 1024
MASK_BIAS = -1e30

TOKEN_TILE = 512
ATTN_ROWS = NA_WIN_H // 2
ATTN_KEY_ROWS = ATTN_ROWS + NA_WIN_H
FF_CHUNKS = ((0, 1536), (1536, 1280))
SSM_CHUNK = 128
SSM_HALF = SSM_W // 2
SSM_QBLK = 4
NQ = SSM_GROUPS // 2 // SSM_QBLK
STATE_COLS = SSM_GROUPS // 2 * SSM_STATE

BF16 = jnp.bfloat16
F32 = jnp.float32


def _dot(a, b):
    return jnp.dot(a, b, preferred_element_type=F32)


def _rms(x, g):
    ms = jnp.mean(x * x, axis=-1, keepdims=True)
    return x * lax.rsqrt(ms + RMS_EPS) * g


def _sigmoid(x):
    return 1.0 / (1.0 + jnp.exp(-x))


def _const_spec(shape):
    nd = len(shape)
    return pl.BlockSpec(shape, lambda *_: (0,) * nd, pipeline_mode=pl.Buffered(1))


def _params(*sem):
    return pltpu.CompilerParams(dimension_semantics=sem, vmem_limit_bytes=VMEM_LIMIT)


def _ffn_kernel(x_ref, g_ref, wg_ref, wu_ref, wd_ref, o_ref):
    x = x_ref[...]
    xn = _rms(x, g_ref[...]).astype(BF16)
    acc = None
    for start, size in FF_CHUNKS:
        gate = _dot(xn, wg_ref[:, start:start + size])
        up = _dot(xn, wu_ref[:, start:start + size])
        act = (gate * _sigmoid(gate) * up).astype(BF16)
        part = _dot(act, wd_ref[start:start + size, :])
        acc = part if acc is None else acc + part
    o_ref[...] = x + 0.5 * acc


def _ffn(x, in_batch_major, norm_g, wg, wu, wd, batch, seq):
    t = TOKEN_TILE
    if in_batch_major:
        in_spec = pl.BlockSpec((None, t, D_MODEL), lambda b, i: (b, i, 0))
    else:
        in_spec = pl.BlockSpec((t, D_MODEL), lambda b, i: (i, b))
    return pl.pallas_call(
        _ffn_kernel,
        grid=(batch, seq // t),
        in_specs=[in_spec, _const_spec((1, D_MODEL)), _const_spec(wg.shape),
                  _const_spec(wu.shape), _const_spec(wd.shape)],
        out_specs=pl.BlockSpec((t, D_MODEL), lambda b, i: (i, b)),
        out_shape=jax.ShapeDtypeStruct((seq, batch * D_MODEL), F32),
        compiler_params=_params("parallel", "parallel"),
        name="ffn",
    )(x, norm_g, wg, wu, wd)


def _inproj_kernel(h_ref, g_ref, w_ref, q_ref, k_ref, v_ref, s_ref, ga_ref, gs_ref):
    u = _rms(h_ref[...], g_ref[...]).astype(BF16)
    a = ATTN_W
    q_ref[...] = (_dot(u, w_ref[:, 0:a]) * (HEAD_DIM ** -0.5)).astype(BF16)
    k_ref[...] = _dot(u, w_ref[:, a:2 * a]).astype(BF16)
    v_ref[...] = _dot(u, w_ref[:, 2 * a:3 * a]).astype(BF16)
    s0 = 3 * a
    s_in = _dot(u, w_ref[:, s0:s0 + SSM_W])
    for c in range(NQ):
        s_ref[c] = s_in[:, c * LANES:(c + 1) * LANES]
    g0 = s0 + SSM_W
    ga_ref[...] = _sigmoid(_dot(u, w_ref[:, g0:g0 + D_MODEL])).astype(BF16)
    gs_ref[...] = _sigmoid(_dot(u, w_ref[:, g0 + D_MODEL:g0 + 2 * D_MODEL])).astype(BF16)


def _inproj(h, norm_g, w_in, batch, seq):
    t = TOKEN_TILE
    def spec(c):
        return pl.BlockSpec((t, c), lambda b, i: (i, b))
    def shape(c, dt):
        return jax.ShapeDtypeStruct((seq, batch * c), dt)
    return pl.pallas_call(
        _inproj_kernel,
        grid=(batch, seq // t),
        in_specs=[spec(D_MODEL), _const_spec((1, D_MODEL)), _const_spec(w_in.shape)],
        out_specs=[spec(ATTN_W), spec(ATTN_W), spec(ATTN_W),
                   pl.BlockSpec((NQ, t, LANES), lambda b, i: (0, i, b)),
                   spec(D_MODEL), spec(D_MODEL)],
        out_shape=[shape(ATTN_W, BF16), shape(ATTN_W, BF16), shape(ATTN_W, BF16),
                   jax.ShapeDtypeStruct((NQ, seq, batch * LANES), F32),
                   shape(D_MODEL, BF16), shape(D_MODEL, BF16)],
        compiler_params=_params("parallel", "parallel"),
        name="inproj",
    )(h, norm_g, w_in)


def _attn_kernel(q_ref, k_ref, v_ref, bias_ref, bias_odd_ref, o_ref, *, rows):
    j = pl.program_id(1)
    n_blocks = rows // ATTN_ROWS
    stack = 2 * GRID_W
    lane = lax.broadcasted_iota(jnp.int32, (GRID_W, LANES), 1)
    first = lane < HEAD_DIM

    def stacked_queries(cols):
        pieces = []
        for i in range(ATTN_ROWS):
            q2 = q_ref[i * GRID_W:(i + 1) * GRID_W, cols]
            zero = jnp.zeros_like(q2)
            pieces += [jnp.where(first, q2, zero), jnp.where(first, zero, q2)]
        return jnp.concatenate(pieces, axis=0)

    def scores(lhs, keys):
        return lax.dot_general(lhs, keys, (((1,), (1,)), ((), ())), preferred_element_type=F32)

    def softmax_terms(s):
        p = jnp.exp(s - jnp.max(s, axis=-1, keepdims=True))
        return p.astype(BF16), 1.0 / jnp.sum(p, axis=-1, keepdims=True)

    def write(o, inv, cols):
        o = o * inv
        for i in range(ATTN_ROWS):
            blk = o[i * stack:(i + 1) * stack]
            o_ref[i * GRID_W:(i + 1) * GRID_W, cols] = jnp.where(
                first, blk[0:GRID_W], blk[GRID_W:stack]).astype(BF16)

    is_edge = jnp.logical_or(j == 0, j == n_blocks - 1)

    @pl.when(is_edge)
    def _():
        r0 = j * ATTN_ROWS
        rs = jnp.clip(r0 - NA_WIN_H // 2, 0, rows - NA_WIN_H)
        start = pl.multiple_of(rs * GRID_W, GRID_W)
        nkeys = NA_WIN_H * GRID_W
        for pair in range(N_HEADS // 2):
            cols = slice(pair * LANES, (pair + 1) * LANES)
            hrows = slice(pair * stack, (pair + 1) * stack)
            s = scores(stacked_queries(cols), k_ref[pl.ds(start, nkeys), cols])
            ps, invs = [], []
            for i in range(ATTN_ROWS):
                p, inv = softmax_terms(s[i * stack:(i + 1) * stack] + bias_ref[r0 + i - rs, hrows, :])
                ps.append(p)
                invs.append(inv)
            o = _dot(jnp.concatenate(ps, axis=0), v_ref[pl.ds(start, nkeys), cols])
            write(o, jnp.concatenate(invs, axis=0), cols)

    @pl.when(jnp.logical_not(is_edge))
    def _():
        start = pl.multiple_of((j * ATTN_ROWS - NA_WIN_H // 2) * GRID_W, ATTN_ROWS * GRID_W)
        nkeys = ATTN_KEY_ROWS * GRID_W
        for pair in range(N_HEADS // 2):
            cols = slice(pair * LANES, (pair + 1) * LANES)
            hrows = slice(pair * stack, (pair + 1) * stack)
            s = scores(stacked_queries(cols), k_ref[pl.ds(start, nkeys), cols])
            ps, invs = [], []
            for i in range(ATTN_ROWS):
                lo = i // 2 * 2
                hi = lo + NA_WIN_H + 2 * (i % 2)
                bias = bias_odd_ref[hrows, :] if i % 2 else bias_ref[NA_WIN_H // 2, hrows, :]
                p, inv = softmax_terms(s[i * stack:(i + 1) * stack, lo * GRID_W:hi * GRID_W] + bias)
                parts = [p]
                if lo:
                    parts.insert(0, jnp.zeros((stack, lo * GRID_W), BF16))
                if hi < ATTN_KEY_ROWS:
                    parts.append(jnp.zeros((stack, (ATTN_KEY_ROWS - hi) * GRID_W), BF16))
                ps.append(jnp.concatenate(parts, axis=1) if len(parts) > 1 else p)
                invs.append(inv)
            o = _dot(jnp.concatenate(ps, axis=0), v_ref[pl.ds(start, nkeys), cols])
            write(o, jnp.concatenate(invs, axis=0), cols)


def _attention(q, k, v, bias_tab, bias_odd, batch, seq):
    rows = seq // GRID_W
    qt = ATTN_ROWS * GRID_W
    kv_spec = pl.BlockSpec((seq, ATTN_W), lambda b, r: (0, b), pipeline_mode=pl.Buffered(1))
    return pl.pallas_call(
        functools.partial(_attn_kernel, rows=rows),
        grid=(batch, rows // ATTN_ROWS),
        in_specs=[pl.BlockSpec((qt, ATTN_W), lambda b, r: (r, b)), kv_spec, kv_spec,
                  _const_spec(bias_tab.shape), _const_spec(bias_odd.shape)],
        out_specs=pl.BlockSpec((qt, ATTN_W), lambda b, r: (r, b)),
        out_shape=jax.ShapeDtypeStruct((seq, batch * ATTN_W), BF16),
        compiler_params=_params("parallel", "arbitrary"),
        name="natten",
    )(q, k, v, bias_tab, bias_odd)


def _attn_bias_table(rpb):
    c = jnp.arange(GRID_W)[:, None]
    kc = jnp.arange(GRID_W)[None, :]
    cs = jnp.clip(c - NA_WIN_W // 2, 0, GRID_W - NA_WIN_W)
    inside = (kc >= cs) & (kc < cs + NA_WIN_W)
    dc = kc - c + (NA_WIN_W - 1)
    onehot = (dc[:, :, None] == jnp.arange(2 * NA_WIN_W - 1)[None, None, :]).astype(F32)
    band = jnp.einsum('hyx,ckx->hcyk', rpb.astype(F32), onehot, precision=lax.Precision.HIGHEST)
    band = jnp.where(inside[None, :, None, :], band, MASK_BIAS)
    tabs = [band[:, :, NA_WIN_H - 1 - var:2 * NA_WIN_H - 1 - var, :] for var in range(NA_WIN_H)]
    tab = jnp.stack(tabs, axis=0)
    tab = tab.reshape(NA_WIN_H, N_HEADS * GRID_W, NA_WIN_H * GRID_W)
    pad = jnp.full((N_HEADS * GRID_W, GRID_W), MASK_BIAS, F32)
    tab_odd = jnp.concatenate([pad, tab[NA_WIN_H // 2], pad], axis=1)
    return tab, tab_odd


def _ssm_kernel(s_ref, wb_ref, wc_ref, are_ref, aim_ref, y_ref,
                u8_ref, st_ref, y8_ref, carry_ref, *, batch):
    d = pl.program_id(0)
    c = pl.program_id(1)
    t_chunk = SSM_CHUNK
    rows8 = 2 * batch

    @pl.when(c == 0)
    def _():
        carry_ref[...] = jnp.zeros_like(carry_ref)

    lane = lax.broadcasted_iota(jnp.int32, (t_chunk, LANES), 1)
    lower = lane < (LANES // 2)
    for q in range(NQ):
        for b in range(batch):
            rows = s_ref[q, :, b * LANES:(b + 1) * LANES]
            u8_ref[q, pl.ds(b, t_chunk, stride=rows8), :] = jnp.where(lower, rows, 0.0)
            u8_ref[q, pl.ds(batch + b, t_chunk, stride=rows8), :] = jnp.where(lower, 0.0, rows)

    for q in range(NQ):
        lhs = u8_ref[q].astype(BF16)
        w = wb_ref[q]
        st_ref[:, q * 256:(q + 1) * 256] = _dot(lhs, w[:, 0:256])
        st_ref[:, STATE_COLS + q * 256:STATE_COLS + (q + 1) * 256] = _dot(lhs, w[:, 256:512])

    a_re = are_ref[...]
    a_im = aim_ref[...]

    def step(i, carry):
        x_re, x_im = carry
        t = jnp.where(d == 0, i, t_chunk - 1 - i)
        row = pl.multiple_of(t * rows8, rows8)
        b_re = st_ref[pl.ds(row, rows8), 0:STATE_COLS]
        b_im = st_ref[pl.ds(row, rows8), STATE_COLS:2 * STATE_COLS]
        n_re = a_re * x_re - a_im * x_im + b_re
        n_im = a_re * x_im + a_im * x_re + b_im
        st_ref[pl.ds(row, rows8), 0:STATE_COLS] = n_re
        st_ref[pl.ds(row, rows8), STATE_COLS:2 * STATE_COLS] = n_im
        return n_re, n_im

    x_re, x_im = lax.fori_loop(
        0, t_chunk, step,
        (carry_ref[:, 0:STATE_COLS], carry_ref[:, STATE_COLS:2 * STATE_COLS]), unroll=4)
    carry_ref[:, 0:STATE_COLS] = x_re
    carry_ref[:, STATE_COLS:2 * STATE_COLS] = x_im

    for q in range(NQ):
        xr = st_ref[:, q * 256:(q + 1) * 256].astype(BF16)
        xi = st_ref[:, STATE_COLS + q * 256:STATE_COLS + (q + 1) * 256].astype(BF16)
        w = wc_ref[q]
        y8_ref[q] = _dot(xr, w[0:256, :]) + _dot(xi, w[256:512, :])

    for q in range(NQ):
        for b in range(batch):
            r0 = y8_ref[q, pl.ds(b, t_chunk, stride=rows8), :]
            r1 = y8_ref[q, pl.ds(batch + b, t_chunk, stride=rows8), :]
            y_ref[q, :, b * LANES:(b + 1) * LANES] = jnp.where(lower, r0, r1)


def _ssm(s_in, wb, wc, a_re, a_im, batch, seq):
    t = SSM_CHUNK
    n_chunks = seq // t
    rows8 = 2 * batch
    def chunk(d, c):
        return jnp.where(d == 0, c, n_chunks - 1 - c)
    return pl.pallas_call(
        functools.partial(_ssm_kernel, batch=batch),
        grid=(2, n_chunks),
        in_specs=[pl.BlockSpec((NQ, t, batch * LANES), lambda d, c: (0, chunk(d, c), 0)),
                  pl.BlockSpec((None, NQ, LANES, 512), lambda d, c: (d, 0, 0, 0)),
                  pl.BlockSpec((None, NQ, 512, LANES), lambda d, c: (d, 0, 0, 0)),
                  pl.BlockSpec((None, rows8, STATE_COLS), lambda d, c: (d, 0, 0)),
                  pl.BlockSpec((None, rows8, STATE_COLS), lambda d, c: (d, 0, 0))],
        out_specs=pl.BlockSpec((None, NQ, t, batch * LANES),
                               lambda d, c: (d, 0, chunk(d, c), 0)),
        out_shape=jax.ShapeDtypeStruct((2, NQ, seq, batch * LANES), F32),
        scratch_shapes=[pltpu.VMEM((NQ, t * rows8, LANES), F32),
                        pltpu.VMEM((t * rows8, 2 * STATE_COLS), F32),
                        pltpu.VMEM((NQ, t * rows8, LANES), F32),
                        pltpu.VMEM((rows8, 2 * STATE_COLS), F32)],
        compiler_params=_params("arbitrary", "arbitrary"),
        name="s5_scan",
    )(s_in, wb, wc, a_re, a_im)


def _ssm_channel_perm():
    q = jnp.arange(NQ)[:, None, None]
    j = jnp.arange(2)[None, :, None]
    c = jnp.arange(SSM_QBLK * SSM_GROUP)[None, None, :]
    return (SSM_HALF * j + SSM_QBLK * SSM_GROUP * q + c).reshape(-1)


def _ssm_weights(lam_re, lam_im, log_dt, b_re, b_im, c_re, c_im, batch):
    lam = lax.complex(jnp.minimum(lam_re, LAMBDA_RE_MAX), lam_im)
    dt = jnp.exp(log_dt)[..., None]
    lam_bar = jnp.exp(lam * dt)
    b_bar = ((lam_bar - 1.0) / lam)[..., None] * lax.complex(b_re, b_im)
    n_dir = lam.shape[0]
    g4, gc, p = SSM_QBLK, SSM_GROUP, SSM_STATE
    eye = jnp.eye(g4, dtype=F32)

    def in_block(x):
        x = x.reshape(n_dir, 2, NQ, g4, p, gc)
        x = x.transpose(0, 2, 1, 3, 5, 4)
        x = x[:, :, :, :, :, None, :] * eye[None, None, None, :, None, :, None]
        return x.reshape(n_dir, NQ, 2 * g4 * gc, g4 * p)

    wb = jnp.concatenate([in_block(b_bar.real), in_block(b_bar.imag)], axis=-1)

    def out_block(x):
        x = x.reshape(n_dir, 2, NQ, g4, gc, p)
        x = x.transpose(0, 2, 3, 5, 1, 4)
        x = x[:, :, :, :, :, None, :] * eye[None, None, :, None, None, :, None]
        return x.reshape(n_dir, NQ, g4 * p, 2 * g4 * gc)

    wc = jnp.concatenate([out_block(c_re), -out_block(c_im)], axis=2)

    def decay(x):
        x = x.reshape(n_dir, 2, 1, STATE_COLS)
        return jnp.broadcast_to(x, (n_dir, 2, batch, STATE_COLS)).reshape(
            n_dir, 2 * batch, STATE_COLS)

    return wb.astype(BF16), wc.astype(BF16), decay(lam_bar.real), decay(lam_bar.imag)


def _gelu_tanh(x):
    return 0.5 * x * (1.0 + jnp.tanh(math.sqrt(2.0 / math.pi) * (x + 0.044715 * (x * x * x))))


def _merge_kernel(h_ref, ya_ref, s_ref, yf_ref, yb_ref, ga_ref, gs_ref,
                  d_ref, gw_ref, gb_ref, wa_ref, ws_ref, wo_ref, o_ref):
    y_attn = _dot(ya_ref[...], wa_ref[...])
    def slabs(ref):
        return jnp.concatenate([ref[c] for c in range(NQ)], axis=-1)
    y = _gelu_tanh(d_ref[...] * slabs(s_ref) + slabs(yf_ref) + slabs(yb_ref))
    y = y * _sigmoid(_dot(y.astype(BF16), gw_ref[...]) + gb_ref[...])
    y_ssm = _dot(y.astype(BF16), ws_ref[...])
    mix = ga_ref[...].astype(F32) * y_attn + gs_ref[...].astype(F32) * y_ssm
    o_ref[...] = h_ref[...] + _dot(mix.astype(BF16), wo_ref[...])


def _merge(h, y_attn, s_in, y_scan, g_attn, g_ssm, d_skip, glu_w, glu_b, wa, ws, wo, batch, seq):
    t = TOKEN_TILE
    def spec(c):
        return pl.BlockSpec((t, c), lambda b, i: (i, b))
    def scan_spec(d):
        return pl.BlockSpec((None, NQ, t, LANES), lambda b, i: (d, 0, i, b))
    return pl.pallas_call(
        _merge_kernel,
        grid=(batch, seq // t),
        in_specs=[spec(D_MODEL), spec(ATTN_W),
                  pl.BlockSpec((NQ, t, LANES), lambda b, i: (0, i, b)),
                  scan_spec(0), scan_spec(1),
                  spec(D_MODEL), spec(D_MODEL),
                  _const_spec((1, SSM_W)), _const_spec(glu_w.shape), _const_spec((1, SSM_W)),
                  _const_spec(wa.shape), _const_spec(ws.shape), _const_spec(wo.shape)],
        out_specs=spec(D_MODEL),
        out_shape=jax.ShapeDtypeStruct((seq, batch * D_MODEL), F32),
        compiler_params=_params("parallel", "parallel"),
        name="merge",
    )(h, y_attn, s_in, y_scan, y_scan, g_attn, g_ssm, d_skip, glu_w, glu_b, wa, ws, wo)


def _ple_kernel(h_ref, p_ref, ng_ref, wg_ref, wp_ref, fg_ref, o_ref):
    h = h_ref[...]
    gate = _sigmoid(_dot(_rms(h, ng_ref[...]).astype(BF16), wg_ref[...]))
    h = h + _dot(p_ref[...].astype(BF16), wp_ref[...]) * gate
    o_ref[...] = _rms(h, fg_ref[...])


def _ple(h, p, norm_g, w_gate, w_proj, final_g, batch, seq):
    t = TOKEN_TILE
    ple_dim = p.shape[-1]
    return pl.pallas_call(
        _ple_kernel,
        grid=(batch, seq // t),
        in_specs=[pl.BlockSpec((t, D_MODEL), lambda b, i: (i, b)),
                  pl.BlockSpec((None, t, ple_dim), lambda b, i: (b, i, 0)),
                  _const_spec((1, D_MODEL)), _const_spec(w_gate.shape),
                  _const_spec(w_proj.shape), _const_spec((1, D_MODEL))],
        out_specs=pl.BlockSpec((None, t, D_MODEL), lambda b, i: (b, i, 0)),
        out_shape=jax.ShapeDtypeStruct((batch, seq, D_MODEL), F32),
        compiler_params=_params("parallel", "parallel"),
        name="ple_final",
    )(h, p, norm_g, w_gate, w_proj, final_g)


def kernel(x, p, ffn1_norm, ffn1_w_gate, ffn1_w_up, ffn1_w_down, mix_norm, w_in, na_rpb, ssm_lam_re, ssm_lam_im, ssm_log_dt, ssm_b_re, ssm_b_im, ssm_c_re, ssm_c_im, ssm_d, ssm_glu_w, ssm_glu_b, w_attn_out, w_ssm_out, w_out, ffn2_norm, ffn2_w_gate, ffn2_w_up, ffn2_w_down, ple_norm, ple_w_gate, ple_w_proj, final_norm):
    batch, seq, _ = x.shape
    depth = p.shape[0]
    perm = _ssm_channel_perm()
    s0 = 3 * ATTN_W
    h = x
    batch_major = True
    for i in range(depth):
        row = lambda v: v.reshape(1, -1)
        h = _ffn(h, batch_major, row(ffn1_norm[i]), ffn1_w_gate[i].astype(BF16),
                 ffn1_w_up[i].astype(BF16), ffn1_w_down[i].astype(BF16), batch, seq)
        w_in_i = jnp.concatenate(
            [w_in[i][:, :s0], w_in[i][:, s0:s0 + SSM_W][:, perm], w_in[i][:, s0 + SSM_W:]], axis=1)
        q, k, v, s_in, g_attn, g_ssm = _inproj(h, row(mix_norm[i]), w_in_i.astype(BF16), batch, seq)
        y_attn = _attention(q, k, v, *_attn_bias_table(na_rpb[i]), batch, seq)
        wb, wc, a_re, a_im = _ssm_weights(ssm_lam_re[i], ssm_lam_im[i], ssm_log_dt[i],
                                          ssm_b_re[i], ssm_b_im[i], ssm_c_re[i], ssm_c_im[i], batch)
        y_scan = _ssm(s_in, wb, wc, a_re, a_im, batch, seq)
        h = _merge(h, y_attn, s_in, y_scan, g_attn, g_ssm,
                   row(ssm_d[i][perm]), ssm_glu_w[i][perm][:, perm].astype(BF16),
                   row(ssm_glu_b[i][perm]), w_attn_out[i].astype(BF16),
                   w_ssm_out[i][perm].astype(BF16), w_out[i].astype(BF16), batch, seq)
        h = _ffn(h, False, row(ffn2_norm[i]), ffn2_w_gate[i].astype(BF16),
                 ffn2_w_up[i].astype(BF16), ffn2_w_down[i].astype(BF16), batch, seq)
        last = i == depth - 1
        assert last, "multi-layer stacks need the per-layer embedding without the final norm"
        h = _ple(h, p[i], row(ple_norm[i]), ple_w_gate[i].astype(BF16),
                 ple_w_proj[i].astype(BF16), row(final_norm), batch, seq)
    return h
```

```python
import functools
import math

import jax
import jax.numpy as jnp
from jax import lax
from jax.experimental import pallas as pl
from jax.experimental.pallas import tpu as pltpu

D_MODEL = 1024
GRID_W = 64
N_HEADS = 8
HEAD_DIM = 64
ATTN_W = N_HEADS * HEAD_DIM
NA_WIN_H = 8
NA_WIN_W = 16
SSM_W = 512
SSM_GROUP = 16
SSM_GROUPS = SSM_W // SSM_GROUP
SSM_STATE = 64
LAMBDA_RE_MAX = -1e-4
D_FF = 2816
RMS_EPS = 1e-6

LANES = 128
VMEM_LIMIT = 56 * 1024 * 1024
MASK_BIAS = -1e30

TOKEN_TILE = 512
ATTN_ROWS = NA_WIN_H // 2
ATTN_KEY_ROWS = ATTN_ROWS + NA_WIN_H
FF_CHUNKS = ((0, 1536), (1536, 1280))
SSM_CHUNK = 128
SSM_QBLK = 4
NQ = SSM_W // LANES
STATE_COLS = SSM_GROUPS // 2 * SSM_STATE

BF16 = jnp.bfloat16
F32 = jnp.float32


def _dot(a, b):
    return jnp.dot(a, b, preferred_element_type=F32)


def _rms(x, g):
    ms = jnp.mean(x * x, axis=-1, keepdims=True)
    return x * lax.rsqrt(ms + RMS_EPS) * g


def _sigmoid(x):
    return 1.0 / (1.0 + jnp.exp(-x))


def _const_spec(shape):
    nd = len(shape)
    return pl.BlockSpec(shape, lambda *_: (0,) * nd, pipeline_mode=pl.Buffered(1))


def _params(*sem):
    return pltpu.CompilerParams(dimension_semantics=sem, vmem_limit_bytes=VMEM_LIMIT)


def _ffn_half_step(x, g_ref, wg_ref, wu_ref, wd_ref):
    xn = _rms(x, g_ref[...]).astype(BF16)
    acc = None
    for start, size in FF_CHUNKS:
        gate = _dot(xn, wg_ref[:, start:start + size])
        up = _dot(xn, wu_ref[:, start:start + size])
        act = (gate * _sigmoid(gate) * up).astype(BF16)
        part = _dot(act, wd_ref[start:start + size, :])
        acc = part if acc is None else acc + part
    return x + 0.5 * acc


def _ffn_kernel(x_ref, g_ref, wg_ref, wu_ref, wd_ref, o_ref):
    o_ref[...] = _ffn_half_step(x_ref[...], g_ref, wg_ref, wu_ref, wd_ref)


def _ffn(x, norm_g, wg, wu, wd, batch, seq):
    t = TOKEN_TILE
    return pl.pallas_call(
        _ffn_kernel,
        grid=(batch, seq // t),
        in_specs=[pl.BlockSpec((None, t, D_MODEL), lambda b, i: (b, i, 0)),
                  _const_spec((1, D_MODEL)), _const_spec(wg.shape),
                  _const_spec(wu.shape), _const_spec(wd.shape)],
        out_specs=pl.BlockSpec((t, D_MODEL), lambda b, i: (i, b)),
        out_shape=jax.ShapeDtypeStruct((seq, batch * D_MODEL), F32),
        compiler_params=_params("parallel", "parallel"),
        name="ffn",
    )(x, norm_g, wg, wu, wd)


def _inproj_kernel(h_ref, g_ref, w_ref, q_ref, k_ref, v_ref, s_ref, ga_ref, gs_ref):
    u = _rms(h_ref[...], g_ref[...]).astype(BF16)
    a = ATTN_W
    q_ref[...] = (_dot(u, w_ref[:, 0:a]) * (HEAD_DIM ** -0.5)).astype(BF16)
    k_ref[...] = _dot(u, w_ref[:, a:2 * a]).astype(BF16)
    v_ref[...] = _dot(u, w_ref[:, 2 * a:3 * a]).astype(BF16)
    s0 = 3 * a
    s_in = _dot(u, w_ref[:, s0:s0 + SSM_W])
    for c in range(NQ):
        s_ref[c] = s_in[:, c * LANES:(c + 1) * LANES]
    g0 = s0 + SSM_W
    ga_ref[...] = _sigmoid(_dot(u, w_ref[:, g0:g0 + D_MODEL])).astype(BF16)
    gs_ref[...] = _sigmoid(_dot(u, w_ref[:, g0 + D_MODEL:g0 + 2 * D_MODEL])).astype(BF16)


def _inproj(h, norm_g, w_in, batch, seq):
    t = TOKEN_TILE
    def spec(c):
        return pl.BlockSpec((t, c), lambda b, i: (i, b))
    def shape(c, dt):
        return jax.ShapeDtypeStruct((seq, batch * c), dt)
    return pl.pallas_call(
        _inproj_kernel,
        grid=(batch, seq // t),
        in_specs=[spec(D_MODEL), _const_spec((1, D_MODEL)), _const_spec(w_in.shape)],
        out_specs=[spec(ATTN_W), spec(ATTN_W), spec(ATTN_W),
                   pl.BlockSpec((NQ, t, LANES), lambda b, i: (0, i, b)),
                   spec(D_MODEL), spec(D_MODEL)],
        out_shape=[shape(ATTN_W, BF16), shape(ATTN_W, BF16), shape(ATTN_W, BF16),
                   jax.ShapeDtypeStruct((NQ, seq, batch * LANES), F32),
                   shape(D_MODEL, BF16), shape(D_MODEL, BF16)],
        compiler_params=_params("parallel", "parallel"),
        name="inproj",
    )(h, norm_g, w_in)


def _attn_kernel(q_ref, k_ref, v_ref, bias_ref, bias_odd_ref, o_ref, *, rows):
    j = pl.program_id(1)
    n_blocks = rows // ATTN_ROWS
    stack = 2 * GRID_W
    lane = lax.broadcasted_iota(jnp.int32, (GRID_W, LANES), 1)
    first = lane < HEAD_DIM

    def stacked_queries(cols):
        pieces = []
        for i in range(ATTN_ROWS):
            q2 = q_ref[i * GRID_W:(i + 1) * GRID_W, cols]
            zero = jnp.zeros_like(q2)
            pieces += [jnp.where(first, q2, zero), jnp.where(first, zero, q2)]
        return jnp.concatenate(pieces, axis=0)

    def scores(lhs, keys):
        return lax.dot_general(lhs, keys, (((1,), (1,)), ((), ())), preferred_element_type=F32)

    def softmax_terms(s):
        p = jnp.exp(s - jnp.max(s, axis=-1, keepdims=True))
        return p.astype(BF16), 1.0 / jnp.sum(p, axis=-1, keepdims=True)

    def write(o, inv, cols):
        o = o * inv
        for i in range(ATTN_ROWS):
            blk = o[i * stack:(i + 1) * stack]
            o_ref[i * GRID_W:(i + 1) * GRID_W, cols] = jnp.where(
                first, blk[0:GRID_W], blk[GRID_W:stack]).astype(BF16)

    is_edge = jnp.logical_or(j == 0, j == n_blocks - 1)

    @pl.when(is_edge)
    def _():
        r0 = j * ATTN_ROWS
        rs = jnp.clip(r0 - NA_WIN_H // 2, 0, rows - NA_WIN_H)
        start = pl.multiple_of(rs * GRID_W, GRID_W)
        nkeys = NA_WIN_H * GRID_W
        for pair in range(N_HEADS // 2):
            cols = slice(pair * LANES, (pair + 1) * LANES)
            hrows = slice(pair * stack, (pair + 1) * stack)
            s = scores(stacked_queries(cols), k_ref[pl.ds(start, nkeys), cols])
            ps, invs = [], []
            for i in range(ATTN_ROWS):
                p, inv = softmax_terms(s[i * stack:(i + 1) * stack] + bias_ref[r0 + i - rs, hrows, :])
                ps.append(p)
                invs.append(inv)
            o = _dot(jnp.concatenate(ps, axis=0), v_ref[pl.ds(start, nkeys), cols])
            write(o, jnp.concatenate(invs, axis=0), cols)

    @pl.when(jnp.logical_not(is_edge))
    def _():
        start = pl.multiple_of((j * ATTN_ROWS - NA_WIN_H // 2) * GRID_W, ATTN_ROWS * GRID_W)
        nkeys = ATTN_KEY_ROWS * GRID_W
        for pair in range(N_HEADS // 2):
            cols = slice(pair * LANES, (pair + 1) * LANES)
            hrows = slice(pair * stack, (pair + 1) * stack)
            s = scores(stacked_queries(cols), k_ref[pl.ds(start, nkeys), cols])
            ps, invs = [], []
            for i in range(ATTN_ROWS):
                lo = i // 2 * 2
                hi = lo + NA_WIN_H + 2 * (i % 2)
                bias = bias_odd_ref[hrows, :] if i % 2 else bias_ref[NA_WIN_H // 2, hrows, :]
                p, inv = softmax_terms(s[i * stack:(i + 1) * stack, lo * GRID_W:hi * GRID_W] + bias)
                parts = [p]
                if lo:
                    parts.insert(0, jnp.zeros((stack, lo * GRID_W), BF16))
                if hi < ATTN_KEY_ROWS:
                    parts.append(jnp.zeros((stack, (ATTN_KEY_ROWS - hi) * GRID_W), BF16))
                ps.append(jnp.concatenate(parts, axis=1) if len(parts) > 1 else p)
                invs.append(inv)
            o = _dot(jnp.concatenate(ps, axis=0), v_ref[pl.ds(start, nkeys), cols])
            write(o, jnp.concatenate(invs, axis=0), cols)


def _attention(q, k, v, bias_tab, bias_odd, batch, seq):
    rows = seq // GRID_W
    qt = ATTN_ROWS * GRID_W
    kv_spec = pl.BlockSpec((seq, ATTN_W), lambda b, r: (0, b), pipeline_mode=pl.Buffered(1))
    return pl.pallas_call(
        functools.partial(_attn_kernel, rows=rows),
        grid=(batch, rows // ATTN_ROWS),
        in_specs=[pl.BlockSpec((qt, ATTN_W), lambda b, r: (r, b)), kv_spec, kv_spec,
                  _const_spec(bias_tab.shape), _const_spec(bias_odd.shape)],
        out_specs=pl.BlockSpec((qt, ATTN_W), lambda b, r: (r, b)),
        out_shape=jax.ShapeDtypeStruct((seq, batch * ATTN_W), BF16),
        compiler_params=_params("parallel", "arbitrary"),
        name="natten",
    )(q, k, v, bias_tab, bias_odd)


def _attn_bias_table(rpb):
    c = jnp.arange(GRID_W)[:, None]
    kc = jnp.arange(GRID_W)[None, :]
    cs = jnp.clip(c - NA_WIN_W // 2, 0, GRID_W - NA_WIN_W)
    inside = (kc >= cs) & (kc < cs + NA_WIN_W)
    dc = kc - c + (NA_WIN_W - 1)
    onehot = (dc[:, :, None] == jnp.arange(2 * NA_WIN_W - 1)[None, None, :]).astype(F32)
    band = jnp.einsum('hyx,ckx->hcyk', rpb.astype(F32), onehot, precision=lax.Precision.HIGHEST)
    band = jnp.where(inside[None, :, None, :], band, MASK_BIAS)
    tabs = [band[:, :, NA_WIN_H - 1 - var:2 * NA_WIN_H - 1 - var, :] for var in range(NA_WIN_H)]
    tab = jnp.stack(tabs, axis=0)
    tab = tab.reshape(NA_WIN_H, N_HEADS * GRID_W, NA_WIN_H * GRID_W)
    pad = jnp.full((N_HEADS * GRID_W, GRID_W), MASK_BIAS, F32)
    tab_odd = jnp.concatenate([pad, tab[NA_WIN_H // 2], pad], axis=1)
    return tab, tab_odd


def _ssm_kernel(*refs, batch, reverse):
    if reverse:
        s_ref, prev_ref, wb_ref, wc_ref, are_ref, aim_ref, y_ref = refs[:7]
    else:
        s_ref, d_ref, wb_ref, wc_ref, are_ref, aim_ref, y_ref = refs[:7]
    u8_ref, st_ref, y8_ref, carry_ref = refs[7:]
    t_chunk = SSM_CHUNK
    rows8 = 2 * batch

    @pl.when(pl.program_id(0) == 0)
    def _():
        carry_ref[...] = jnp.zeros_like(carry_ref)

    lane = lax.broadcasted_iota(jnp.int32, (t_chunk, LANES), 1)
    lower = lane < (LANES // 2)
    for q in range(NQ):
        for b in range(batch):
            rows = s_ref[q, :, b * LANES:(b + 1) * LANES]
            u8_ref[q, pl.ds(b, t_chunk, stride=rows8), :] = jnp.where(lower, rows, 0.0)
            u8_ref[q, pl.ds(batch + b, t_chunk, stride=rows8), :] = jnp.where(lower, 0.0, rows)

    for q in range(NQ):
        lhs = u8_ref[q].astype(BF16)
        w = wb_ref[q]
        st_ref[:, q * 256:(q + 1) * 256] = _dot(lhs, w[:, 0:256])
        st_ref[:, STATE_COLS + q * 256:STATE_COLS + (q + 1) * 256] = _dot(lhs, w[:, 256:512])

    a_re = are_ref[...]
    a_im = aim_ref[...]

    def step(i, carry):
        x_re, x_im = carry
        t = t_chunk - 1 - i if reverse else i
        row = pl.multiple_of(t * rows8, rows8)
        b_re = st_ref[pl.ds(row, rows8), 0:STATE_COLS]
        b_im = st_ref[pl.ds(row, rows8), STATE_COLS:2 * STATE_COLS]
        n_re = a_re * x_re - a_im * x_im + b_re
        n_im = a_re * x_im + a_im * x_re + b_im
        st_ref[pl.ds(row, rows8), 0:STATE_COLS] = n_re
        st_ref[pl.ds(row, rows8), STATE_COLS:2 * STATE_COLS] = n_im
        return n_re, n_im

    x_re, x_im = lax.fori_loop(
        0, t_chunk, step,
        (carry_ref[:, 0:STATE_COLS], carry_ref[:, STATE_COLS:2 * STATE_COLS]), unroll=4)
    carry_ref[:, 0:STATE_COLS] = x_re
    carry_ref[:, STATE_COLS:2 * STATE_COLS] = x_im

    for q in range(NQ):
        xr = st_ref[:, q * 256:(q + 1) * 256].astype(BF16)
        xi = st_ref[:, STATE_COLS + q * 256:STATE_COLS + (q + 1) * 256].astype(BF16)
        w = wc_ref[q]
        y8_ref[q] = _dot(xr, w[0:256, :]) + _dot(xi, w[256:512, :])

    for q in range(NQ):
        for b in range(batch):
            cols = slice(b * LANES, (b + 1) * LANES)
            r0 = y8_ref[q, pl.ds(b, t_chunk, stride=rows8), :]
            r1 = y8_ref[q, pl.ds(batch + b, t_chunk, stride=rows8), :]
            base = prev_ref[q, :, cols] if reverse else d_ref[q] * s_ref[q, :, cols]
            y_ref[q, :, cols] = base + jnp.where(lower, r0, r1)


def _ssm(s_in, extra, wb, wc, a_re, a_im, batch, seq, reverse):
    t = SSM_CHUNK
    n_chunks = seq // t
    rows8 = 2 * batch
    chunk = (lambda c: (0, n_chunks - 1 - c, 0)) if reverse else (lambda c: (0, c, 0))
    io_spec = pl.BlockSpec((NQ, t, batch * LANES), chunk)
    extra_spec = io_spec if reverse else _const_spec(extra.shape)
    return pl.pallas_call(
        functools.partial(_ssm_kernel, batch=batch, reverse=reverse),
        grid=(n_chunks,),
        in_specs=[io_spec, extra_spec, _const_spec(wb.shape), _const_spec(wc.shape),
                  _const_spec(a_re.shape), _const_spec(a_im.shape)],
        out_specs=io_spec,
        out_shape=jax.ShapeDtypeStruct((NQ, seq, batch * LANES), F32),
        scratch_shapes=[pltpu.VMEM((NQ, t * rows8, LANES), F32),
                        pltpu.VMEM((t * rows8, 2 * STATE_COLS), F32),
                        pltpu.VMEM((NQ, t * rows8, LANES), F32),
                        pltpu.VMEM((rows8, 2 * STATE_COLS), F32)],
        input_output_aliases={1: 0} if reverse else {},
        compiler_params=_params("arbitrary"),
        name="s5_scan_bwd" if reverse else "s5_scan_fwd",
    )(s_in, extra, wb, wc, a_re, a_im)


def _ssm_weights(lam_re, lam_im, log_dt, b_re, b_im, c_re, c_im, batch):
    lam = lax.complex(jnp.minimum(lam_re, LAMBDA_RE_MAX), lam_im)
    dt = jnp.exp(log_dt)[:, None]
    lam_bar = jnp.exp(lam * dt)
    b_bar = ((lam_bar - 1.0) / lam)[..., None] * lax.complex(b_re, b_im)
    g4, gc, p = SSM_QBLK, SSM_GROUP, SSM_STATE
    eye = jnp.eye(g4, dtype=F32)

    def in_block(x):
        x = x.reshape(NQ, 2, g4, p, gc).transpose(0, 1, 2, 4, 3)
        x = x[:, :, :, :, None, :] * eye[None, None, :, None, :, None]
        return x.reshape(NQ, 2 * g4 * gc, g4 * p)

    wb = jnp.concatenate([in_block(b_bar.real), in_block(b_bar.imag)], axis=-1)

    def out_block(x):
        x = x.reshape(NQ, 2, g4, gc, p).transpose(0, 2, 4, 1, 3)
        x = x[:, :, :, :, None, :] * eye[None, :, None, None, :, None]
        return x.reshape(NQ, g4 * p, 2 * g4 * gc)

    wc = jnp.concatenate([out_block(c_re), -out_block(c_im)], axis=1)

    def decay(x):
        x = x.reshape(NQ, 2, g4 * p).transpose(1, 0, 2).reshape(2, 1, STATE_COLS)
        return jnp.broadcast_to(x, (2, batch, STATE_COLS)).reshape(2 * batch, STATE_COLS)

    return wb.astype(BF16), wc.astype(BF16), decay(lam_bar.real), decay(lam_bar.imag)


def _gelu_tanh(x):
    return 0.5 * x * (1.0 + jnp.tanh(math.sqrt(2.0 / math.pi) * (x + 0.044715 * (x * x * x))))


def _tail_kernel(h_ref, ya_ref, ys_ref, ga_ref, gs_ref, p_ref,
                 gw_ref, gb_ref, wa_ref, ws_ref, wo_ref,
                 fn_ref, wg_ref, wu_ref, wd_ref,
                 pn_ref, pg_ref, pp_ref, on_ref, o_ref):
    y_attn = _dot(ya_ref[...], wa_ref[...])
    y = _gelu_tanh(jnp.concatenate([ys_ref[c] for c in range(NQ)], axis=-1))
    y = y * _sigmoid(_dot(y.astype(BF16), gw_ref[...]) + gb_ref[...])
    y_ssm = _dot(y.astype(BF16), ws_ref[...])
    mix = ga_ref[...].astype(F32) * y_attn + gs_ref[...].astype(F32) * y_ssm
    h = h_ref[...] + _dot(mix.astype(BF16), wo_ref[...])
    h = _ffn_half_step(h, fn_ref, wg_ref, wu_ref, wd_ref)
    gate = _sigmoid(_dot(_rms(h, pn_ref[...]).astype(BF16), pg_ref[...]))
    h = h + _dot(p_ref[...].astype(BF16), pp_ref[...]) * gate
    o_ref[...] = _rms(h, on_ref[...])


def _tail(h, y_attn, y_scan, g_attn, g_ssm, p, weights, batch, seq):
    t = TOKEN_TILE
    def spec(c):
        return pl.BlockSpec((t, c), lambda b, i: (i, b))
    return pl.pallas_call(
        _tail_kernel,
        grid=(batch, seq // t),
        in_specs=[spec(D_MODEL), spec(ATTN_W),
                  pl.BlockSpec((NQ, t, LANES), lambda b, i: (0, i, b)),
                  spec(D_MODEL), spec(D_MODEL),
                  pl.BlockSpec((None, t, p.shape[-1]), lambda b, i: (b, i, 0))]
                 + [_const_spec(w.shape) for w in weights],
        out_specs=pl.BlockSpec((None, t, D_MODEL), lambda b, i: (b, i, 0)),
        out_shape=jax.ShapeDtypeStruct((batch, seq, D_MODEL), F32),
        compiler_params=_params("parallel", "parallel"),
        name="tail",
    )(h, y_attn, y_scan, g_attn, g_ssm, p, *weights)


def kernel(x, p, ffn1_norm, ffn1_w_gate, ffn1_w_up, ffn1_w_down, mix_norm, w_in, na_rpb, ssm_lam_re, ssm_lam_im, ssm_log_dt, ssm_b_re, ssm_b_im, ssm_c_re, ssm_c_im, ssm_d, ssm_glu_w, ssm_glu_b, w_attn_out, w_ssm_out, w_out, ffn2_norm, ffn2_w_gate, ffn2_w_up, ffn2_w_down, ple_norm, ple_w_gate, ple_w_proj, final_norm):
    batch, seq, _ = x.shape
    assert p.shape[0] == 1, "single-layer block"
    i = 0
    row = lambda v: v.reshape(1, -1)
    bf = lambda w: w.astype(BF16)
    h = _ffn(x, row(ffn1_norm[i]), bf(ffn1_w_gate[i]), bf(ffn1_w_up[i]), bf(ffn1_w_down[i]),
             batch, seq)
    q, k, v, s_in, g_attn, g_ssm = _inproj(h, row(mix_norm[i]), bf(w_in[i]), batch, seq)
    y_attn = _attention(q, k, v, *_attn_bias_table(na_rpb[i]), batch, seq)
    y_scan = ssm_d[i].reshape(NQ, 1, LANES)
    for d in range(2):
        wb, wc, a_re, a_im = _ssm_weights(
            ssm_lam_re[i, d], ssm_lam_im[i, d], ssm_log_dt[i, d], ssm_b_re[i, d], ssm_b_im[i, d],
            ssm_c_re[i, d], ssm_c_im[i, d], batch)
        y_scan = _ssm(s_in, y_scan, wb, wc, a_re, a_im, batch, seq, reverse=d == 1)
    weights = [bf(ssm_glu_w[i]), row(ssm_glu_b[i]), bf(w_attn_out[i]), bf(w_ssm_out[i]), bf(w_out[i]),
               row(ffn2_norm[i]), bf(ffn2_w_gate[i]), bf(ffn2_w_up[i]), bf(ffn2_w_down[i]),
               row(ple_norm[i]), bf(ple_w_gate[i]), bf(ple_w_proj[i]), row(final_norm)]
    return _tail(h, y_attn, y_scan, g_attn, g_ssm, p[i], weights, batch, seq)
```

```python
import functools
import math

import jax
import jax.numpy as jnp
from jax import lax
from jax.experimental import pallas as pl
from jax.experimental.pallas import tpu as pltpu

D_MODEL = 1024
GRID_W = 64
N_HEADS = 8
HEAD_DIM = 64
ATTN_W = N_HEADS * HEAD_DIM
NA_WIN_H = 8
NA_WIN_W = 16
SSM_W = 512
SSM_GROUP = 16
SSM_GROUPS = SSM_W // SSM_GROUP
SSM_STATE = 64
LAMBDA_RE_MAX = -1e-4
D_FF = 2816
RMS_EPS = 1e-6

LANES = 128
VMEM_LIMIT = 56 * 1024 * 1024
MASK_BIAS = -1e30

TOKEN_TILE = 512
ATTN_ROWS = NA_WIN_H // 2
ATTN_KEY_ROWS = ATTN_ROWS + NA_WIN_H
FF_CHUNKS = ((0, 1536), (1536, 1280))
SSM_CHUNK = 64
SSM_DEPTH = 4
SSM_QBLK = 4
NQ = SSM_W // LANES
STATE_COLS = SSM_GROUPS // 2 * SSM_STATE

BF16 = jnp.bfloat16
F32 = jnp.float32


def _dot(a, b):
    return jnp.dot(a, b, preferred_element_type=F32)


def _rms(x, g):
    ms = jnp.mean(x * x, axis=-1, keepdims=True)
    return x * lax.rsqrt(ms + RMS_EPS) * g


def _sigmoid(x):
    return 1.0 / (1.0 + jnp.exp(-x))


def _const_spec(shape):
    nd = len(shape)
    return pl.BlockSpec(shape, lambda *_: (0,) * nd, pipeline_mode=pl.Buffered(1))


def _params(*sem):
    return pltpu.CompilerParams(dimension_semantics=sem, vmem_limit_bytes=VMEM_LIMIT)


def _ffn_half_step(x, g_ref, wg_ref, wu_ref, wd_ref):
    xn = _rms(x, g_ref[...]).astype(BF16)
    acc = None
    for start, size in FF_CHUNKS:
        gate = _dot(xn, wg_ref[:, start:start + size])
        up = _dot(xn, wu_ref[:, start:start + size])
        act = (gate * _sigmoid(gate) * up).astype(BF16)
        part = _dot(act, wd_ref[start:start + size, :])
        acc = part if acc is None else acc + part
    return x + 0.5 * acc


def _ffn_kernel(x_ref, g_ref, wg_ref, wu_ref, wd_ref, o_ref):
    o_ref[...] = _ffn_half_step(x_ref[...], g_ref, wg_ref, wu_ref, wd_ref)


def _ffn(x, norm_g, wg, wu, wd, batch, seq):
    t = TOKEN_TILE
    return pl.pallas_call(
        _ffn_kernel,
        grid=(batch, seq // t),
        in_specs=[pl.BlockSpec((None, t, D_MODEL), lambda b, i: (b, i, 0)),
                  _const_spec((1, D_MODEL)), _const_spec(wg.shape),
                  _const_spec(wu.shape), _const_spec(wd.shape)],
        out_specs=pl.BlockSpec((t, D_MODEL), lambda b, i: (i, b)),
        out_shape=jax.ShapeDtypeStruct((seq, batch * D_MODEL), F32),
        compiler_params=_params("parallel", "parallel"),
        name="ffn",
    )(x, norm_g, wg, wu, wd)


def _inproj_kernel(h_ref, g_ref, w_ref, q_ref, k_ref, v_ref, s_ref, ga_ref, gs_ref):
    u = _rms(h_ref[...], g_ref[...]).astype(BF16)
    a = ATTN_W
    q_ref[...] = (_dot(u, w_ref[:, 0:a]) * (HEAD_DIM ** -0.5)).astype(BF16)
    k_ref[...] = _dot(u, w_ref[:, a:2 * a]).astype(BF16)
    v_ref[...] = _dot(u, w_ref[:, 2 * a:3 * a]).astype(BF16)
    s0 = 3 * a
    s_in = _dot(u, w_ref[:, s0:s0 + SSM_W])
    for c in range(NQ):
        s_ref[c] = s_in[:, c * LANES:(c + 1) * LANES]
    g0 = s0 + SSM_W
    ga_ref[...] = _sigmoid(_dot(u, w_ref[:, g0:g0 + D_MODEL])).astype(BF16)
    gs_ref[...] = _sigmoid(_dot(u, w_ref[:, g0 + D_MODEL:g0 + 2 * D_MODEL])).astype(BF16)


def _inproj(h, norm_g, w_in, batch, seq):
    t = TOKEN_TILE
    def spec(c):
        return pl.BlockSpec((t, c), lambda b, i: (i, b))
    def shape(c, dt):
        return jax.ShapeDtypeStruct((seq, batch * c), dt)
    return pl.pallas_call(
        _inproj_kernel,
        grid=(batch, seq // t),
        in_specs=[spec(D_MODEL), _const_spec((1, D_MODEL)), _const_spec(w_in.shape)],
        out_specs=[spec(ATTN_W), spec(ATTN_W), spec(ATTN_W),
                   pl.BlockSpec((NQ, t, LANES), lambda b, i: (0, i, b)),
                   spec(D_MODEL), spec(D_MODEL)],
        out_shape=[shape(ATTN_W, BF16), shape(ATTN_W, BF16), shape(ATTN_W, BF16),
                   jax.ShapeDtypeStruct((NQ, seq, batch * LANES), F32),
                   shape(D_MODEL, BF16), shape(D_MODEL, BF16)],
        compiler_params=_params("parallel", "parallel"),
        name="inproj",
    )(h, norm_g, w_in)


def _attn_kernel(q_ref, k_ref, v_ref, bias_ref, bias_odd_ref, o_ref, *, rows):
    j = pl.program_id(1)
    n_blocks = rows // ATTN_ROWS
    stack = 2 * GRID_W
    lane = lax.broadcasted_iota(jnp.int32, (GRID_W, LANES), 1)
    first = lane < HEAD_DIM

    def stacked_queries(cols):
        pieces = []
        for i in range(ATTN_ROWS):
            q2 = q_ref[i * GRID_W:(i + 1) * GRID_W, cols]
            zero = jnp.zeros_like(q2)
            pieces += [jnp.where(first, q2, zero), jnp.where(first, zero, q2)]
        return jnp.concatenate(pieces, axis=0)

    def scores(lhs, keys):
        return lax.dot_general(lhs, keys, (((1,), (1,)), ((), ())), preferred_element_type=F32)

    def softmax_terms(s):
        p = jnp.exp(s - jnp.max(s, axis=-1, keepdims=True))
        return p.astype(BF16), 1.0 / jnp.sum(p, axis=-1, keepdims=True)

    def write(o, inv, cols):
        o = o * inv
        for i in range(ATTN_ROWS):
            blk = o[i * stack:(i + 1) * stack]
            o_ref[i * GRID_W:(i + 1) * GRID_W, cols] = jnp.where(
                first, blk[0:GRID_W], blk[GRID_W:stack]).astype(BF16)

    is_edge = jnp.logical_or(j == 0, j == n_blocks - 1)

    @pl.when(is_edge)
    def _():
        r0 = j * ATTN_ROWS
        rs = jnp.clip(r0 - NA_WIN_H // 2, 0, rows - NA_WIN_H)
        start = pl.multiple_of(rs * GRID_W, GRID_W)
        nkeys = NA_WIN_H * GRID_W
        for pair in range(N_HEADS // 2):
            cols = slice(pair * LANES, (pair + 1) * LANES)
            hrows = slice(pair * stack, (pair + 1) * stack)
            s = scores(stacked_queries(cols), k_ref[pl.ds(start, nkeys), cols])
            ps, invs = [], []
            for i in range(ATTN_ROWS):
                p, inv = softmax_terms(s[i * stack:(i + 1) * stack] + bias_ref[r0 + i - rs, hrows, :])
                ps.append(p)
                invs.append(inv)
            o = _dot(jnp.concatenate(ps, axis=0), v_ref[pl.ds(start, nkeys), cols])
            write(o, jnp.concatenate(invs, axis=0), cols)

    @pl.when(jnp.logical_not(is_edge))
    def _():
        start = pl.multiple_of((j * ATTN_ROWS - NA_WIN_H // 2) * GRID_W, ATTN_ROWS * GRID_W)
        nkeys = ATTN_KEY_ROWS * GRID_W
        for pair in range(N_HEADS // 2):
            cols = slice(pair * LANES, (pair + 1) * LANES)
            hrows = slice(pair * stack, (pair + 1) * stack)
            s = scores(stacked_queries(cols), k_ref[pl.ds(start, nkeys), cols])
            ps, invs = [], []
            for i in range(ATTN_ROWS):
                lo = i // 2 * 2
                hi = lo + NA_WIN_H + 2 * (i % 2)
                bias = bias_odd_ref[hrows, :] if i % 2 else bias_ref[NA_WIN_H // 2, hrows, :]
                p, inv = softmax_terms(s[i * stack:(i + 1) * stack, lo * GRID_W:hi * GRID_W] + bias)
                parts = [p]
                if lo:
                    parts.insert(0, jnp.zeros((stack, lo * GRID_W), BF16))
                if hi < ATTN_KEY_ROWS:
                    parts.append(jnp.zeros((stack, (ATTN_KEY_ROWS - hi) * GRID_W), BF16))
                ps.append(jnp.concatenate(parts, axis=1) if len(parts) > 1 else p)
                invs.append(inv)
            o = _dot(jnp.concatenate(ps, axis=0), v_ref[pl.ds(start, nkeys), cols])
            write(o, jnp.concatenate(invs, axis=0), cols)


def _attention(q, k, v, bias_tab, bias_odd, batch, seq):
    rows = seq // GRID_W
    qt = ATTN_ROWS * GRID_W
    kv_spec = pl.BlockSpec((seq, ATTN_W), lambda b, r: (0, b), pipeline_mode=pl.Buffered(1))
    return pl.pallas_call(
        functools.partial(_attn_kernel, rows=rows),
        grid=(batch, rows // ATTN_ROWS),
        in_specs=[pl.BlockSpec((qt, ATTN_W), lambda b, r: (r, b)), kv_spec, kv_spec,
                  _const_spec(bias_tab.shape), _const_spec(bias_odd.shape)],
        out_specs=pl.BlockSpec((qt, ATTN_W), lambda b, r: (r, b)),
        out_shape=jax.ShapeDtypeStruct((seq, batch * ATTN_W), BF16),
        compiler_params=_params("parallel", "arbitrary"),
        name="natten",
    )(q, k, v, bias_tab, bias_odd)


def _attn_bias_table(rpb):
    c = jnp.arange(GRID_W)[:, None]
    kc = jnp.arange(GRID_W)[None, :]
    cs = jnp.clip(c - NA_WIN_W // 2, 0, GRID_W - NA_WIN_W)
    inside = (kc >= cs) & (kc < cs + NA_WIN_W)
    dc = kc - c + (NA_WIN_W - 1)
    onehot = (dc[:, :, None] == jnp.arange(2 * NA_WIN_W - 1)[None, None, :]).astype(F32)
    band = jnp.einsum('hyx,ckx->hcyk', rpb.astype(F32), onehot, precision=lax.Precision.HIGHEST)
    band = jnp.where(inside[None, :, None, :], band, MASK_BIAS)
    tabs = [band[:, :, NA_WIN_H - 1 - var:2 * NA_WIN_H - 1 - var, :] for var in range(NA_WIN_H)]
    tab = jnp.stack(tabs, axis=0)
    tab = tab.reshape(NA_WIN_H, N_HEADS * GRID_W, NA_WIN_H * GRID_W)
    pad = jnp.full((N_HEADS * GRID_W, GRID_W), MASK_BIAS, F32)
    tab_odd = jnp.concatenate([pad, tab[NA_WIN_H // 2], pad], axis=1)
    return tab, tab_odd


def _ssm_kernel(*refs, batch, reverse):
    s_cur_ref, s_nxt0_ref, s_nxt1_ref, extra_ref, wb_ref, wc_ref, are_ref, aim_ref, y_ref = refs[:9]
    st_refs = refs[9:9 + SSM_DEPTH]
    u8_ref, y8_ref, carry_ref = refs[9 + SSM_DEPTH:]
    t_chunk = SSM_CHUNK
    rows8 = 2 * batch
    lane = lax.broadcasted_iota(jnp.int32, (t_chunk, LANES), 1)
    lower = lane < (LANES // 2)

    def chunk_rows(pos, n_in_block):
        idx = n_in_block - 1 - pos if reverse else pos
        return slice(idx * t_chunk, (idx + 1) * t_chunk)

    def project_in(src_ref, rows, st_ref):
        for q in range(NQ):
            for b in range(batch):
                u = src_ref[q, rows, b * LANES:(b + 1) * LANES]
                u8_ref[q, pl.ds(b, t_chunk, stride=rows8), :] = jnp.where(lower, u, 0.0)
                u8_ref[q, pl.ds(batch + b, t_chunk, stride=rows8), :] = jnp.where(lower, 0.0, u)
        for q in range(NQ):
            lhs = u8_ref[q].astype(BF16)
            w = wb_ref[q]
            st_ref[:, q * 256:(q + 1) * 256] = _dot(lhs, w[:, 0:256])
            st_ref[:, STATE_COLS + q * 256:STATE_COLS + (q + 1) * 256] = _dot(lhs, w[:, 256:512])

    def recurrence(st_ref, carry):
        a_re = are_ref[...]
        a_im = aim_ref[...]
        x_re, x_im = carry
        for i in range(t_chunk):
            t = t_chunk - 1 - i if reverse else i
            rows = slice(t * rows8, (t + 1) * rows8)
            n_re = a_re * x_re - a_im * x_im + st_ref[rows, 0:STATE_COLS]
            n_im = a_re * x_im + a_im * x_re + st_ref[rows, STATE_COLS:2 * STATE_COLS]
            st_ref[rows, 0:STATE_COLS] = n_re
            st_ref[rows, STATE_COLS:2 * STATE_COLS] = n_im
            x_re, x_im = n_re, n_im
        return x_re, x_im

    def project_out(st_ref, rows):
        for q in range(NQ):
            xr = st_ref[:, q * 256:(q + 1) * 256].astype(BF16)
            xi = st_ref[:, STATE_COLS + q * 256:STATE_COLS + (q + 1) * 256].astype(BF16)
            w = wc_ref[q]
            y8_ref[q] = _dot(xr, w[0:256, :]) + _dot(xi, w[256:512, :])
        for q in range(NQ):
            for b in range(batch):
                cols = slice(b * LANES, (b + 1) * LANES)
                r0 = y8_ref[q, pl.ds(b, t_chunk, stride=rows8), :]
                r1 = y8_ref[q, pl.ds(batch + b, t_chunk, stride=rows8), :]
                if reverse:
                    base = extra_ref[q, rows, cols]
                else:
                    base = extra_ref[q] * s_cur_ref[q, rows, cols]
                y_ref[q, rows, cols] = base + jnp.where(lower, r0, r1)

    def load_carry():
        return carry_ref[:, 0:STATE_COLS], carry_ref[:, STATE_COLS:2 * STATE_COLS]

    def store_carry(carry):
        carry_ref[:, 0:STATE_COLS] = carry[0]
        carry_ref[:, STATE_COLS:2 * STATE_COLS] = carry[1]

    @pl.when(pl.program_id(0) == 0)
    def _():
        project_in(s_cur_ref, chunk_rows(0, SSM_DEPTH), st_refs[0])
        zero = jnp.zeros((rows8, STATE_COLS), F32)
        store_carry(recurrence(st_refs[0], (zero, zero)))
        project_in(s_cur_ref, chunk_rows(1, SSM_DEPTH), st_refs[1])

    carry = load_carry()
    for i in range(SSM_DEPTH):
        project_out(st_refs[i], chunk_rows(i, SSM_DEPTH))
        carry = recurrence(st_refs[(i + 1) % SSM_DEPTH], carry)
        nxt_ref = s_nxt0_ref if i < SSM_DEPTH // 2 else s_nxt1_ref
        project_in(nxt_ref, chunk_rows(i % (SSM_DEPTH // 2), SSM_DEPTH // 2),
                   st_refs[(i + 2) % SSM_DEPTH])
    store_carry(carry)


def _ssm(s_in, extra, wb, wc, a_re, a_im, batch, seq, reverse):
    t = SSM_CHUNK
    rows8 = 2 * batch
    n_steps = seq // (t * SSM_DEPTH)
    half = SSM_DEPTH // 2
    n_half = seq // (t * half)

    def pos(block, n_blocks):
        block = jnp.minimum(block, n_blocks - 1)
        return n_blocks - 1 - block if reverse else block

    cur_spec = pl.BlockSpec((NQ, t * SSM_DEPTH, batch * LANES), lambda k: (0, pos(k, n_steps), 0))
    def ahead_spec(off):
        return pl.BlockSpec((NQ, t * half, batch * LANES),
                            lambda k: (0, pos(2 * k + off, n_half), 0))
    extra_spec = cur_spec if reverse else _const_spec(extra.shape)
    return pl.pallas_call(
        functools.partial(_ssm_kernel, batch=batch, reverse=reverse),
        grid=(n_steps,),
        in_specs=[cur_spec, ahead_spec(1), ahead_spec(2), extra_spec,
                  _const_spec(wb.shape), _const_spec(wc.shape),
                  _const_spec(a_re.shape), _const_spec(a_im.shape)],
        out_specs=cur_spec,
        out_shape=jax.ShapeDtypeStruct((NQ, seq, batch * LANES), F32),
        scratch_shapes=[pltpu.VMEM((t * rows8, 2 * STATE_COLS), F32) for _ in range(SSM_DEPTH)]
                       + [pltpu.VMEM((NQ, t * rows8, LANES), F32),
                          pltpu.VMEM((NQ, t * rows8, LANES), F32),
                          pltpu.VMEM((rows8, 2 * STATE_COLS), F32)],
        input_output_aliases={3: 0} if reverse else {},
        compiler_params=_params("arbitrary"),
        name="s5_scan_bwd" if reverse else "s5_scan_fwd",
    )(s_in, s_in, s_in, extra, wb, wc, a_re, a_im)


def _ssm_weights(lam_re, lam_im, log_dt, b_re, b_im, c_re, c_im, batch):
    lam = lax.complex(jnp.minimum(lam_re, LAMBDA_RE_MAX), lam_im)
    dt = jnp.exp(log_dt)[:, None]
    lam_bar = jnp.exp(lam * dt)
    b_bar = ((lam_bar - 1.0) / lam)[..., None] * lax.complex(b_re, b_im)
    g4, gc, p = SSM_QBLK, SSM_GROUP, SSM_STATE
    eye = jnp.eye(g4, dtype=F32)

    def in_block(x):
        x = x.reshape(NQ, 2, g4, p, gc).transpose(0, 1, 2, 4, 3)
        x = x[:, :, :, :, None, :] * eye[None, None, :, None, :, None]
        return x.reshape(NQ, 2 * g4 * gc, g4 * p)

    wb = jnp.concatenate([in_block(b_bar.real), in_block(b_bar.imag)], axis=-1)

    def out_block(x):
        x = x.reshape(NQ, 2, g4, gc, p).transpose(0, 2, 4, 1, 3)
        x = x[:, :, :, :, None, :] * eye[None, :, None, None, :, None]
        return x.reshape(NQ, g4 * p, 2 * g4 * gc)

    wc = jnp.concatenate([out_block(c_re), -out_block(c_im)], axis=1)

    def decay(x):
        x = x.reshape(NQ, 2, g4 * p).transpose(1, 0, 2).reshape(2, 1, STATE_COLS)
        return jnp.broadcast_to(x, (2, batch, STATE_COLS)).reshape(2 * batch, STATE_COLS)

    return wb.astype(BF16), wc.astype(BF16), decay(lam_bar.real), decay(lam_bar.imag)


def _gelu_tanh(x):
    return 0.5 * x * (1.0 + jnp.tanh(math.sqrt(2.0 / math.pi) * (x + 0.044715 * (x * x * x))))


def _tail_kernel(h_ref, ya_ref, ys_ref, ga_ref, gs_ref, p_ref,
                 gw_ref, gb_ref, wa_ref, ws_ref, wo_ref,
                 fn_ref, wg_ref, wu_ref, wd_ref,
                 pn_ref, pg_ref, pp_ref, on_ref, o_ref):
    y_attn = _dot(ya_ref[...], wa_ref[...])
    y = _gelu_tanh(jnp.concatenate([ys_ref[c] for c in range(NQ)], axis=-1))
    y = y * _sigmoid(_dot(y.astype(BF16), gw_ref[...]) + gb_ref[...])
    y_ssm = _dot(y.astype(BF16), ws_ref[...])
    mix = ga_ref[...].astype(F32) * y_attn + gs_ref[...].astype(F32) * y_ssm
    h = h_ref[...] + _dot(mix.astype(BF16), wo_ref[...])
    h = _ffn_half_step(h, fn_ref, wg_ref, wu_ref, wd_ref)
    gate = _sigmoid(_dot(_rms(h, pn_ref[...]).astype(BF16), pg_ref[...]))
    h = h + _dot(p_ref[...].astype(BF16), pp_ref[...]) * gate
    o_ref[...] = _rms(h, on_ref[...])


def _tail(h, y_attn, y_scan, g_attn, g_ssm, p, weights, batch, seq):
    t = TOKEN_TILE
    def spec(c):
        return pl.BlockSpec((t, c), lambda b, i: (i, b))
    return pl.pallas_call(
        _tail_kernel,
        grid=(batch, seq // t),
        in_specs=[spec(D_MODEL), spec(ATTN_W),
                  pl.BlockSpec((NQ, t, LANES), lambda b, i: (0, i, b)),
                  spec(D_MODEL), spec(D_MODEL),
                  pl.BlockSpec((None, t, p.shape[-1]), lambda b, i: (b, i, 0))]
                 + [_const_spec(w.shape) for w in weights],
        out_specs=pl.BlockSpec((None, t, D_MODEL), lambda b, i: (b, i, 0)),
        out_shape=jax.ShapeDtypeStruct((batch, seq, D_MODEL), F32),
        compiler_params=_params("parallel", "parallel"),
        name="tail",
    )(h, y_attn, y_scan, g_attn, g_ssm, p, *weights)


def kernel(x, p, ffn1_norm, ffn1_w_gate, ffn1_w_up, ffn1_w_down, mix_norm, w_in, na_rpb, ssm_lam_re, ssm_lam_im, ssm_log_dt, ssm_b_re, ssm_b_im, ssm_c_re, ssm_c_im, ssm_d, ssm_glu_w, ssm_glu_b, w_attn_out, w_ssm_out, w_out, ffn2_norm, ffn2_w_gate, ffn2_w_up, ffn2_w_down, ple_norm, ple_w_gate, ple_w_proj, final_norm):
    batch, seq, _ = x.shape
    assert p.shape[0] == 1, "single-layer block"
    i = 0
    row = lambda v: v.reshape(1, -1)
    bf = lambda w: w.astype(BF16)
    h = _ffn(x, row(ffn1_norm[i]), bf(ffn1_w_gate[i]), bf(ffn1_w_up[i]), bf(ffn1_w_down[i]),
             batch, seq)
    q, k, v, s_in, g_attn, g_ssm = _inproj(h, row(mix_norm[i]), bf(w_in[i]), batch, seq)
    y_attn = _attention(q, k, v, *_attn_bias_table(na_rpb[i]), batch, seq)
    y_scan = ssm_d[i].reshape(NQ, 1, LANES)
    for d in range(2):
        wb, wc, a_re, a_im = _ssm_weights(
            ssm_lam_re[i, d], ssm_lam_im[i, d], ssm_log_dt[i, d], ssm_b_re[i, d], ssm_b_im[i, d],
            ssm_c_re[i, d], ssm_c_im[i, d], batch)
        y_scan = _ssm(s_in, y_scan, wb, wc, a_re, a_im, batch, seq, reverse=d == 1)
    weights = [bf(ssm_glu_w[i]), row(ssm_glu_b[i]), bf(w_attn_out[i]), bf(w_ssm_out[i]), bf(w_out[i]),
               row(ffn2_norm[i]), bf(ffn2_w_gate[i]), bf(ffn2_w_up[i]), bf(ffn2_w_down[i]),
               row(ple_norm[i]), bf(ple_w_gate[i]), bf(ple_w_proj[i]), row(final_norm)]
    return _tail(h, y_attn, y_scan, g_attn, g_ssm, p[i], weights, batch, seq)
```

```python
import functools
import math

import jax
import jax.numpy as jnp
from jax import lax
from jax.experimental import pallas as pl
from jax.experimental.pallas import tpu as pltpu

D_MODEL = 1024
GRID_W = 64
N_HEADS = 8
HEAD_DIM = 64
ATTN_W = N_HEADS * HEAD_DIM
NA_WIN_H = 8
NA_WIN_W = 16
SSM_W = 512
SSM_GROUP = 16
SSM_GROUPS = SSM_W // SSM_GROUP
SSM_STATE = 64
LAMBDA_RE_MAX = -1e-4
D_FF = 2816
RMS_EPS = 1e-6

LANES = 128
VMEM_LIMIT = 56 * 1024 * 1024
MASK_BIAS = -1e30

TOKEN_TILE = 512
TIME_TILE = 128
ATTN_ROWS = NA_WIN_H // 2
ATTN_KEY_ROWS = ATTN_ROWS + NA_WIN_H
FF_CHUNKS = ((0, 1536), (1536, 1280))
SSM_CHUNK = 64
SSM_DEPTH = 4
SSM_QBLK = 4
NQ = SSM_W // LANES
STATE_COLS = SSM_GROUPS // 2 * SSM_STATE

BF16 = jnp.bfloat16
F32 = jnp.float32


def _dot(a, b):
    return jnp.dot(a, b, preferred_element_type=F32)


def _rms(x, g):
    ms = jnp.mean(x * x, axis=-1, keepdims=True)
    return x * lax.rsqrt(ms + RMS_EPS) * g


def _sigmoid(x):
    return 1.0 / (1.0 + jnp.exp(-x))


def _const_spec(shape):
    nd = len(shape)
    return pl.BlockSpec(shape, lambda *_: (0,) * nd, pipeline_mode=pl.Buffered(1))


def _params(*sem):
    return pltpu.CompilerParams(dimension_semantics=sem, vmem_limit_bytes=VMEM_LIMIT)


def _ffn_half_step(x, g_ref, wg_ref, wu_ref, wd_ref):
    xn = _rms(x, g_ref[...]).astype(BF16)
    acc = None
    for start, size in FF_CHUNKS:
        gate = _dot(xn, wg_ref[:, start:start + size])
        up = _dot(xn, wu_ref[:, start:start + size])
        act = (gate * _sigmoid(gate) * up).astype(BF16)
        part = _dot(act, wd_ref[start:start + size, :])
        acc = part if acc is None else acc + part
    return x + 0.5 * acc


def _ffn_kernel(x_ref, g_ref, wg_ref, wu_ref, wd_ref, o_ref):
    o_ref[...] = _ffn_half_step(x_ref[...], g_ref, wg_ref, wu_ref, wd_ref)


def _ffn(x, norm_g, wg, wu, wd, batch, seq):
    t = TOKEN_TILE
    return pl.pallas_call(
        _ffn_kernel,
        grid=(batch, seq // t),
        in_specs=[pl.BlockSpec((None, t, D_MODEL), lambda b, i: (b, i, 0)),
                  _const_spec((1, D_MODEL)), _const_spec(wg.shape),
                  _const_spec(wu.shape), _const_spec(wd.shape)],
        out_specs=pl.BlockSpec((t, D_MODEL), lambda b, i: (i, b)),
        out_shape=jax.ShapeDtypeStruct((seq, batch * D_MODEL), F32),
        compiler_params=_params("parallel", "parallel"),
        name="ffn",
    )(x, norm_g, wg, wu, wd)


def _stack_batches(ref, batch, width):
    return jnp.concatenate([ref[:, b * width:(b + 1) * width] for b in range(batch)], axis=0)


def _unstack_batches(ref, x, batch, width):
    t = x.shape[0] // batch
    for b in range(batch):
        ref[:, b * width:(b + 1) * width] = x[b * t:(b + 1) * t]


def _inproj_kernel(h_ref, g_ref, w_ref, q_ref, k_ref, v_ref, s_ref, u8_ref, ga_ref, gs_ref,
                   u8_scr, *, batch):
    t = TIME_TILE
    rows8 = 2 * batch
    u = _rms(_stack_batches(h_ref, batch, D_MODEL), g_ref[...]).astype(BF16)
    a = ATTN_W
    _unstack_batches(q_ref, (_dot(u, w_ref[:, 0:a]) * (HEAD_DIM ** -0.5)).astype(BF16), batch, a)
    _unstack_batches(k_ref, _dot(u, w_ref[:, a:2 * a]).astype(BF16), batch, a)
    _unstack_batches(v_ref, _dot(u, w_ref[:, 2 * a:3 * a]).astype(BF16), batch, a)
    s0 = 3 * a
    s_in = _dot(u, w_ref[:, s0:s0 + SSM_W])
    _unstack_batches(s_ref, s_in, batch, SSM_W)
    lane = lax.broadcasted_iota(jnp.int32, (t, LANES), 1)
    lower = lane < (LANES // 2)
    for q in range(NQ):
        for b in range(batch):
            piece = s_in[b * t:(b + 1) * t, q * LANES:(q + 1) * LANES]
            u8_scr[q, pl.ds(b, t, stride=rows8), :] = jnp.where(lower, piece, 0.0)
            u8_scr[q, pl.ds(batch + b, t, stride=rows8), :] = jnp.where(lower, 0.0, piece)
    u8_ref[...] = u8_scr[...].astype(BF16)
    g0 = s0 + SSM_W
    _unstack_batches(ga_ref, _sigmoid(_dot(u, w_ref[:, g0:g0 + D_MODEL])).astype(BF16),
                     batch, D_MODEL)
    _unstack_batches(gs_ref, _sigmoid(_dot(u, w_ref[:, g0 + D_MODEL:g0 + 2 * D_MODEL])).astype(BF16),
                     batch, D_MODEL)


def _inproj(h, norm_g, w_in, batch, seq):
    t = TIME_TILE
    rows8 = 2 * batch
    def spec(c):
        return pl.BlockSpec((t, batch * c), lambda i: (i, 0))
    def shape(c, dt):
        return jax.ShapeDtypeStruct((seq, batch * c), dt)
    return pl.pallas_call(
        functools.partial(_inproj_kernel, batch=batch),
        grid=(seq // t,),
        in_specs=[spec(D_MODEL), _const_spec((1, D_MODEL)), _const_spec(w_in.shape)],
        out_specs=[spec(ATTN_W), spec(ATTN_W), spec(ATTN_W), spec(SSM_W),
                   pl.BlockSpec((NQ, t * rows8, LANES), lambda i: (0, i, 0)),
                   spec(D_MODEL), spec(D_MODEL)],
        out_shape=[shape(ATTN_W, BF16), shape(ATTN_W, BF16), shape(ATTN_W, BF16),
                   shape(SSM_W, F32),
                   jax.ShapeDtypeStruct((NQ, seq * rows8, LANES), BF16),
                   shape(D_MODEL, BF16), shape(D_MODEL, BF16)],
        scratch_shapes=[pltpu.VMEM((NQ, t * rows8, LANES), F32)],
        compiler_params=_params("parallel"),
        name="inproj",
    )(h, norm_g, w_in)


def _attn_kernel(q_ref, k_ref, v_ref, bias_ref, bias_odd_ref, o_ref, *, rows):
    j = pl.program_id(1)
    n_blocks = rows // ATTN_ROWS
    stack = 2 * GRID_W
    lane = lax.broadcasted_iota(jnp.int32, (GRID_W, LANES), 1)
    first = lane < HEAD_DIM

    def stacked_queries(cols):
        pieces = []
        for i in range(ATTN_ROWS):
            q2 = q_ref[i * GRID_W:(i + 1) * GRID_W, cols]
            zero = jnp.zeros_like(q2)
            pieces += [jnp.where(first, q2, zero), jnp.where(first, zero, q2)]
        return jnp.concatenate(pieces, axis=0)

    def scores(lhs, keys):
        return lax.dot_general(lhs, keys, (((1,), (1,)), ((), ())), preferred_element_type=F32)

    def softmax_terms(s):
        p = jnp.exp(s - jnp.max(s, axis=-1, keepdims=True))
        return p.astype(BF16), 1.0 / jnp.sum(p, axis=-1, keepdims=True)

    def write(o, inv, cols):
        o = o * inv
        for i in range(ATTN_ROWS):
            blk = o[i * stack:(i + 1) * stack]
            o_ref[i * GRID_W:(i + 1) * GRID_W, cols] = jnp.where(
                first, blk[0:GRID_W], blk[GRID_W:stack]).astype(BF16)

    is_edge = jnp.logical_or(j == 0, j == n_blocks - 1)

    @pl.when(is_edge)
    def _():
        r0 = j * ATTN_ROWS
        rs = jnp.clip(r0 - NA_WIN_H // 2, 0, rows - NA_WIN_H)
        start = pl.multiple_of(rs * GRID_W, GRID_W)
        nkeys = NA_WIN_H * GRID_W
        for pair in range(N_HEADS // 2):
            cols = slice(pair * LANES, (pair + 1) * LANES)
            hrows = slice(pair * stack, (pair + 1) * stack)
            s = scores(stacked_queries(cols), k_ref[pl.ds(start, nkeys), cols])
            ps, invs = [], []
            for i in range(ATTN_ROWS):
                p, inv = softmax_terms(s[i * stack:(i + 1) * stack] + bias_ref[r0 + i - rs, hrows, :])
                ps.append(p)
                invs.append(inv)
            o = _dot(jnp.concatenate(ps, axis=0), v_ref[pl.ds(start, nkeys), cols])
            write(o, jnp.concatenate(invs, axis=0), cols)

    @pl.when(jnp.logical_not(is_edge))
    def _():
        start = pl.multiple_of((j * ATTN_ROWS - NA_WIN_H // 2) * GRID_W, ATTN_ROWS * GRID_W)
        nkeys = ATTN_KEY_ROWS * GRID_W
        for pair in range(N_HEADS // 2):
            cols = slice(pair * LANES, (pair + 1) * LANES)
            hrows = slice(pair * stack, (pair + 1) * stack)
            s = scores(stacked_queries(cols), k_ref[pl.ds(start, nkeys), cols])
            ps, invs = [], []
            for i in range(ATTN_ROWS):
                lo = i // 2 * 2
                hi = lo + NA_WIN_H + 2 * (i % 2)
                bias = bias_odd_ref[hrows, :] if i % 2 else bias_ref[NA_WIN_H // 2, hrows, :]
                p, inv = softmax_terms(s[i * stack:(i + 1) * stack, lo * GRID_W:hi * GRID_W] + bias)
                parts = [p]
                if lo:
                    parts.insert(0, jnp.zeros((stack, lo * GRID_W), BF16))
                if hi < ATTN_KEY_ROWS:
                    parts.append(jnp.zeros((stack, (ATTN_KEY_ROWS - hi) * GRID_W), BF16))
                ps.append(jnp.concatenate(parts, axis=1) if len(parts) > 1 else p)
                invs.append(inv)
            o = _dot(jnp.concatenate(ps, axis=0), v_ref[pl.ds(start, nkeys), cols])
            write(o, jnp.concatenate(invs, axis=0), cols)


def _attention(q, k, v, bias_tab, bias_odd, batch, seq):
    rows = seq // GRID_W
    qt = ATTN_ROWS * GRID_W
    kv_spec = pl.BlockSpec((seq, ATTN_W), lambda b, r: (0, b))
    return pl.pallas_call(
        functools.partial(_attn_kernel, rows=rows),
        grid=(batch, rows // ATTN_ROWS),
        in_specs=[pl.BlockSpec((qt, ATTN_W), lambda b, r: (r, b)), kv_spec, kv_spec,
                  _const_spec(bias_tab.shape), _const_spec(bias_odd.shape)],
        out_specs=pl.BlockSpec((qt, ATTN_W), lambda b, r: (r, b)),
        out_shape=jax.ShapeDtypeStruct((seq, batch * ATTN_W), BF16),
        compiler_params=_params("parallel", "arbitrary"),
        name="natten",
    )(q, k, v, bias_tab, bias_odd)


def _attn_bias_table(rpb):
    c = jnp.arange(GRID_W)[:, None]
    kc = jnp.arange(GRID_W)[None, :]
    cs = jnp.clip(c - NA_WIN_W // 2, 0, GRID_W - NA_WIN_W)
    inside = (kc >= cs) & (kc < cs + NA_WIN_W)
    dc = kc - c + (NA_WIN_W - 1)
    onehot = (dc[:, :, None] == jnp.arange(2 * NA_WIN_W - 1)[None, None, :]).astype(F32)
    band = jnp.einsum('hyx,ckx->hcyk', rpb.astype(F32), onehot, precision=lax.Precision.HIGHEST)
    band = jnp.where(inside[None, :, None, :], band, MASK_BIAS)
    tabs = [band[:, :, NA_WIN_H - 1 - var:2 * NA_WIN_H - 1 - var, :] for var in range(NA_WIN_H)]
    tab = jnp.stack(tabs, axis=0)
    tab = tab.reshape(NA_WIN_H, N_HEADS * GRID_W, NA_WIN_H * GRID_W)
    pad = jnp.full((N_HEADS * GRID_W, GRID_W), MASK_BIAS, F32)
    tab_odd = jnp.concatenate([pad, tab[NA_WIN_H // 2], pad], axis=1)
    return tab, tab_odd


def _ssm_kernel(*refs, batch, reverse):
    n_in = 8 if reverse else 7
    u_cur_ref, u_nxt0_ref, u_nxt1_ref = refs[:3]
    prev_ref = refs[3] if reverse else None
    wb_ref, wc_ref, are_ref, aim_ref, y_ref = refs[n_in - 4:n_in + 1]
    st_refs = refs[n_in + 1:n_in + 1 + SSM_DEPTH]
    carry_ref = refs[n_in + 1 + SSM_DEPTH]
    rows8 = 2 * batch
    chunk = SSM_CHUNK * rows8

    def chunk_rows(pos, n_in_block):
        idx = n_in_block - 1 - pos if reverse else pos
        return slice(idx * chunk, (idx + 1) * chunk)

    def project_in(src_ref, rows, st_ref):
        for q in range(NQ):
            lhs = src_ref[q, rows, :]
            w = wb_ref[q]
            st_ref[:, q * 256:(q + 1) * 256] = _dot(lhs, w[:, 0:256])
            st_ref[:, STATE_COLS + q * 256:STATE_COLS + (q + 1) * 256] = _dot(lhs, w[:, 256:512])

    def recurrence(st_ref, carry):
        a_re = are_ref[...]
        a_im = aim_ref[...]
        x_re, x_im = carry
        for i in range(SSM_CHUNK):
            t = SSM_CHUNK - 1 - i if reverse else i
            rows = slice(t * rows8, (t + 1) * rows8)
            n_re = a_re * x_re - a_im * x_im + st_ref[rows, 0:STATE_COLS]
            n_im = a_re * x_im + a_im * x_re + st_ref[rows, STATE_COLS:2 * STATE_COLS]
            st_ref[rows, 0:STATE_COLS] = n_re
            st_ref[rows, STATE_COLS:2 * STATE_COLS] = n_im
            x_re, x_im = n_re, n_im
        return x_re, x_im

    def project_out(st_ref, rows):
        for q in range(NQ):
            xr = st_ref[:, q * 256:(q + 1) * 256].astype(BF16)
            xi = st_ref[:, STATE_COLS + q * 256:STATE_COLS + (q + 1) * 256].astype(BF16)
            w = wc_ref[q]
            y = _dot(xr, w[0:256, :]) + _dot(xi, w[256:512, :])
            if reverse:
                y = y + prev_ref[q, rows, :]
            y_ref[q, rows, :] = y

    def load_carry():
        return carry_ref[:, 0:STATE_COLS], carry_ref[:, STATE_COLS:2 * STATE_COLS]

    def store_carry(carry):
        carry_ref[:, 0:STATE_COLS] = carry[0]
        carry_ref[:, STATE_COLS:2 * STATE_COLS] = carry[1]

    @pl.when(pl.program_id(0) == 0)
    def _():
        project_in(u_cur_ref, chunk_rows(0, SSM_DEPTH), st_refs[0])
        zero = jnp.zeros((rows8, STATE_COLS), F32)
        store_carry(recurrence(st_refs[0], (zero, zero)))
        project_in(u_cur_ref, chunk_rows(1, SSM_DEPTH), st_refs[1])

    carry = load_carry()
    for i in range(SSM_DEPTH):
        project_out(st_refs[i], chunk_rows(i, SSM_DEPTH))
        carry = recurrence(st_refs[(i + 1) % SSM_DEPTH], carry)
        nxt_ref = u_nxt0_ref if i < SSM_DEPTH // 2 else u_nxt1_ref
        project_in(nxt_ref, chunk_rows(i % (SSM_DEPTH // 2), SSM_DEPTH // 2),
                   st_refs[(i + 2) % SSM_DEPTH])
    store_carry(carry)


def _ssm(u8, prev, wb, wc, a_re, a_im, batch, seq, reverse):
    rows8 = 2 * batch
    chunk = SSM_CHUNK * rows8
    n_steps = seq // (SSM_CHUNK * SSM_DEPTH)
    half = SSM_DEPTH // 2
    n_half = seq // (SSM_CHUNK * half)

    def pos(block, n_blocks):
        block = jnp.minimum(block, n_blocks - 1)
        return n_blocks - 1 - block if reverse else block

    cur_spec = pl.BlockSpec((NQ, chunk * SSM_DEPTH, LANES), lambda k: (0, pos(k, n_steps), 0))
    def ahead_spec(off):
        return pl.BlockSpec((NQ, chunk * half, LANES), lambda k: (0, pos(2 * k + off, n_half), 0))
    operands = [u8, u8, u8] + ([prev] if reverse else []) + [wb, wc, a_re, a_im]
    in_specs = ([cur_spec, ahead_spec(1), ahead_spec(2)] + ([cur_spec] if reverse else [])
                + [_const_spec(w.shape) for w in (wb, wc, a_re, a_im)])
    return pl.pallas_call(
        functools.partial(_ssm_kernel, batch=batch, reverse=reverse),
        grid=(n_steps,),
        in_specs=in_specs,
        out_specs=cur_spec,
        out_shape=jax.ShapeDtypeStruct((NQ, seq * rows8, LANES), F32),
        scratch_shapes=[pltpu.VMEM((chunk, 2 * STATE_COLS), F32) for _ in range(SSM_DEPTH)]
                       + [pltpu.VMEM((rows8, 2 * STATE_COLS), F32)],
        input_output_aliases={3: 0} if reverse else {},
        compiler_params=_params("arbitrary"),
        name="s5_scan_bwd" if reverse else "s5_scan_fwd",
    )(*operands)


def _ssm_weights(lam_re, lam_im, log_dt, b_re, b_im, c_re, c_im, batch):
    lam = lax.complex(jnp.minimum(lam_re, LAMBDA_RE_MAX), lam_im)
    dt = jnp.exp(log_dt)[:, None]
    lam_bar = jnp.exp(lam * dt)
    b_bar = ((lam_bar - 1.0) / lam)[..., None] * lax.complex(b_re, b_im)
    g4, gc, p = SSM_QBLK, SSM_GROUP, SSM_STATE
    eye = jnp.eye(g4, dtype=F32)

    def in_block(x):
        x = x.reshape(NQ, 2, g4, p, gc).transpose(0, 1, 2, 4, 3)
        x = x[:, :, :, :, None, :] * eye[None, None, :, None, :, None]
        return x.reshape(NQ, 2 * g4 * gc, g4 * p)

    wb = jnp.concatenate([in_block(b_bar.real), in_block(b_bar.imag)], axis=-1)

    def out_block(x):
        x = x.reshape(NQ, 2, g4, gc, p).transpose(0, 2, 4, 1, 3)
        x = x[:, :, :, :, None, :] * eye[None, :, None, None, :, None]
        return x.reshape(NQ, g4 * p, 2 * g4 * gc)

    wc = jnp.concatenate([out_block(c_re), -out_block(c_im)], axis=1)

    def decay(x):
        x = x.reshape(NQ, 2, g4 * p).transpose(1, 0, 2).reshape(2, 1, STATE_COLS)
        return jnp.broadcast_to(x, (2, batch, STATE_COLS)).reshape(2 * batch, STATE_COLS)

    return wb.astype(BF16), wc.astype(BF16), decay(lam_bar.real), decay(lam_bar.imag)


def _gelu_tanh(x):
    return 0.5 * x * (1.0 + jnp.tanh(math.sqrt(2.0 / math.pi) * (x + 0.044715 * (x * x * x))))


def _tail_kernel(h_ref, ya_ref, s_ref, y8_ref, ga_ref, gs_ref, p_ref,
                 d_ref, gw_ref, gb_ref, wa_ref, ws_ref, wo_ref,
                 fn_ref, wg_ref, wu_ref, wd_ref,
                 pn_ref, pg_ref, pp_ref, on_ref, o_ref, *, batch):
    t = TIME_TILE
    rows8 = 2 * batch
    y_attn = _dot(_stack_batches(ya_ref, batch, ATTN_W), wa_ref[...])
    lane = lax.broadcasted_iota(jnp.int32, (t, LANES), 1)
    lower = lane < (LANES // 2)
    y_scan = jnp.concatenate(
        [jnp.concatenate(
            [jnp.where(lower, y8_ref[q, pl.ds(b, t, stride=rows8), :],
                       y8_ref[q, pl.ds(batch + b, t, stride=rows8), :]) for q in range(NQ)], axis=1)
         for b in range(batch)], axis=0)
    y = _gelu_tanh(d_ref[...] * _stack_batches(s_ref, batch, SSM_W) + y_scan)
    y = y * _sigmoid(_dot(y.astype(BF16), gw_ref[...]) + gb_ref[...])
    y_ssm = _dot(y.astype(BF16), ws_ref[...])
    mix = (_stack_batches(ga_ref, batch, D_MODEL).astype(F32) * y_attn
           + _stack_batches(gs_ref, batch, D_MODEL).astype(F32) * y_ssm)
    h = _stack_batches(h_ref, batch, D_MODEL) + _dot(mix.astype(BF16), wo_ref[...])
    h = _ffn_half_step(h, fn_ref, wg_ref, wu_ref, wd_ref)
    gate = _sigmoid(_dot(_rms(h, pn_ref[...]).astype(BF16), pg_ref[...]))
    pe = jnp.concatenate([p_ref[b] for b in range(batch)], axis=0).astype(BF16)
    h = h + _dot(pe, pp_ref[...]) * gate
    out = _rms(h, on_ref[...])
    for b in range(batch):
        o_ref[b] = out[b * t:(b + 1) * t]


def _tail(h, y_attn, s_in, y8, g_attn, g_ssm, p, weights, batch, seq):
    t = TIME_TILE
    rows8 = 2 * batch
    def spec(c):
        return pl.BlockSpec((t, batch * c), lambda i: (i, 0))
    return pl.pallas_call(
        functools.partial(_tail_kernel, batch=batch),
        grid=(seq // t,),
        in_specs=[spec(D_MODEL), spec(ATTN_W), spec(SSM_W),
                  pl.BlockSpec((NQ, t * rows8, LANES), lambda i: (0, i, 0)),
                  spec(D_MODEL), spec(D_MODEL),
                  pl.BlockSpec((batch, t, p.shape[-1]), lambda i: (0, i, 0))]
                 + [_const_spec(w.shape) for w in weights],
        out_specs=pl.BlockSpec((batch, t, D_MODEL), lambda i: (0, i, 0)),
        out_shape=jax.ShapeDtypeStruct((batch, seq, D_MODEL), F32),
        compiler_params=_params("parallel"),
        name="tail",
    )(h, y_attn, s_in, y8, g_attn, g_ssm, p, *weights)


def kernel(x, p, ffn1_norm, ffn1_w_gate, ffn1_w_up, ffn1_w_down, mix_norm, w_in, na_rpb, ssm_lam_re, ssm_lam_im, ssm_log_dt, ssm_b_re, ssm_b_im, ssm_c_re, ssm_c_im, ssm_d, ssm_glu_w, ssm_glu_b, w_attn_out, w_ssm_out, w_out, ffn2_norm, ffn2_w_gate, ffn2_w_up, ffn2_w_down, ple_norm, ple_w_gate, ple_w_proj, final_norm):
    batch, seq, _ = x.shape
    assert p.shape[0] == 1, "single-layer block"
    i = 0
    row = lambda v: v.reshape(1, -1)
    bf = lambda w: w.astype(BF16)
    h = _ffn(x, row(ffn1_norm[i]), bf(ffn1_w_gate[i]), bf(ffn1_w_up[i]), bf(ffn1_w_down[i]),
             batch, seq)
    q, k, v, s_in, u8, g_attn, g_ssm = _inproj(h, row(mix_norm[i]), bf(w_in[i]), batch, seq)
    y_attn = _attention(q, k, v, *_attn_bias_table(na_rpb[i]), batch, seq)
    y8 = None
    for d in range(2):
        wb, wc, a_re, a_im = _ssm_weights(
            ssm_lam_re[i, d], ssm_lam_im[i, d], ssm_log_dt[i, d], ssm_b_re[i, d], ssm_b_im[i, d],
            ssm_c_re[i, d], ssm_c_im[i, d], batch)
        y8 = _ssm(u8, y8, wb, wc, a_re, a_im, batch, seq, reverse=d == 1)
    weights = [row(ssm_d[i]), bf(ssm_glu_w[i]), row(ssm_glu_b[i]),
               bf(w_attn_out[i]), bf(w_ssm_out[i]), bf(w_out[i]),
               row(ffn2_norm[i]), bf(ffn2_w_gate[i]), bf(ffn2_w_up[i]), bf(ffn2_w_down[i]),
               row(ple_norm[i]), bf(ple_w_gate[i]), bf(ple_w_proj[i]), row(final_norm)]
    return _tail(h, y_attn, s_in, y8, g_attn, g_ssm, p[i], weights, batch, seq)
```

```python
import functools
import math

import jax
import jax.numpy as jnp
from jax import lax
from jax.experimental import pallas as pl
from jax.experimental.pallas import tpu as pltpu

D_MODEL = 1024
GRID_W = 64
N_HEADS = 8
HEAD_DIM = 64
ATTN_W = N_HEADS * HEAD_DIM
NA_WIN_H = 8
NA_WIN_W = 16
SSM_W = 512
SSM_GROUP = 16
SSM_GROUPS = SSM_W // SSM_GROUP
SSM_STATE = 64
LAMBDA_RE_MAX = -1e-4
D_FF = 2816
RMS_EPS = 1e-6

LANES = 128
VMEM_LIMIT = 56 * 1024 * 1024
MASK_BIAS = -1e30

TOKEN_TILE = 1024
INPROJ_TILE = 256
TIME_TILE = 128
ATTN_ROWS = NA_WIN_H // 2
ATTN_KEY_ROWS = ATTN_ROWS + NA_WIN_H
FF_CHUNKS = ((0, 1536), (1536, 1280))
SSM_CHUNK = 64
SSM_DEPTH = 4
SSM_QBLK = 4
NQ = SSM_W // LANES
STATE_COLS = SSM_GROUPS // 2 * SSM_STATE

BF16 = jnp.bfloat16
F32 = jnp.float32


def _dot(a, b):
    return jnp.dot(a, b, preferred_element_type=F32)


def _rms(x, g):
    ms = jnp.mean(x * x, axis=-1, keepdims=True)
    return x * lax.rsqrt(ms + RMS_EPS) * g


def _sigmoid(x):
    return 1.0 / (1.0 + jnp.exp(-x))


def _const_spec(shape):
    nd = len(shape)
    return pl.BlockSpec(shape, lambda *_: (0,) * nd, pipeline_mode=pl.Buffered(1))


def _params(*sem):
    return pltpu.CompilerParams(dimension_semantics=sem, vmem_limit_bytes=VMEM_LIMIT)


def _ffn_half_step(x, g_ref, wg_ref, wu_ref, wd_ref):
    xn = _rms(x, g_ref[...]).astype(BF16)
    acc = None
    for start, size in FF_CHUNKS:
        gate = _dot(xn, wg_ref[:, start:start + size])
        up = _dot(xn, wu_ref[:, start:start + size])
        act = (gate * _sigmoid(gate) * up).astype(BF16)
        part = _dot(act, wd_ref[start:start + size, :])
        acc = part if acc is None else acc + part
    return x + 0.5 * acc


def _ffn_kernel(x_ref, g_ref, wg_ref, wu_ref, wd_ref, o_ref):
    o_ref[...] = _ffn_half_step(x_ref[...], g_ref, wg_ref, wu_ref, wd_ref)


def _ffn(x, norm_g, wg, wu, wd, batch, seq):
    t = TOKEN_TILE
    return pl.pallas_call(
        _ffn_kernel,
        grid=(batch, seq // t),
        in_specs=[pl.BlockSpec((None, t, D_MODEL), lambda b, i: (b, i, 0)),
                  _const_spec((1, D_MODEL)), _const_spec(wg.shape),
                  _const_spec(wu.shape), _const_spec(wd.shape)],
        out_specs=pl.BlockSpec((t, D_MODEL), lambda b, i: (i, b)),
        out_shape=jax.ShapeDtypeStruct((seq, batch * D_MODEL), F32),
        compiler_params=_params("parallel", "parallel"),
        name="ffn",
    )(x, norm_g, wg, wu, wd)


def _stack_batches(ref, batch, width):
    return jnp.concatenate([ref[:, b * width:(b + 1) * width] for b in range(batch)], axis=0)


def _unstack_batches(ref, x, batch, width):
    t = x.shape[0] // batch
    for b in range(batch):
        ref[:, b * width:(b + 1) * width] = x[b * t:(b + 1) * t]


def _inproj_kernel(h_ref, g_ref, w_ref, q_ref, k_ref, v_ref, s_ref, u8_ref, ga_ref, gs_ref,
                   u8_scr, *, batch):
    t = INPROJ_TILE
    rows8 = 2 * batch
    u = _rms(_stack_batches(h_ref, batch, D_MODEL), g_ref[...]).astype(BF16)
    a = ATTN_W
    _unstack_batches(q_ref, (_dot(u, w_ref[:, 0:a]) * (HEAD_DIM ** -0.5)).astype(BF16), batch, a)
    _unstack_batches(k_ref, _dot(u, w_ref[:, a:2 * a]).astype(BF16), batch, a)
    _unstack_batches(v_ref, _dot(u, w_ref[:, 2 * a:3 * a]).astype(BF16), batch, a)
    s0 = 3 * a
    s_in = _dot(u, w_ref[:, s0:s0 + SSM_W])
    _unstack_batches(s_ref, s_in, batch, SSM_W)
    lane = lax.broadcasted_iota(jnp.int32, (t, LANES), 1)
    lower = lane < (LANES // 2)
    for q in range(NQ):
        for b in range(batch):
            piece = s_in[b * t:(b + 1) * t, q * LANES:(q + 1) * LANES]
            u8_scr[q, pl.ds(b, t, stride=rows8), :] = jnp.where(lower, piece, 0.0)
            u8_scr[q, pl.ds(batch + b, t, stride=rows8), :] = jnp.where(lower, 0.0, piece)
    u8_ref[...] = u8_scr[...].astype(BF16)
    g0 = s0 + SSM_W
    _unstack_batches(ga_ref, _sigmoid(_dot(u, w_ref[:, g0:g0 + D_MODEL])).astype(BF16),
                     batch, D_MODEL)
    _unstack_batches(gs_ref, _sigmoid(_dot(u, w_ref[:, g0 + D_MODEL:g0 + 2 * D_MODEL])).astype(BF16),
                     batch, D_MODEL)


def _inproj(h, norm_g, w_in, batch, seq):
    t = INPROJ_TILE
    rows8 = 2 * batch
    def spec(c):
        return pl.BlockSpec((t, batch * c), lambda i: (i, 0))
    def shape(c, dt):
        return jax.ShapeDtypeStruct((seq, batch * c), dt)
    return pl.pallas_call(
        functools.partial(_inproj_kernel, batch=batch),
        grid=(seq // t,),
        in_specs=[spec(D_MODEL), _const_spec((1, D_MODEL)), _const_spec(w_in.shape)],
        out_specs=[spec(ATTN_W), spec(ATTN_W), spec(ATTN_W), spec(SSM_W),
                   pl.BlockSpec((NQ, t * rows8, LANES), lambda i: (0, i, 0)),
                   spec(D_MODEL), spec(D_MODEL)],
        out_shape=[shape(ATTN_W, BF16), shape(ATTN_W, BF16), shape(ATTN_W, BF16),
                   shape(SSM_W, F32),
                   jax.ShapeDtypeStruct((NQ, seq * rows8, LANES), BF16),
                   shape(D_MODEL, BF16), shape(D_MODEL, BF16)],
        scratch_shapes=[pltpu.VMEM((NQ, t * rows8, LANES), F32)],
        compiler_params=_params("parallel"),
        name="inproj",
    )(h, norm_g, w_in)


def _attn_kernel(q_ref, k_ref, v_ref, bias_ref, bias_odd_ref, o_ref, *, rows):
    j = pl.program_id(1)
    n_blocks = rows // ATTN_ROWS
    stack = 2 * GRID_W
    lane = lax.broadcasted_iota(jnp.int32, (GRID_W, LANES), 1)
    first = lane < HEAD_DIM

    def stacked_queries(cols):
        pieces = []
        for i in range(ATTN_ROWS):
            q2 = q_ref[i * GRID_W:(i + 1) * GRID_W, cols]
            zero = jnp.zeros_like(q2)
            pieces += [jnp.where(first, q2, zero), jnp.where(first, zero, q2)]
        return jnp.concatenate(pieces, axis=0)

    def scores(lhs, keys):
        return lax.dot_general(lhs, keys, (((1,), (1,)), ((), ())), preferred_element_type=F32)

    def softmax_terms(s):
        p = jnp.exp(s - jnp.max(s, axis=-1, keepdims=True))
        return p.astype(BF16), 1.0 / jnp.sum(p, axis=-1, keepdims=True)

    def write(o, inv, cols):
        o = o * inv
        for i in range(ATTN_ROWS):
            blk = o[i * stack:(i + 1) * stack]
            o_ref[i * GRID_W:(i + 1) * GRID_W, cols] = jnp.where(
                first, blk[0:GRID_W], blk[GRID_W:stack]).astype(BF16)

    is_edge = jnp.logical_or(j == 0, j == n_blocks - 1)

    @pl.when(is_edge)
    def _():
        r0 = j * ATTN_ROWS
        rs = jnp.clip(r0 - NA_WIN_H // 2, 0, rows - NA_WIN_H)
        start = pl.multiple_of(rs * GRID_W, GRID_W)
        nkeys = NA_WIN_H * GRID_W
        for pair in range(N_HEADS // 2):
            cols = slice(pair * LANES, (pair + 1) * LANES)
            hrows = slice(pair * stack, (pair + 1) * stack)
            s = scores(stacked_queries(cols), k_ref[pl.ds(start, nkeys), cols])
            ps, invs = [], []
            for i in range(ATTN_ROWS):
                p, inv = softmax_terms(s[i * stack:(i + 1) * stack] + bias_ref[r0 + i - rs, hrows, :])
                ps.append(p)
                invs.append(inv)
            o = _dot(jnp.concatenate(ps, axis=0), v_ref[pl.ds(start, nkeys), cols])
            write(o, jnp.concatenate(invs, axis=0), cols)

    @pl.when(jnp.logical_not(is_edge))
    def _():
        start = pl.multiple_of((j * ATTN_ROWS - NA_WIN_H // 2) * GRID_W, ATTN_ROWS * GRID_W)
        nkeys = ATTN_KEY_ROWS * GRID_W
        for pair in range(N_HEADS // 2):
            cols = slice(pair * LANES, (pair + 1) * LANES)
            hrows = slice(pair * stack, (pair + 1) * stack)
            s = scores(stacked_queries(cols), k_ref[pl.ds(start, nkeys), cols])
            ps, invs = [], []
            for i in range(ATTN_ROWS):
                lo = i // 2 * 2
                hi = lo + NA_WIN_H + 2 * (i % 2)
                bias = bias_odd_ref[hrows, :] if i % 2 else bias_ref[NA_WIN_H // 2, hrows, :]
                p, inv = softmax_terms(s[i * stack:(i + 1) * stack, lo * GRID_W:hi * GRID_W] + bias)
                parts = [p]
                if lo:
                    parts.insert(0, jnp.zeros((stack, lo * GRID_W), BF16))
                if hi < ATTN_KEY_ROWS:
                    parts.append(jnp.zeros((stack, (ATTN_KEY_ROWS - hi) * GRID_W), BF16))
                ps.append(jnp.concatenate(parts, axis=1) if len(parts) > 1 else p)
                invs.append(inv)
            o = _dot(jnp.concatenate(ps, axis=0), v_ref[pl.ds(start, nkeys), cols])
            write(o, jnp.concatenate(invs, axis=0), cols)


def _attention(q, k, v, bias_tab, bias_odd, batch, seq):
    rows = seq // GRID_W
    qt = ATTN_ROWS * GRID_W
    kv_spec = pl.BlockSpec((seq, ATTN_W), lambda b, r: (0, b))
    return pl.pallas_call(
        functools.partial(_attn_kernel, rows=rows),
        grid=(batch, rows // ATTN_ROWS),
        in_specs=[pl.BlockSpec((qt, ATTN_W), lambda b, r: (r, b)), kv_spec, kv_spec,
                  _const_spec(bias_tab.shape), _const_spec(bias_odd.shape)],
        out_specs=pl.BlockSpec((qt, ATTN_W), lambda b, r: (r, b)),
        out_shape=jax.ShapeDtypeStruct((seq, batch * ATTN_W), BF16),
        compiler_params=_params("parallel", "arbitrary"),
        name="natten",
    )(q, k, v, bias_tab, bias_odd)


def _attn_bias_table(rpb):
    c = jnp.arange(GRID_W)[:, None]
    kc = jnp.arange(GRID_W)[None, :]
    cs = jnp.clip(c - NA_WIN_W // 2, 0, GRID_W - NA_WIN_W)
    inside = (kc >= cs) & (kc < cs + NA_WIN_W)
    dc = kc - c + (NA_WIN_W - 1)
    onehot = (dc[:, :, None] == jnp.arange(2 * NA_WIN_W - 1)[None, None, :]).astype(F32)
    band = jnp.einsum('hyx,ckx->hcyk', rpb.astype(F32), onehot, precision=lax.Precision.HIGHEST)
    band = jnp.where(inside[None, :, None, :], band, MASK_BIAS)
    tabs = [band[:, :, NA_WIN_H - 1 - var:2 * NA_WIN_H - 1 - var, :] for var in range(NA_WIN_H)]
    tab = jnp.stack(tabs, axis=0)
    tab = tab.reshape(NA_WIN_H, N_HEADS * GRID_W, NA_WIN_H * GRID_W)
    pad = jnp.full((N_HEADS * GRID_W, GRID_W), MASK_BIAS, F32)
    tab_odd = jnp.concatenate([pad, tab[NA_WIN_H // 2], pad], axis=1)
    return tab, tab_odd


def _ssm_kernel(*refs, batch, reverse):
    n_in = 8 if reverse else 7
    u_cur_ref, u_nxt0_ref, u_nxt1_ref = refs[:3]
    prev_ref = refs[3] if reverse else None
    wb_ref, wc_ref, are_ref, aim_ref, y_ref = refs[n_in - 4:n_in + 1]
    st_refs = refs[n_in + 1:n_in + 1 + SSM_DEPTH]
    carry_ref = refs[n_in + 1 + SSM_DEPTH]
    rows8 = 2 * batch
    chunk = SSM_CHUNK * rows8

    def chunk_rows(pos, n_in_block):
        idx = n_in_block - 1 - pos if reverse else pos
        return slice(idx * chunk, (idx + 1) * chunk)

    def project_in(src_ref, rows, st_ref):
        for q in range(NQ):
            lhs = src_ref[q, rows, :]
            w = wb_ref[q]
            st_ref[:, q * 256:(q + 1) * 256] = _dot(lhs, w[:, 0:256])
            st_ref[:, STATE_COLS + q * 256:STATE_COLS + (q + 1) * 256] = _dot(lhs, w[:, 256:512])

    def recurrence(st_ref, carry):
        a_re = are_ref[...]
        a_im = aim_ref[...]
        x_re, x_im = carry
        for i in range(SSM_CHUNK):
            t = SSM_CHUNK - 1 - i if reverse else i
            rows = slice(t * rows8, (t + 1) * rows8)
            n_re = a_re * x_re - a_im * x_im + st_ref[rows, 0:STATE_COLS]
            n_im = a_re * x_im + a_im * x_re + st_ref[rows, STATE_COLS:2 * STATE_COLS]
            st_ref[rows, 0:STATE_COLS] = n_re
            st_ref[rows, STATE_COLS:2 * STATE_COLS] = n_im
            x_re, x_im = n_re, n_im
        return x_re, x_im

    def project_out(st_ref, rows):
        for q in range(NQ):
            xr = st_ref[:, q * 256:(q + 1) * 256].astype(BF16)
            xi = st_ref[:, STATE_COLS + q * 256:STATE_COLS + (q + 1) * 256].astype(BF16)
            w = wc_ref[q]
            y = _dot(xr, w[0:256, :]) + _dot(xi, w[256:512, :])
            if reverse:
                y = y + prev_ref[q, rows, :]
            y_ref[q, rows, :] = y

    def load_carry():
        return carry_ref[:, 0:STATE_COLS], carry_ref[:, STATE_COLS:2 * STATE_COLS]

    def store_carry(carry):
        carry_ref[:, 0:STATE_COLS] = carry[0]
        carry_ref[:, STATE_COLS:2 * STATE_COLS] = carry[1]

    @pl.when(pl.program_id(0) == 0)
    def _():
        project_in(u_cur_ref, chunk_rows(0, SSM_DEPTH), st_refs[0])
        zero = jnp.zeros((rows8, STATE_COLS), F32)
        store_carry(recurrence(st_refs[0], (zero, zero)))
        project_in(u_cur_ref, chunk_rows(1, SSM_DEPTH), st_refs[1])

    carry = load_carry()
    for i in range(SSM_DEPTH):
        project_out(st_refs[i], chunk_rows(i, SSM_DEPTH))
        carry = recurrence(st_refs[(i + 1) % SSM_DEPTH], carry)
        nxt_ref = u_nxt0_ref if i < SSM_DEPTH // 2 else u_nxt1_ref
        project_in(nxt_ref, chunk_rows(i % (SSM_DEPTH // 2), SSM_DEPTH // 2),
                   st_refs[(i + 2) % SSM_DEPTH])
    store_carry(carry)


def _ssm(u8, prev, wb, wc, a_re, a_im, batch, seq, reverse):
    rows8 = 2 * batch
    chunk = SSM_CHUNK * rows8
    n_steps = seq // (SSM_CHUNK * SSM_DEPTH)
    half = SSM_DEPTH // 2
    n_half = seq // (SSM_CHUNK * half)

    def pos(block, n_blocks):
        block = jnp.minimum(block, n_blocks - 1)
        return n_blocks - 1 - block if reverse else block

    cur_spec = pl.BlockSpec((NQ, chunk * SSM_DEPTH, LANES), lambda k: (0, pos(k, n_steps), 0))
    def ahead_spec(off):
        return pl.BlockSpec((NQ, chunk * half, LANES), lambda k: (0, pos(2 * k + off, n_half), 0))
    operands = [u8, u8, u8] + ([prev] if reverse else []) + [wb, wc, a_re, a_im]
    in_specs = ([cur_spec, ahead_spec(1), ahead_spec(2)] + ([cur_spec] if reverse else [])
                + [_const_spec(w.shape) for w in (wb, wc, a_re, a_im)])
    return pl.pallas_call(
        functools.partial(_ssm_kernel, batch=batch, reverse=reverse),
        grid=(n_steps,),
        in_specs=in_specs,
        out_specs=cur_spec,
        out_shape=jax.ShapeDtypeStruct((NQ, seq * rows8, LANES), F32),
        scratch_shapes=[pltpu.VMEM((chunk, 2 * STATE_COLS), F32) for _ in range(SSM_DEPTH)]
                       + [pltpu.VMEM((rows8, 2 * STATE_COLS), F32)],
        input_output_aliases={3: 0} if reverse else {},
        compiler_params=_params("arbitrary"),
        name="s5_scan_bwd" if reverse else "s5_scan_fwd",
    )(*operands)


def _ssm_weights(lam_re, lam_im, log_dt, b_re, b_im, c_re, c_im, batch):
    lam = lax.complex(jnp.minimum(lam_re, LAMBDA_RE_MAX), lam_im)
    dt = jnp.exp(log_dt)[:, None]
    lam_bar = jnp.exp(lam * dt)
    b_bar = ((lam_bar - 1.0) / lam)[..., None] * lax.complex(b_re, b_im)
    g4, gc, p = SSM_QBLK, SSM_GROUP, SSM_STATE
    eye = jnp.eye(g4, dtype=F32)

    def in_block(x):
        x = x.reshape(NQ, 2, g4, p, gc).transpose(0, 1, 2, 4, 3)
        x = x[:, :, :, :, None, :] * eye[None, None, :, None, :, None]
        return x.reshape(NQ, 2 * g4 * gc, g4 * p)

    wb = jnp.concatenate([in_block(b_bar.real), in_block(b_bar.imag)], axis=-1)

    def out_block(x):
        x = x.reshape(NQ, 2, g4, gc, p).transpose(0, 2, 4, 1, 3)
        x = x[:, :, :, :, None, :] * eye[None, :, None, None, :, None]
        return x.reshape(NQ, g4 * p, 2 * g4 * gc)

    wc = jnp.concatenate([out_block(c_re), -out_block(c_im)], axis=1)

    def decay(x):
        x = x.reshape(NQ, 2, g4 * p).transpose(1, 0, 2).reshape(2, 1, STATE_COLS)
        return jnp.broadcast_to(x, (2, batch, STATE_COLS)).reshape(2 * batch, STATE_COLS)

    return wb.astype(BF16), wc.astype(BF16), decay(lam_bar.real), decay(lam_bar.imag)


def _gelu_tanh(x):
    return 0.5 * x * (1.0 + jnp.tanh(math.sqrt(2.0 / math.pi) * (x + 0.044715 * (x * x * x))))


def _tail_kernel(h_ref, ya_ref, s_ref, y8_ref, ga_ref, gs_ref, p_ref,
                 d_ref, gw_ref, gb_ref, wa_ref, ws_ref, wo_ref,
                 fn_ref, wg_ref, wu_ref, wd_ref,
                 pn_ref, pg_ref, pp_ref, on_ref, o_ref, *, batch):
    t = TIME_TILE
    rows8 = 2 * batch
    y_attn = _dot(_stack_batches(ya_ref, batch, ATTN_W), wa_ref[...])
    lane = lax.broadcasted_iota(jnp.int32, (t, LANES), 1)
    lower = lane < (LANES // 2)
    y_scan = jnp.concatenate(
        [jnp.concatenate(
            [jnp.where(lower, y8_ref[q, pl.ds(b, t, stride=rows8), :],
                       y8_ref[q, pl.ds(batch + b, t, stride=rows8), :]) for q in range(NQ)], axis=1)
         for b in range(batch)], axis=0)
    y = _gelu_tanh(d_ref[...] * _stack_batches(s_ref, batch, SSM_W) + y_scan)
    y = y * _sigmoid(_dot(y.astype(BF16), gw_ref[...]) + gb_ref[...])
    y_ssm = _dot(y.astype(BF16), ws_ref[...])
    mix = (_stack_batches(ga_ref, batch, D_MODEL).astype(F32) * y_attn
           + _stack_batches(gs_ref, batch, D_MODEL).astype(F32) * y_ssm)
    h = _stack_batches(h_ref, batch, D_MODEL) + _dot(mix.astype(BF16), wo_ref[...])
    h = _ffn_half_step(h, fn_ref, wg_ref, wu_ref, wd_ref)
    gate = _sigmoid(_dot(_rms(h, pn_ref[...]).astype(BF16), pg_ref[...]))
    pe = jnp.concatenate([p_ref[b] for b in range(batch)], axis=0).astype(BF16)
    h = h + _dot(pe, pp_ref[...]) * gate
    out = _rms(h, on_ref[...])
    for b in range(batch):
        o_ref[b] = out[b * t:(b + 1) * t]


def _tail(h, y_attn, s_in, y8, g_attn, g_ssm, p, weights, batch, seq):
    t = TIME_TILE
    rows8 = 2 * batch
    def spec(c):
        return pl.BlockSpec((t, batch * c), lambda i: (i, 0))
    return pl.pallas_call(
        functools.partial(_tail_kernel, batch=batch),
        grid=(seq // t,),
        in_specs=[spec(D_MODEL), spec(ATTN_W), spec(SSM_W),
                  pl.BlockSpec((NQ, t * rows8, LANES), lambda i: (0, i, 0)),
                  spec(D_MODEL), spec(D_MODEL),
                  pl.BlockSpec((batch, t, p.shape[-1]), lambda i: (0, i, 0))]
                 + [_const_spec(w.shape) for w in weights],
        out_specs=pl.BlockSpec((batch, t, D_MODEL), lambda i: (0, i, 0)),
        out_shape=jax.ShapeDtypeStruct((batch, seq, D_MODEL), F32),
        compiler_params=_params("parallel"),
        name="tail",
    )(h, y_attn, s_in, y8, g_attn, g_ssm, p, *weights)


def kernel(x, p, ffn1_norm, ffn1_w_gate, ffn1_w_up, ffn1_w_down, mix_norm, w_in, na_rpb, ssm_lam_re, ssm_lam_im, ssm_log_dt, ssm_b_re, ssm_b_im, ssm_c_re, ssm_c_im, ssm_d, ssm_glu_w, ssm_glu_b, w_attn_out, w_ssm_out, w_out, ffn2_norm, ffn2_w_gate, ffn2_w_up, ffn2_w_down, ple_norm, ple_w_gate, ple_w_proj, final_norm):
    batch, seq, _ = x.shape
    assert p.shape[0] == 1, "single-layer block"
    i = 0
    row = lambda v: v.reshape(1, -1)
    bf = lambda w: w.astype(BF16)
    h = _ffn(x, row(ffn1_norm[i]), bf(ffn1_w_gate[i]), bf(ffn1_w_up[i]), bf(ffn1_w_down[i]),
             batch, seq)
    q, k, v, s_in, u8, g_attn, g_ssm = _inproj(h, row(mix_norm[i]), bf(w_in[i]), batch, seq)
    y_attn = _attention(q, k, v, *_attn_bias_table(na_rpb[i]), batch, seq)
    y8 = None
    for d in range(2):
        wb, wc, a_re, a_im = _ssm_weights(
            ssm_lam_re[i, d], ssm_lam_im[i, d], ssm_log_dt[i, d], ssm_b_re[i, d], ssm_b_im[i, d],
            ssm_c_re[i, d], ssm_c_im[i, d], batch)
        y8 = _ssm(u8, y8, wb, wc, a_re, a_im, batch, seq, reverse=d == 1)
    weights = [row(ssm_d[i]), bf(ssm_glu_w[i]), row(ssm_glu_b[i]),
               bf(w_attn_out[i]), bf(w_ssm_out[i]), bf(w_out[i]),
               row(ffn2_norm[i]), bf(ffn2_w_gate[i]), bf(ffn2_w_up[i]), bf(ffn2_w_down[i]),
               row(ple_norm[i]), bf(ple_w_gate[i]), bf(ple_w_proj[i]), row(final_norm)]
    return _tail(h, y_attn, s_in, y8, g_attn, g_ssm, p[i], weights, batch, seq)
```

```python
import functools
import math

import jax
import jax.numpy as jnp
from jax import lax
from jax.experimental import pallas as pl
from jax.experimental.pallas import tpu as pltpu

D_MODEL = 1024
GRID_W = 64
N_HEADS = 8
HEAD_DIM = 64
ATTN_W = N_HEADS * HEAD_DIM
NA_WIN_H = 8
NA_WIN_W = 16
SSM_W = 512
SSM_GROUP = 16
SSM_GROUPS = SSM_W // SSM_GROUP
SSM_STATE = 64
LAMBDA_RE_MAX = -1e-4
D_FF = 2816
RMS_EPS = 1e-6

LANES = 128
VMEM_LIMIT = 56 * 1024 * 1024
MASK_BIAS = -1e30

TOKEN_TILE = 1024
INPROJ_TILE = 256
TIME_TILE = 128
ATTN_ROWS = NA_WIN_H // 2
ATTN_KEY_ROWS = ATTN_ROWS + NA_WIN_H
FF_CHUNKS = ((0, 1536), (1536, 1280))
SSM_CHUNK = 16
SSM_CHUNKS = LANES
NQ = SSM_W // LANES

BF16 = jnp.bfloat16
F32 = jnp.float32


def _dot(a, b):
    return jnp.dot(a, b, preferred_element_type=F32)


def _rms(x, g):
    ms = jnp.mean(x * x, axis=-1, keepdims=True)
    return x * lax.rsqrt(ms + RMS_EPS) * g


def _sigmoid(x):
    return 1.0 / (1.0 + jnp.exp(-x))


def _const_spec(shape):
    nd = len(shape)
    return pl.BlockSpec(shape, lambda *_: (0,) * nd, pipeline_mode=pl.Buffered(1))


def _params(*sem):
    return pltpu.CompilerParams(dimension_semantics=sem, vmem_limit_bytes=VMEM_LIMIT)


def _ffn_half_step(x, g_ref, wg_ref, wu_ref, wd_ref):
    xn = _rms(x, g_ref[...]).astype(BF16)
    acc = None
    for start, size in FF_CHUNKS:
        gate = _dot(xn, wg_ref[:, start:start + size])
        up = _dot(xn, wu_ref[:, start:start + size])
        act = (gate * _sigmoid(gate) * up).astype(BF16)
        part = _dot(act, wd_ref[start:start + size, :])
        acc = part if acc is None else acc + part
    return x + 0.5 * acc


def _ffn_kernel(x_ref, g_ref, wg_ref, wu_ref, wd_ref, o_ref):
    o_ref[...] = _ffn_half_step(x_ref[...], g_ref, wg_ref, wu_ref, wd_ref)


def _ffn(x, norm_g, wg, wu, wd, batch, seq):
    t = TOKEN_TILE
    return pl.pallas_call(
        _ffn_kernel,
        grid=(batch, seq // t),
        in_specs=[pl.BlockSpec((None, t, D_MODEL), lambda b, i: (b, i, 0)),
                  _const_spec((1, D_MODEL)), _const_spec(wg.shape),
                  _const_spec(wu.shape), _const_spec(wd.shape)],
        out_specs=pl.BlockSpec((t, D_MODEL), lambda b, i: (i, b)),
        out_shape=jax.ShapeDtypeStruct((seq, batch * D_MODEL), F32),
        compiler_params=_params("parallel", "parallel"),
        name="ffn",
    )(x, norm_g, wg, wu, wd)


def _stack_batches(ref, batch, width):
    return jnp.concatenate([ref[:, b * width:(b + 1) * width] for b in range(batch)], axis=0)


def _unstack_batches(ref, x, batch, width):
    t = x.shape[0] // batch
    for b in range(batch):
        ref[:, b * width:(b + 1) * width] = x[b * t:(b + 1) * t]


def _inproj_kernel(h_ref, g_ref, w_ref, q_ref, k_ref, v_ref, s_ref, ga_ref, gs_ref, *, batch):
    t = INPROJ_TILE
    u = _rms(_stack_batches(h_ref, batch, D_MODEL), g_ref[...]).astype(BF16)
    a = ATTN_W
    _unstack_batches(q_ref, (_dot(u, w_ref[:, 0:a]) * (HEAD_DIM ** -0.5)).astype(BF16), batch, a)
    _unstack_batches(k_ref, _dot(u, w_ref[:, a:2 * a]).astype(BF16), batch, a)
    _unstack_batches(v_ref, _dot(u, w_ref[:, 2 * a:3 * a]).astype(BF16), batch, a)
    s0 = 3 * a
    s_in = _dot(u, w_ref[:, s0:s0 + SSM_W])
    for c in range(NQ):
        for b in range(batch):
            s_ref[c, :, b * LANES:(b + 1) * LANES] = s_in[b * t:(b + 1) * t, c * LANES:(c + 1) * LANES]
    g0 = s0 + SSM_W
    _unstack_batches(ga_ref, _sigmoid(_dot(u, w_ref[:, g0:g0 + D_MODEL])).astype(BF16),
                     batch, D_MODEL)
    _unstack_batches(gs_ref, _sigmoid(_dot(u, w_ref[:, g0 + D_MODEL:g0 + 2 * D_MODEL])).astype(BF16),
                     batch, D_MODEL)


def _inproj(h, norm_g, w_in, batch, seq):
    t = INPROJ_TILE
    def spec(c):
        return pl.BlockSpec((t, batch * c), lambda i: (i, 0))
    def shape(c, dt):
        return jax.ShapeDtypeStruct((seq, batch * c), dt)
    return pl.pallas_call(
        functools.partial(_inproj_kernel, batch=batch),
        grid=(seq // t,),
        in_specs=[spec(D_MODEL), _const_spec((1, D_MODEL)), _const_spec(w_in.shape)],
        out_specs=[spec(ATTN_W), spec(ATTN_W), spec(ATTN_W),
                   pl.BlockSpec((NQ, t, batch * LANES), lambda i: (0, i, 0)),
                   spec(D_MODEL), spec(D_MODEL)],
        out_shape=[shape(ATTN_W, BF16), shape(ATTN_W, BF16), shape(ATTN_W, BF16),
                   jax.ShapeDtypeStruct((NQ, seq, batch * LANES), F32),
                   shape(D_MODEL, BF16), shape(D_MODEL, BF16)],
        compiler_params=_params("parallel"),
        name="inproj",
    )(h, norm_g, w_in)


def _attn_kernel(q_ref, k_ref, v_ref, bias_ref, bias_odd_ref, o_ref, *, rows):
    j = pl.program_id(1)
    n_blocks = rows // ATTN_ROWS
    stack = 2 * GRID_W
    lane = lax.broadcasted_iota(jnp.int32, (GRID_W, LANES), 1)
    first = lane < HEAD_DIM

    def stacked_queries(cols):
        pieces = []
        for i in range(ATTN_ROWS):
            q2 = q_ref[i * GRID_W:(i + 1) * GRID_W, cols]
            zero = jnp.zeros_like(q2)
            pieces += [jnp.where(first, q2, zero), jnp.where(first, zero, q2)]
        return jnp.concatenate(pieces, axis=0)

    def scores(lhs, keys):
        return lax.dot_general(lhs, keys, (((1,), (1,)), ((), ())), preferred_element_type=F32)

    def softmax_terms(s):
        p = jnp.exp(s - jnp.max(s, axis=-1, keepdims=True))
        return p.astype(BF16), 1.0 / jnp.sum(p, axis=-1, keepdims=True)

    def write(o, inv, cols):
        o = o * inv
        for i in range(ATTN_ROWS):
            blk = o[i * stack:(i + 1) * stack]
            o_ref[i * GRID_W:(i + 1) * GRID_W, cols] = jnp.where(
                first, blk[0:GRID_W], blk[GRID_W:stack]).astype(BF16)

    is_edge = jnp.logical_or(j == 0, j == n_blocks - 1)

    @pl.when(is_edge)
    def _():
        r0 = j * ATTN_ROWS
        rs = jnp.clip(r0 - NA_WIN_H // 2, 0, rows - NA_WIN_H)
        start = pl.multiple_of(rs * GRID_W, GRID_W)
        nkeys = NA_WIN_H * GRID_W
        for pair in range(N_HEADS // 2):
            cols = slice(pair * LANES, (pair + 1) * LANES)
            hrows = slice(pair * stack, (pair + 1) * stack)
            s = scores(stacked_queries(cols), k_ref[pl.ds(start, nkeys), cols])
            ps, invs = [], []
            for i in range(ATTN_ROWS):
                p, inv = softmax_terms(s[i * stack:(i + 1) * stack] + bias_ref[r0 + i - rs, hrows, :])
                ps.append(p)
                invs.append(inv)
            o = _dot(jnp.concatenate(ps, axis=0), v_ref[pl.ds(start, nkeys), cols])
            write(o, jnp.concatenate(invs, axis=0), cols)

    @pl.when(jnp.logical_not(is_edge))
    def _():
        start = pl.multiple_of((j * ATTN_ROWS - NA_WIN_H // 2) * GRID_W, ATTN_ROWS * GRID_W)
        nkeys = ATTN_KEY_ROWS * GRID_W
        for pair in range(N_HEADS // 2):
            cols = slice(pair * LANES, (pair + 1) * LANES)
            hrows = slice(pair * stack, (pair + 1) * stack)
            s = scores(stacked_queries(cols), k_ref[pl.ds(start, nkeys), cols])
            ps, invs = [], []
            for i in range(ATTN_ROWS):
                lo = i // 2 * 2
                hi = lo + NA_WIN_H + 2 * (i % 2)
                bias = bias_odd_ref[hrows, :] if i % 2 else bias_ref[NA_WIN_H // 2, hrows, :]
                p, inv = softmax_terms(s[i * stack:(i + 1) * stack, lo * GRID_W:hi * GRID_W] + bias)
                parts = [p]
                if lo:
                    parts.insert(0, jnp.zeros((stack, lo * GRID_W), BF16))
                if hi < ATTN_KEY_ROWS:
                    parts.append(jnp.zeros((stack, (ATTN_KEY_ROWS - hi) * GRID_W), BF16))
                ps.append(jnp.concatenate(parts, axis=1) if len(parts) > 1 else p)
                invs.append(inv)
            o = _dot(jnp.concatenate(ps, axis=0), v_ref[pl.ds(start, nkeys), cols])
            write(o, jnp.concatenate(invs, axis=0), cols)


def _attention(q, k, v, bias_tab, bias_odd, batch, seq):
    rows = seq // GRID_W
    qt = ATTN_ROWS * GRID_W
    kv_spec = pl.BlockSpec((seq, ATTN_W), lambda b, r: (0, b))
    return pl.pallas_call(
        functools.partial(_attn_kernel, rows=rows),
        grid=(batch, rows // ATTN_ROWS),
        in_specs=[pl.BlockSpec((qt, ATTN_W), lambda b, r: (r, b)), kv_spec, kv_spec,
                  _const_spec(bias_tab.shape), _const_spec(bias_odd.shape)],
        out_specs=pl.BlockSpec((qt, ATTN_W), lambda b, r: (r, b)),
        out_shape=jax.ShapeDtypeStruct((seq, batch * ATTN_W), BF16),
        compiler_params=_params("parallel", "arbitrary"),
        name="natten",
    )(q, k, v, bias_tab, bias_odd)


def _attn_bias_table(rpb):
    c = jnp.arange(GRID_W)[:, None]
    kc = jnp.arange(GRID_W)[None, :]
    cs = jnp.clip(c - NA_WIN_W // 2, 0, GRID_W - NA_WIN_W)
    inside = (kc >= cs) & (kc < cs + NA_WIN_W)
    dc = kc - c + (NA_WIN_W - 1)
    onehot = (dc[:, :, None] == jnp.arange(2 * NA_WIN_W - 1)[None, None, :]).astype(F32)
    band = jnp.einsum('hyx,ckx->hcyk', rpb.astype(F32), onehot, precision=lax.Precision.HIGHEST)
    band = jnp.where(inside[None, :, None, :], band, MASK_BIAS)
    tabs = [band[:, :, NA_WIN_H - 1 - var:2 * NA_WIN_H - 1 - var, :] for var in range(NA_WIN_H)]
    tab = jnp.stack(tabs, axis=0)
    tab = tab.reshape(NA_WIN_H, N_HEADS * GRID_W, NA_WIN_H * GRID_W)
    pad = jnp.full((N_HEADS * GRID_W, GRID_W), MASK_BIAS, F32)
    tab_odd = jnp.concatenate([pad, tab[NA_WIN_H // 2], pad], axis=1)
    return tab, tab_odd


def _ssm_kernel(*refs, reverse):
    if reverse:
        s_ref, prev_ref = refs[:2]
        refs = refs[2:]
    else:
        s_ref, prev_ref = refs[0], None
        refs = refs[1:]
    (km_ref, ss_ref, wx_ref, mur_ref, mui_ref, y_ref,
     zt_ref, swr_ref, swi_ref, xwr_ref, xwi_ref, ytq_ref, er_ref, ei_ref) = refs
    step_rows, n_chunks, gc, p = SSM_CHUNK, SSM_CHUNKS, SSM_GROUP, SSM_STATE
    n_pairs = SSM_GROUPS // 2
    slab_groups = LANES // gc

    @pl.when(pl.program_id(1) == 0)
    def _():
        er_ref[...] = jnp.zeros_like(er_ref)
        ei_ref[...] = jnp.zeros_like(ei_ref)

    for q in range(NQ):
        for tau in range(step_rows):
            at = s_ref[q, pl.ds(tau, n_chunks, stride=step_rows), :].T.astype(BF16)
            for g8 in range(slab_groups):
                zt_ref[slab_groups * q + g8, tau * gc:(tau + 1) * gc, :] = at[g8 * gc:(g8 + 1) * gc, :]

    for gp in range(n_pairs):
        r0 = _dot(ss_ref[2 * gp], zt_ref[2 * gp])
        r1 = _dot(ss_ref[2 * gp + 1], zt_ref[2 * gp + 1])
        swr_ref[pl.ds(gp, n_chunks, stride=n_pairs), :] = jnp.concatenate(
            [r0[0:p], r1[0:p]], axis=0).T
        swi_ref[pl.ds(gp, n_chunks, stride=n_pairs), :] = jnp.concatenate(
            [r0[p:2 * p], r1[p:2 * p]], axis=0).T

    mu_r = mur_ref[...]
    mu_i = mui_ref[...]

    def step(j, carry):
        e_r, e_i = carry
        chunk = n_chunks - 1 - j if reverse else j
        rows = pl.ds(pl.multiple_of(chunk * n_pairs, n_pairs), n_pairs)
        xwr_ref[rows, :] = e_r
        xwi_ref[rows, :] = e_i
        return (mu_r * e_r - mu_i * e_i + swr_ref[rows, :],
                mu_r * e_i + mu_i * e_r + swi_ref[rows, :])

    e_r, e_i = lax.fori_loop(0, n_chunks, step, (er_ref[...], ei_ref[...]))
    er_ref[...] = e_r
    ei_ref[...] = e_i

    for q in range(NQ):
        for g8 in range(0, slab_groups, 2):
            gp = (slab_groups * q + g8) // 2
            xr_t = xwr_ref[pl.ds(gp, n_chunks, stride=n_pairs), :].T
            xi_t = xwi_ref[pl.ds(gp, n_chunks, stride=n_pairs), :].T
            for h in range(2):
                g = 2 * gp + h
                x_t = jnp.concatenate([xr_t[h * p:(h + 1) * p], xi_t[h * p:(h + 1) * p]],
                                      axis=0).astype(BF16)
                ytq_ref[g8 + h] = _dot(km_ref[g], zt_ref[g]) + _dot(wx_ref[g], x_t)
        for tau in range(step_rows):
            piece = jnp.concatenate(
                [ytq_ref[g8, tau * gc:(tau + 1) * gc, :] for g8 in range(slab_groups)], axis=0)
            out = piece.T
            if reverse:
                out = out + prev_ref[q, pl.ds(tau, n_chunks, stride=step_rows), :]
            y_ref[q, pl.ds(tau, n_chunks, stride=step_rows), :] = out


def _ssm(s_in, prev, weights, batch, seq, reverse):
    tile = SSM_CHUNK * SSM_CHUNKS
    n_tiles = seq // tile
    n_pairs = SSM_GROUPS // 2
    pos = (lambda k: n_tiles - 1 - k) if reverse else (lambda k: k)
    io_spec = pl.BlockSpec((NQ, tile, LANES), lambda b, k: (0, pos(k), b))
    operands = [s_in] + ([prev] if reverse else []) + list(weights)
    in_specs = [io_spec] + ([io_spec] if reverse else []) + [_const_spec(w.shape) for w in weights]
    wide = SSM_CHUNK * SSM_GROUP
    return pl.pallas_call(
        functools.partial(_ssm_kernel, reverse=reverse),
        grid=(batch, n_tiles),
        in_specs=in_specs,
        out_specs=io_spec,
        out_shape=jax.ShapeDtypeStruct((NQ, seq, batch * LANES), F32),
        scratch_shapes=[pltpu.VMEM((SSM_GROUPS, wide, SSM_CHUNKS), BF16)]
                       + [pltpu.VMEM((SSM_CHUNKS * n_pairs, LANES), F32) for _ in range(4)]
                       + [pltpu.VMEM((LANES // SSM_GROUP, wide, SSM_CHUNKS), F32),
                          pltpu.VMEM((n_pairs, LANES), F32), pltpu.VMEM((n_pairs, LANES), F32)],
        input_output_aliases={1: 0} if reverse else {},
        compiler_params=_params("parallel", "arbitrary"),
        name="s5_scan_bwd" if reverse else "s5_scan_fwd",
    )(*operands)


def _ssm_weights(lam_re, lam_im, log_dt, b_re, b_im, c_re, c_im, reverse):
    n, g, p, gc = SSM_CHUNK, SSM_GROUPS, SSM_STATE, SSM_GROUP
    lam = lax.complex(jnp.minimum(lam_re, LAMBDA_RE_MAX), lam_im)
    z = lam * jnp.exp(log_dt)[:, None]
    lam_bar = jnp.exp(z)
    b_bar = ((lam_bar - 1.0) / lam)[..., None] * lax.complex(b_re, b_im)
    c = lax.complex(c_re, c_im)
    pw = jnp.exp(z[None] * jnp.arange(n + 1, dtype=F32)[:, None, None])
    kk = jnp.sum(c[None, :, :, :, None] * pw[:n, :, None, :, None] * b_bar[None, :, None, :, :],
                 axis=3).real
    t_out = jnp.arange(n)[:, None]
    t_in = jnp.arange(n)[None, :]
    lag = (t_in - t_out) if reverse else (t_out - t_in)
    onehot = (lag[:, :, None] == jnp.arange(n)[None, None, :]).astype(F32)
    kmat = jnp.einsum('abk,kgoi->gaobi', onehot, kk,
                      precision=lax.Precision.HIGHEST).reshape(g, n * gc, n * gc)
    pw_in = pw[:n] if reverse else jnp.flip(pw[:n], axis=0)
    sb = pw_in[..., None] * b_bar[None]
    ssum = jnp.concatenate([sb.real, sb.imag], axis=2).transpose(1, 2, 0, 3).reshape(g, 2 * p, n * gc)
    pw_out = jnp.flip(pw[1:], axis=0) if reverse else pw[1:]
    cw = c[None] * pw_out[:, :, None, :]
    wx = jnp.concatenate([cw.real, -cw.imag], axis=-1).transpose(1, 0, 2, 3).reshape(g, n * gc, 2 * p)
    mu = pw[n].reshape(g // 2, 2 * p)
    return kmat.astype(BF16), ssum.astype(BF16), wx.astype(BF16), mu.real, mu.imag


def _gelu_tanh(x):
    return 0.5 * x * (1.0 + jnp.tanh(math.sqrt(2.0 / math.pi) * (x + 0.044715 * (x * x * x))))


def _tail_kernel(h_ref, ya_ref, s_ref, ys_ref, ga_ref, gs_ref, p_ref,
                 d_ref, gw_ref, gb_ref, wa_ref, ws_ref, wo_ref,
                 fn_ref, wg_ref, wu_ref, wd_ref,
                 pn_ref, pg_ref, pp_ref, on_ref, o_ref, *, batch):
    t = TIME_TILE
    y_attn = _dot(_stack_batches(ya_ref, batch, ATTN_W), wa_ref[...])

    def slabs(ref):
        return jnp.concatenate(
            [jnp.concatenate([ref[c, :, b * LANES:(b + 1) * LANES] for c in range(NQ)], axis=1)
             for b in range(batch)], axis=0)

    y = _gelu_tanh(d_ref[...] * slabs(s_ref) + slabs(ys_ref))
    y = y * _sigmoid(_dot(y.astype(BF16), gw_ref[...]) + gb_ref[...])
    y_ssm = _dot(y.astype(BF16), ws_ref[...])
    mix = (_stack_batches(ga_ref, batch, D_MODEL).astype(F32) * y_attn
           + _stack_batches(gs_ref, batch, D_MODEL).astype(F32) * y_ssm)
    h = _stack_batches(h_ref, batch, D_MODEL) + _dot(mix.astype(BF16), wo_ref[...])
    h = _ffn_half_step(h, fn_ref, wg_ref, wu_ref, wd_ref)
    gate = _sigmoid(_dot(_rms(h, pn_ref[...]).astype(BF16), pg_ref[...]))
    pe = jnp.concatenate([p_ref[b] for b in range(batch)], axis=0).astype(BF16)
    h = h + _dot(pe, pp_ref[...]) * gate
    out = _rms(h, on_ref[...])
    for b in range(batch):
        o_ref[b] = out[b * t:(b + 1) * t]


def _tail(h, y_attn, s_in, y_scan, g_attn, g_ssm, p, weights, batch, seq):
    t = TIME_TILE
    def spec(c):
        return pl.BlockSpec((t, batch * c), lambda i: (i, 0))
    slab_spec = pl.BlockSpec((NQ, t, batch * LANES), lambda i: (0, i, 0))
    return pl.pallas_call(
        functools.partial(_tail_kernel, batch=batch),
        grid=(seq // t,),
        in_specs=[spec(D_MODEL), spec(ATTN_W), slab_spec, slab_spec,
                  spec(D_MODEL), spec(D_MODEL),
                  pl.BlockSpec((batch, t, p.shape[-1]), lambda i: (0, i, 0))]
                 + [_const_spec(w.shape) for w in weights],
        out_specs=pl.BlockSpec((batch, t, D_MODEL), lambda i: (0, i, 0)),
        out_shape=jax.ShapeDtypeStruct((batch, seq, D_MODEL), F32),
        compiler_params=_params("parallel"),
        name="tail",
    )(h, y_attn, s_in, y_scan, g_attn, g_ssm, p, *weights)


def kernel(x, p, ffn1_norm, ffn1_w_gate, ffn1_w_up, ffn1_w_down, mix_norm, w_in, na_rpb, ssm_lam_re, ssm_lam_im, ssm_log_dt, ssm_b_re, ssm_b_im, ssm_c_re, ssm_c_im, ssm_d, ssm_glu_w, ssm_glu_b, w_attn_out, w_ssm_out, w_out, ffn2_norm, ffn2_w_gate, ffn2_w_up, ffn2_w_down, ple_norm, ple_w_gate, ple_w_proj, final_norm):
    batch, seq, _ = x.shape
    assert p.shape[0] == 1, "single-layer block"
    i = 0
    row = lambda v: v.reshape(1, -1)
    bf = lambda w: w.astype(BF16)
    h = _ffn(x, row(ffn1_norm[i]), bf(ffn1_w_gate[i]), bf(ffn1_w_up[i]), bf(ffn1_w_down[i]),
             batch, seq)
    q, k, v, s_in, g_attn, g_ssm = _inproj(h, row(mix_norm[i]), bf(w_in[i]), batch, seq)
    y_attn = _attention(q, k, v, *_attn_bias_table(na_rpb[i]), batch, seq)
    y_scan = None
    for d in range(2):
        weights = _ssm_weights(
            ssm_lam_re[i, d], ssm_lam_im[i, d], ssm_log_dt[i, d], ssm_b_re[i, d], ssm_b_im[i, d],
            ssm_c_re[i, d], ssm_c_im[i, d], reverse=d == 1)
        y_scan = _ssm(s_in, y_scan, weights, batch, seq, reverse=d == 1)
    weights = [row(ssm_d[i]), bf(ssm_glu_w[i]), row(ssm_glu_b[i]),
               bf(w_attn_out[i]), bf(w_ssm_out[i]), bf(w_out[i]),
               row(ffn2_norm[i]), bf(ffn2_w_gate[i]), bf(ffn2_w_up[i]), bf(ffn2_w_down[i]),
               row(ple_norm[i]), bf(ple_w_gate[i]), bf(ple_w_proj[i]), row(final_norm)]
    return _tail(h, y_attn, s_in, y_scan, g_attn, g_ssm, p[i], weights, batch, seq)
```

```python
import functools
import math

import jax
import jax.numpy as jnp
from jax import lax
from jax.experimental import pallas as pl
from jax.experimental.pallas import tpu as pltpu

D_MODEL = 1024
GRID_W = 64
N_HEADS = 8
HEAD_DIM = 64
ATTN_W = N_HEADS * HEAD_DIM
NA_WIN_H = 8
NA_WIN_W = 16
SSM_W = 512
SSM_GROUP = 16
SSM_GROUPS = SSM_W // SSM_GROUP
SSM_STATE = 64
LAMBDA_RE_MAX = -1e-4
D_FF = 2816
RMS_EPS = 1e-6

LANES = 128
VMEM_LIMIT = 56 * 1024 * 1024
MASK_BIAS = -1e30

TOKEN_TILE = 1024
INPROJ_TILE = 256
TIME_TILE = 128
ATTN_ROWS = NA_WIN_H // 2
ATTN_KEY_ROWS = ATTN_ROWS + NA_WIN_H
FF_CHUNKS = ((0, 1536), (1536, 1280))
SSM_CHUNK = 16
SSM_CHUNKS = LANES
NQ = SSM_W // LANES

BF16 = jnp.bfloat16
F32 = jnp.float32


def _dot(a, b):
    return jnp.dot(a, b, preferred_element_type=F32)


def _rms(x, g):
    ms = jnp.mean(x * x, axis=-1, keepdims=True)
    return x * lax.rsqrt(ms + RMS_EPS) * g


def _sigmoid(x):
    return 1.0 / (1.0 + jnp.exp(-x))


def _const_spec(shape):
    nd = len(shape)
    return pl.BlockSpec(shape, lambda *_: (0,) * nd, pipeline_mode=pl.Buffered(1))


def _params(*sem):
    return pltpu.CompilerParams(dimension_semantics=sem, vmem_limit_bytes=VMEM_LIMIT)


def _ffn_half_step(x, g_ref, wg_ref, wu_ref, wd_ref):
    xn = _rms(x, g_ref[...]).astype(BF16)
    acc = None
    for start, size in FF_CHUNKS:
        gate = _dot(xn, wg_ref[:, start:start + size])
        up = _dot(xn, wu_ref[:, start:start + size])
        act = (gate * _sigmoid(gate) * up).astype(BF16)
        part = _dot(act, wd_ref[start:start + size, :])
        acc = part if acc is None else acc + part
    return x + 0.5 * acc


def _ffn_kernel(x_ref, g_ref, wg_ref, wu_ref, wd_ref, o_ref):
    o_ref[...] = _ffn_half_step(x_ref[...], g_ref, wg_ref, wu_ref, wd_ref)


def _ffn(x, norm_g, wg, wu, wd, batch, seq):
    t = TOKEN_TILE
    return pl.pallas_call(
        _ffn_kernel,
        grid=(batch, seq // t),
        in_specs=[pl.BlockSpec((None, t, D_MODEL), lambda b, i: (b, i, 0)),
                  _const_spec((1, D_MODEL)), _const_spec(wg.shape),
                  _const_spec(wu.shape), _const_spec(wd.shape)],
        out_specs=pl.BlockSpec((t, D_MODEL), lambda b, i: (i, b)),
        out_shape=jax.ShapeDtypeStruct((seq, batch * D_MODEL), F32),
        compiler_params=_params("parallel", "parallel"),
        name="ffn",
    )(x, norm_g, wg, wu, wd)


def _stack_batches(ref, batch, width):
    return jnp.concatenate([ref[:, b * width:(b + 1) * width] for b in range(batch)], axis=0)


def _unstack_batches(ref, x, batch, width):
    t = x.shape[0] // batch
    for b in range(batch):
        ref[:, b * width:(b + 1) * width] = x[b * t:(b + 1) * t]


def _inproj_kernel(h_ref, g_ref, w_ref, q_ref, k_ref, v_ref, s_ref, ga_ref, gs_ref, *, batch):
    t = INPROJ_TILE
    u = _rms(_stack_batches(h_ref, batch, D_MODEL), g_ref[...]).astype(BF16)
    a = ATTN_W
    _unstack_batches(q_ref, (_dot(u, w_ref[:, 0:a]) * (HEAD_DIM ** -0.5)).astype(BF16), batch, a)
    _unstack_batches(k_ref, _dot(u, w_ref[:, a:2 * a]).astype(BF16), batch, a)
    _unstack_batches(v_ref, _dot(u, w_ref[:, 2 * a:3 * a]).astype(BF16), batch, a)
    s0 = 3 * a
    s_in = _dot(u, w_ref[:, s0:s0 + SSM_W])
    for c in range(NQ):
        for b in range(batch):
            s_ref[c, :, b * LANES:(b + 1) * LANES] = s_in[b * t:(b + 1) * t, c * LANES:(c + 1) * LANES]
    g0 = s0 + SSM_W
    _unstack_batches(ga_ref, _sigmoid(_dot(u, w_ref[:, g0:g0 + D_MODEL])).astype(BF16),
                     batch, D_MODEL)
    _unstack_batches(gs_ref, _sigmoid(_dot(u, w_ref[:, g0 + D_MODEL:g0 + 2 * D_MODEL])).astype(BF16),
                     batch, D_MODEL)


def _inproj(h, norm_g, w_in, batch, seq):
    t = INPROJ_TILE
    def spec(c):
        return pl.BlockSpec((t, batch * c), lambda i: (i, 0))
    def shape(c, dt):
        return jax.ShapeDtypeStruct((seq, batch * c), dt)
    return pl.pallas_call(
        functools.partial(_inproj_kernel, batch=batch),
        grid=(seq // t,),
        in_specs=[spec(D_MODEL), _const_spec((1, D_MODEL)), _const_spec(w_in.shape)],
        out_specs=[spec(ATTN_W), spec(ATTN_W), spec(ATTN_W),
                   pl.BlockSpec((NQ, t, batch * LANES), lambda i: (0, i, 0)),
                   spec(D_MODEL), spec(D_MODEL)],
        out_shape=[shape(ATTN_W, BF16), shape(ATTN_W, BF16), shape(ATTN_W, BF16),
                   jax.ShapeDtypeStruct((NQ, seq, batch * LANES), F32),
                   shape(D_MODEL, BF16), shape(D_MODEL, BF16)],
        compiler_params=_params("parallel"),
        name="inproj",
    )(h, norm_g, w_in)


def _attn_kernel(q_ref, k_ref, v_ref, bias_ref, bias_odd_ref, o_ref, *, rows):
    j = pl.program_id(1)
    n_blocks = rows // ATTN_ROWS
    stack = 2 * GRID_W
    lane = lax.broadcasted_iota(jnp.int32, (GRID_W, LANES), 1)
    first = lane < HEAD_DIM

    def stacked_queries(cols):
        pieces = []
        for i in range(ATTN_ROWS):
            q2 = q_ref[i * GRID_W:(i + 1) * GRID_W, cols]
            zero = jnp.zeros_like(q2)
            pieces += [jnp.where(first, q2, zero), jnp.where(first, zero, q2)]
        return jnp.concatenate(pieces, axis=0)

    def scores(lhs, keys):
        return lax.dot_general(lhs, keys, (((1,), (1,)), ((), ())), preferred_element_type=F32)

    def softmax_terms(s):
        p = jnp.exp(s - jnp.max(s, axis=-1, keepdims=True))
        return p.astype(BF16), 1.0 / jnp.sum(p, axis=-1, keepdims=True)

    def write(o, inv, cols):
        o = o * inv
        for i in range(ATTN_ROWS):
            blk = o[i * stack:(i + 1) * stack]
            o_ref[i * GRID_W:(i + 1) * GRID_W, cols] = jnp.where(
                first, blk[0:GRID_W], blk[GRID_W:stack]).astype(BF16)

    is_edge = jnp.logical_or(j == 0, j == n_blocks - 1)

    @pl.when(is_edge)
    def _():
        r0 = j * ATTN_ROWS
        rs = jnp.clip(r0 - NA_WIN_H // 2, 0, rows - NA_WIN_H)
        start = pl.multiple_of(rs * GRID_W, GRID_W)
        nkeys = NA_WIN_H * GRID_W
        for pair in range(N_HEADS // 2):
            cols = slice(pair * LANES, (pair + 1) * LANES)
            hrows = slice(pair * stack, (pair + 1) * stack)
            s = scores(stacked_queries(cols), k_ref[pl.ds(start, nkeys), cols])
            ps, invs = [], []
            for i in range(ATTN_ROWS):
                p, inv = softmax_terms(s[i * stack:(i + 1) * stack] + bias_ref[r0 + i - rs, hrows, :])
                ps.append(p)
                invs.append(inv)
            o = _dot(jnp.concatenate(ps, axis=0), v_ref[pl.ds(start, nkeys), cols])
            write(o, jnp.concatenate(invs, axis=0), cols)

    @pl.when(jnp.logical_not(is_edge))
    def _():
        start = pl.multiple_of((j * ATTN_ROWS - NA_WIN_H // 2) * GRID_W, ATTN_ROWS * GRID_W)
        nkeys = ATTN_KEY_ROWS * GRID_W
        for pair in range(N_HEADS // 2):
            cols = slice(pair * LANES, (pair + 1) * LANES)
            hrows = slice(pair * stack, (pair + 1) * stack)
            s = scores(stacked_queries(cols), k_ref[pl.ds(start, nkeys), cols])
            ps, invs = [], []
            for i in range(ATTN_ROWS):
                lo = i // 2 * 2
                hi = lo + NA_WIN_H + 2 * (i % 2)
                bias = bias_odd_ref[hrows, :] if i % 2 else bias_ref[NA_WIN_H // 2, hrows, :]
                p, inv = softmax_terms(s[i * stack:(i + 1) * stack, lo * GRID_W:hi * GRID_W] + bias)
                parts = [p]
                if lo:
                    parts.insert(0, jnp.zeros((stack, lo * GRID_W), BF16))
                if hi < ATTN_KEY_ROWS:
                    parts.append(jnp.zeros((stack, (ATTN_KEY_ROWS - hi) * GRID_W), BF16))
                ps.append(jnp.concatenate(parts, axis=1) if len(parts) > 1 else p)
                invs.append(inv)
            o = _dot(jnp.concatenate(ps, axis=0), v_ref[pl.ds(start, nkeys), cols])
            write(o, jnp.concatenate(invs, axis=0), cols)


def _attention(q, k, v, bias_tab, bias_odd, batch, seq):
    rows = seq // GRID_W
    qt = ATTN_ROWS * GRID_W
    kv_spec = pl.BlockSpec((seq, ATTN_W), lambda b, r: (0, b))
    return pl.pallas_call(
        functools.partial(_attn_kernel, rows=rows),
        grid=(batch, rows // ATTN_ROWS),
        in_specs=[pl.BlockSpec((qt, ATTN_W), lambda b, r: (r, b)), kv_spec, kv_spec,
                  _const_spec(bias_tab.shape), _const_spec(bias_odd.shape)],
        out_specs=pl.BlockSpec((qt, ATTN_W), lambda b, r: (r, b)),
        out_shape=jax.ShapeDtypeStruct((seq, batch * ATTN_W), BF16),
        compiler_params=_params("parallel", "arbitrary"),
        name="natten",
    )(q, k, v, bias_tab, bias_odd)


def _attn_bias_table(rpb):
    c = jnp.arange(GRID_W)[:, None]
    kc = jnp.arange(GRID_W)[None, :]
    cs = jnp.clip(c - NA_WIN_W // 2, 0, GRID_W - NA_WIN_W)
    inside = (kc >= cs) & (kc < cs + NA_WIN_W)
    dc = kc - c + (NA_WIN_W - 1)
    onehot = (dc[:, :, None] == jnp.arange(2 * NA_WIN_W - 1)[None, None, :]).astype(F32)
    band = jnp.einsum('hyx,ckx->hcyk', rpb.astype(F32), onehot, precision=lax.Precision.HIGHEST)
    band = jnp.where(inside[None, :, None, :], band, MASK_BIAS)
    tabs = [band[:, :, NA_WIN_H - 1 - var:2 * NA_WIN_H - 1 - var, :] for var in range(NA_WIN_H)]
    tab = jnp.stack(tabs, axis=0)
    tab = tab.reshape(NA_WIN_H, N_HEADS * GRID_W, NA_WIN_H * GRID_W)
    pad = jnp.full((N_HEADS * GRID_W, GRID_W), MASK_BIAS, F32)
    tab_odd = jnp.concatenate([pad, tab[NA_WIN_H // 2], pad], axis=1)
    return tab, tab_odd


def _ssm_kernel(*refs, reverse):
    if reverse:
        s_ref, prev_ref = refs[:2]
        refs = refs[2:]
    else:
        s_ref, prev_ref = refs[0], None
        refs = refs[1:]
    (km_ref, ss_ref, wx_ref, mur_ref, mui_ref, y_ref,
     zt_ref, swr_ref, swi_ref, xwr_ref, xwi_ref, ytq_ref, er_ref, ei_ref) = refs
    step_rows, n_chunks, gc, p = SSM_CHUNK, SSM_CHUNKS, SSM_GROUP, SSM_STATE
    n_pairs = SSM_GROUPS // 2
    slab_groups = LANES // gc

    @pl.when(pl.program_id(1) == 0)
    def _():
        er_ref[...] = jnp.zeros_like(er_ref)
        ei_ref[...] = jnp.zeros_like(ei_ref)

    for q in range(NQ):
        for tau in range(step_rows):
            at = s_ref[q, pl.ds(tau, n_chunks, stride=step_rows), :].T.astype(BF16)
            for g8 in range(slab_groups):
                zt_ref[slab_groups * q + g8, tau * gc:(tau + 1) * gc, :] = at[g8 * gc:(g8 + 1) * gc, :]

    for gp in range(n_pairs):
        r0 = _dot(ss_ref[2 * gp], zt_ref[2 * gp])
        r1 = _dot(ss_ref[2 * gp + 1], zt_ref[2 * gp + 1])
        swr_ref[pl.ds(gp, n_chunks, stride=n_pairs), :] = jnp.concatenate(
            [r0[0:p], r1[0:p]], axis=0).T
        swi_ref[pl.ds(gp, n_chunks, stride=n_pairs), :] = jnp.concatenate(
            [r0[p:2 * p], r1[p:2 * p]], axis=0).T

    mu_r = mur_ref[...]
    mu_i = mui_ref[...]

    def step(j, carry):
        e_r, e_i = carry
        chunk = n_chunks - 1 - j if reverse else j
        rows = pl.ds(pl.multiple_of(chunk * n_pairs, n_pairs), n_pairs)
        xwr_ref[rows, :] = e_r
        xwi_ref[rows, :] = e_i
        return (mu_r * e_r - mu_i * e_i + swr_ref[rows, :],
                mu_r * e_i + mu_i * e_r + swi_ref[rows, :])

    e_r, e_i = lax.fori_loop(0, n_chunks, step, (er_ref[...], ei_ref[...]))
    er_ref[...] = e_r
    ei_ref[...] = e_i

    for q in range(NQ):
        for g8 in range(0, slab_groups, 2):
            gp = (slab_groups * q + g8) // 2
            xr_t = xwr_ref[pl.ds(gp, n_chunks, stride=n_pairs), :].T
            xi_t = xwi_ref[pl.ds(gp, n_chunks, stride=n_pairs), :].T
            for h in range(2):
                g = 2 * gp + h
                x_t = jnp.concatenate([xr_t[h * p:(h + 1) * p], xi_t[h * p:(h + 1) * p]],
                                      axis=0).astype(BF16)
                ytq_ref[g8 + h] = _dot(km_ref[g], zt_ref[g]) + _dot(wx_ref[g], x_t)
        for tau in range(step_rows):
            piece = jnp.concatenate(
                [ytq_ref[g8, tau * gc:(tau + 1) * gc, :] for g8 in range(slab_groups)], axis=0)
            out = piece.T
            if reverse:
                out = out + prev_ref[q, pl.ds(tau, n_chunks, stride=step_rows), :]
            y_ref[q, pl.ds(tau, n_chunks, stride=step_rows), :] = out


def _ssm(s_in, prev, weights, batch, seq, reverse):
    tile = SSM_CHUNK * SSM_CHUNKS
    n_tiles = seq // tile
    n_pairs = SSM_GROUPS // 2
    pos = (lambda k: n_tiles - 1 - k) if reverse else (lambda k: k)
    io_spec = pl.BlockSpec((NQ, tile, LANES), lambda b, k: (0, pos(k), b))
    operands = [s_in] + ([prev] if reverse else []) + list(weights)
    in_specs = [io_spec] + ([io_spec] if reverse else []) + [_const_spec(w.shape) for w in weights]
    wide = SSM_CHUNK * SSM_GROUP
    return pl.pallas_call(
        functools.partial(_ssm_kernel, reverse=reverse),
        grid=(batch, n_tiles),
        in_specs=in_specs,
        out_specs=io_spec,
        out_shape=jax.ShapeDtypeStruct((NQ, seq, batch * LANES), F32),
        scratch_shapes=[pltpu.VMEM((SSM_GROUPS, wide, SSM_CHUNKS), BF16)]
                       + [pltpu.VMEM((SSM_CHUNKS * n_pairs, LANES), F32) for _ in range(4)]
                       + [pltpu.VMEM((LANES // SSM_GROUP, wide, SSM_CHUNKS), F32),
                          pltpu.VMEM((n_pairs, LANES), F32), pltpu.VMEM((n_pairs, LANES), F32)],
        input_output_aliases={1: 0} if reverse else {},
        compiler_params=_params("parallel", "arbitrary"),
        name="s5_scan_bwd" if reverse else "s5_scan_fwd",
    )(*operands)


def _ssm_weights(lam_re, lam_im, log_dt, b_re, b_im, c_re, c_im, reverse):
    n, g, p, gc = SSM_CHUNK, SSM_GROUPS, SSM_STATE, SSM_GROUP
    lam = lax.complex(jnp.minimum(lam_re, LAMBDA_RE_MAX), lam_im)
    z = lam * jnp.exp(log_dt)[:, None]
    lam_bar = jnp.exp(z)
    b_bar = ((lam_bar - 1.0) / lam)[..., None] * lax.complex(b_re, b_im)
    c = lax.complex(c_re, c_im)
    pw = jnp.exp(z[None] * jnp.arange(n + 1, dtype=F32)[:, None, None])
    cp = c[None] * pw[:n, :, None, :]
    hi = lax.Precision.HIGHEST
    kk = (jnp.einsum('kgop,gpi->kgoi', cp.real, b_bar.real, precision=hi)
          - jnp.einsum('kgop,gpi->kgoi', cp.imag, b_bar.imag, precision=hi)).astype(BF16)
    lags = jnp.concatenate([jnp.zeros((n - 1,) + kk.shape[1:], BF16), kk], axis=0)
    base = lags if reverse else jnp.flip(lags, axis=0)
    kmat = jnp.stack([base[n - 1 - a:2 * n - 1 - a] for a in range(n)], axis=0)
    kmat = kmat.transpose(2, 0, 3, 1, 4).reshape(g, n * gc, n * gc)
    pw_in = pw[:n] if reverse else jnp.flip(pw[:n], axis=0)
    sb = pw_in[..., None] * b_bar[None]
    ssum = jnp.concatenate([sb.real, sb.imag], axis=2).astype(BF16)
    ssum = ssum.transpose(1, 2, 0, 3).reshape(g, 2 * p, n * gc)
    pw_out = jnp.flip(pw[1:], axis=0) if reverse else pw[1:]
    cw = c[None] * pw_out[:, :, None, :]
    wx = jnp.concatenate([cw.real, -cw.imag], axis=-1).astype(BF16)
    wx = wx.transpose(1, 0, 2, 3).reshape(g, n * gc, 2 * p)
    mu = pw[n].reshape(g // 2, 2 * p)
    return kmat, ssum, wx, mu.real, mu.imag


def _gelu_tanh(x):
    return 0.5 * x * (1.0 + jnp.tanh(math.sqrt(2.0 / math.pi) * (x + 0.044715 * (x * x * x))))


def _tail_kernel(h_ref, ya_ref, s_ref, ys_ref, ga_ref, gs_ref, p_ref,
                 d_ref, gw_ref, gb_ref, wa_ref, ws_ref, wo_ref,
                 fn_ref, wg_ref, wu_ref, wd_ref,
                 pn_ref, pg_ref, pp_ref, on_ref, o_ref, *, batch):
    t = TIME_TILE
    y_attn = _dot(_stack_batches(ya_ref, batch, ATTN_W), wa_ref[...])

    def slabs(ref):
        return jnp.concatenate(
            [jnp.concatenate([ref[c, :, b * LANES:(b + 1) * LANES] for c in range(NQ)], axis=1)
             for b in range(batch)], axis=0)

    y = _gelu_tanh(d_ref[...] * slabs(s_ref) + slabs(ys_ref))
    y = y * _sigmoid(_dot(y.astype(BF16), gw_ref[...]) + gb_ref[...])
    y_ssm = _dot(y.astype(BF16), ws_ref[...])
    mix = (_stack_batches(ga_ref, batch, D_MODEL).astype(F32) * y_attn
           + _stack_batches(gs_ref, batch, D_MODEL).astype(F32) * y_ssm)
    h = _stack_batches(h_ref, batch, D_MODEL) + _dot(mix.astype(BF16), wo_ref[...])
    h = _ffn_half_step(h, fn_ref, wg_ref, wu_ref, wd_ref)
    gate = _sigmoid(_dot(_rms(h, pn_ref[...]).astype(BF16), pg_ref[...]))
    pe = jnp.concatenate([p_ref[b] for b in range(batch)], axis=0).astype(BF16)
    h = h + _dot(pe, pp_ref[...]) * gate
    out = _rms(h, on_ref[...])
    for b in range(batch):
        o_ref[b] = out[b * t:(b + 1) * t]


def _tail(h, y_attn, s_in, y_scan, g_attn, g_ssm, p, weights, batch, seq):
    t = TIME_TILE
    def spec(c):
        return pl.BlockSpec((t, batch * c), lambda i: (i, 0))
    slab_spec = pl.BlockSpec((NQ, t, batch * LANES), lambda i: (0, i, 0))
    return pl.pallas_call(
        functools.partial(_tail_kernel, batch=batch),
        grid=(seq // t,),
        in_specs=[spec(D_MODEL), spec(ATTN_W), slab_spec, slab_spec,
                  spec(D_MODEL), spec(D_MODEL),
                  pl.BlockSpec((batch, t, p.shape[-1]), lambda i: (0, i, 0))]
                 + [_const_spec(w.shape) for w in weights],
        out_specs=pl.BlockSpec((batch, t, D_MODEL), lambda i: (0, i, 0)),
        out_shape=jax.ShapeDtypeStruct((batch, seq, D_MODEL), F32),
        compiler_params=_params("parallel"),
        name="tail",
    )(h, y_attn, s_in, y_scan, g_attn, g_ssm, p, *weights)


def kernel(x, p, ffn1_norm, ffn1_w_gate, ffn1_w_up, ffn1_w_down, mix_norm, w_in, na_rpb, ssm_lam_re, ssm_lam_im, ssm_log_dt, ssm_b_re, ssm_b_im, ssm_c_re, ssm_c_im, ssm_d, ssm_glu_w, ssm_glu_b, w_attn_out, w_ssm_out, w_out, ffn2_norm, ffn2_w_gate, ffn2_w_up, ffn2_w_down, ple_norm, ple_w_gate, ple_w_proj, final_norm):
    batch, seq, _ = x.shape
    assert p.shape[0] == 1, "single-layer block"
    i = 0
    row = lambda v: v.reshape(1, -1)
    bf = lambda w: w.astype(BF16)
    h = _ffn(x, row(ffn1_norm[i]), bf(ffn1_w_gate[i]), bf(ffn1_w_up[i]), bf(ffn1_w_down[i]),
             batch, seq)
    q, k, v, s_in, g_attn, g_ssm = _inproj(h, row(mix_norm[i]), bf(w_in[i]), batch, seq)
    y_attn = _attention(q, k, v, *_attn_bias_table(na_rpb[i]), batch, seq)
    y_scan = None
    for d in range(2):
        weights = _ssm_weights(
            ssm_lam_re[i, d], ssm_lam_im[i, d], ssm_log_dt[i, d], ssm_b_re[i, d], ssm_b_im[i, d],
            ssm_c_re[i, d], ssm_c_im[i, d], reverse=d == 1)
        y_scan = _ssm(s_in, y_scan, weights, batch, seq, reverse=d == 1)
    weights = [row(ssm_d[i]), bf(ssm_glu_w[i]), row(ssm_glu_b[i]),
               bf(w_attn_out[i]), bf(w_ssm_out[i]), bf(w_out[i]),
               row(ffn2_norm[i]), bf(ffn2_w_gate[i]), bf(ffn2_w_up[i]), bf(ffn2_w_down[i]),
               row(ple_norm[i]), bf(ple_w_gate[i]), bf(ple_w_proj[i]), row(final_norm)]
    return _tail(h, y_attn, s_in, y_scan, g_attn, g_ssm, p[i], weights, batch, seq)
```

```python
import functools
import math

import jax
import jax.numpy as jnp
from jax import lax
from jax.experimental import pallas as pl
from jax.experimental.pallas import tpu as pltpu

D_MODEL = 1024
GRID_W = 64
N_HEADS = 8
HEAD_DIM = 64
ATTN_W = N_HEADS * HEAD_DIM
NA_WIN_H = 8
NA_WIN_W = 16
SSM_W = 512
SSM_GROUP = 16
SSM_GROUPS = SSM_W // SSM_GROUP
SSM_STATE = 64
LAMBDA_RE_MAX = -1e-4
D_FF = 2816
RMS_EPS = 1e-6

LANES = 128
VMEM_LIMIT = 56 * 1024 * 1024
MASK_BIAS = -1e30

TOKEN_TILE = 1024
INPROJ_TILE = 256
TIME_TILE = 128
ATTN_ROWS = NA_WIN_H // 2
ATTN_KEY_ROWS = ATTN_ROWS + NA_WIN_H
FF_CHUNKS = ((0, 1536), (1536, 1280))
SSM_CHUNK = 16
SSM_CHUNKS = LANES
NQ = SSM_W // LANES

BF16 = jnp.bfloat16
F32 = jnp.float32


def _dot(a, b):
    return jnp.dot(a, b, preferred_element_type=F32)


def _rms(x, g):
    ms = jnp.mean(x * x, axis=-1, keepdims=True)
    return x * lax.rsqrt(ms + RMS_EPS) * g


def _sigmoid(x):
    return 1.0 / (1.0 + jnp.exp(-x))


def _const_spec(shape):
    nd = len(shape)
    return pl.BlockSpec(shape, lambda *_: (0,) * nd, pipeline_mode=pl.Buffered(1))


def _params(*sem):
    return pltpu.CompilerParams(dimension_semantics=sem, vmem_limit_bytes=VMEM_LIMIT)


def _ffn_half_step(x, g_ref, wg_ref, wu_ref, wd_ref):
    xn = _rms(x, g_ref[...]).astype(BF16)
    acc = None
    for start, size in FF_CHUNKS:
        gate = _dot(xn, wg_ref[:, start:start + size])
        up = _dot(xn, wu_ref[:, start:start + size])
        act = (gate * _sigmoid(gate) * up).astype(BF16)
        part = _dot(act, wd_ref[start:start + size, :])
        acc = part if acc is None else acc + part
    return x + 0.5 * acc


def _ffn_kernel(x_ref, g_ref, wg_ref, wu_ref, wd_ref, o_ref):
    o_ref[...] = _ffn_half_step(x_ref[...], g_ref, wg_ref, wu_ref, wd_ref)


def _ffn(x, norm_g, wg, wu, wd, batch, seq):
    t = TOKEN_TILE
    return pl.pallas_call(
        _ffn_kernel,
        grid=(batch, seq // t),
        in_specs=[pl.BlockSpec((None, t, D_MODEL), lambda b, i: (b, i, 0)),
                  _const_spec((1, D_MODEL)), _const_spec(wg.shape),
                  _const_spec(wu.shape), _const_spec(wd.shape)],
        out_specs=pl.BlockSpec((t, D_MODEL), lambda b, i: (i, b)),
        out_shape=jax.ShapeDtypeStruct((seq, batch * D_MODEL), F32),
        compiler_params=_params("parallel", "parallel"),
        name="ffn",
    )(x, norm_g, wg, wu, wd)


def _stack_batches(ref, batch, width):
    return jnp.concatenate([ref[:, b * width:(b + 1) * width] for b in range(batch)], axis=0)


def _unstack_batches(ref, x, batch, width):
    t = x.shape[0] // batch
    for b in range(batch):
        ref[:, b * width:(b + 1) * width] = x[b * t:(b + 1) * t]


def _inproj_kernel(h_ref, g_ref, w_ref, q_ref, k_ref, v_ref, s_ref, ga_ref, gs_ref, *, batch):
    t = INPROJ_TILE
    u = _rms(_stack_batches(h_ref, batch, D_MODEL), g_ref[...]).astype(BF16)
    a = ATTN_W
    _unstack_batches(q_ref, (_dot(u, w_ref[:, 0:a]) * (HEAD_DIM ** -0.5)).astype(BF16), batch, a)
    _unstack_batches(k_ref, _dot(u, w_ref[:, a:2 * a]).astype(BF16), batch, a)
    _unstack_batches(v_ref, _dot(u, w_ref[:, 2 * a:3 * a]).astype(BF16), batch, a)
    s0 = 3 * a
    s_in = _dot(u, w_ref[:, s0:s0 + SSM_W])
    for c in range(NQ):
        for b in range(batch):
            s_ref[c, :, b * LANES:(b + 1) * LANES] = s_in[b * t:(b + 1) * t, c * LANES:(c + 1) * LANES]
    g0 = s0 + SSM_W
    _unstack_batches(ga_ref, _sigmoid(_dot(u, w_ref[:, g0:g0 + D_MODEL])).astype(BF16),
                     batch, D_MODEL)
    _unstack_batches(gs_ref, _sigmoid(_dot(u, w_ref[:, g0 + D_MODEL:g0 + 2 * D_MODEL])).astype(BF16),
                     batch, D_MODEL)


def _inproj(h, norm_g, w_in, batch, seq):
    t = INPROJ_TILE
    def spec(c):
        return pl.BlockSpec((t, batch * c), lambda i: (i, 0))
    def shape(c, dt):
        return jax.ShapeDtypeStruct((seq, batch * c), dt)
    return pl.pallas_call(
        functools.partial(_inproj_kernel, batch=batch),
        grid=(seq // t,),
        in_specs=[spec(D_MODEL), _const_spec((1, D_MODEL)), _const_spec(w_in.shape)],
        out_specs=[spec(ATTN_W), spec(ATTN_W), spec(ATTN_W),
                   pl.BlockSpec((NQ, t, batch * LANES), lambda i: (0, i, 0)),
                   spec(D_MODEL), spec(D_MODEL)],
        out_shape=[shape(ATTN_W, BF16), shape(ATTN_W, BF16), shape(ATTN_W, BF16),
                   jax.ShapeDtypeStruct((NQ, seq, batch * LANES), F32),
                   shape(D_MODEL, BF16), shape(D_MODEL, BF16)],
        compiler_params=_params("parallel"),
        name="inproj",
    )(h, norm_g, w_in)


def _attn_kernel(q_ref, k_ref, v_ref, bias_ref, bias_odd_ref, o_ref, *, rows):
    j = pl.program_id(1)
    n_blocks = rows // ATTN_ROWS
    stack = 2 * GRID_W
    lane = lax.broadcasted_iota(jnp.int32, (GRID_W, LANES), 1)
    first = lane < HEAD_DIM

    def stacked_queries(cols):
        pieces = []
        for i in range(ATTN_ROWS):
            q2 = q_ref[i * GRID_W:(i + 1) * GRID_W, cols]
            zero = jnp.zeros_like(q2)
            pieces += [jnp.where(first, q2, zero), jnp.where(first, zero, q2)]
        return jnp.concatenate(pieces, axis=0)

    def scores(lhs, keys):
        return lax.dot_general(lhs, keys, (((1,), (1,)), ((), ())), preferred_element_type=F32)

    def softmax_terms(s):
        p = jnp.exp(s - jnp.max(s, axis=-1, keepdims=True))
        return p.astype(BF16), 1.0 / jnp.sum(p, axis=-1, keepdims=True)

    def write(o, inv, cols):
        o = o * inv
        for i in range(ATTN_ROWS):
            blk = o[i * stack:(i + 1) * stack]
            o_ref[i * GRID_W:(i + 1) * GRID_W, cols] = jnp.where(
                first, blk[0:GRID_W], blk[GRID_W:stack]).astype(BF16)

    is_edge = jnp.logical_or(j == 0, j == n_blocks - 1)

    @pl.when(is_edge)
    def _():
        r0 = j * ATTN_ROWS
        rs = jnp.clip(r0 - NA_WIN_H // 2, 0, rows - NA_WIN_H)
        start = pl.multiple_of(rs * GRID_W, GRID_W)
        nkeys = NA_WIN_H * GRID_W
        for pair in range(N_HEADS // 2):
            cols = slice(pair * LANES, (pair + 1) * LANES)
            hrows = slice(pair * stack, (pair + 1) * stack)
            s = scores(stacked_queries(cols), k_ref[pl.ds(start, nkeys), cols])
            ps, invs = [], []
            for i in range(ATTN_ROWS):
                p, inv = softmax_terms(s[i * stack:(i + 1) * stack] + bias_ref[r0 + i - rs, hrows, :])
                ps.append(p)
                invs.append(inv)
            o = _dot(jnp.concatenate(ps, axis=0), v_ref[pl.ds(start, nkeys), cols])
            write(o, jnp.concatenate(invs, axis=0), cols)

    @pl.when(jnp.logical_not(is_edge))
    def _():
        start = pl.multiple_of((j * ATTN_ROWS - NA_WIN_H // 2) * GRID_W, ATTN_ROWS * GRID_W)
        nkeys = ATTN_KEY_ROWS * GRID_W
        for pair in range(N_HEADS // 2):
            cols = slice(pair * LANES, (pair + 1) * LANES)
            hrows = slice(pair * stack, (pair + 1) * stack)
            s = scores(stacked_queries(cols), k_ref[pl.ds(start, nkeys), cols])
            ps, invs = [], []
            for i in range(ATTN_ROWS):
                lo = i // 2 * 2
                hi = lo + NA_WIN_H + 2 * (i % 2)
                bias = bias_odd_ref[hrows, :] if i % 2 else bias_ref[NA_WIN_H // 2, hrows, :]
                p, inv = softmax_terms(s[i * stack:(i + 1) * stack, lo * GRID_W:hi * GRID_W] + bias)
                parts = [p]
                if lo:
                    parts.insert(0, jnp.zeros((stack, lo * GRID_W), BF16))
                if hi < ATTN_KEY_ROWS:
                    parts.append(jnp.zeros((stack, (ATTN_KEY_ROWS - hi) * GRID_W), BF16))
                ps.append(jnp.concatenate(parts, axis=1) if len(parts) > 1 else p)
                invs.append(inv)
            o = _dot(jnp.concatenate(ps, axis=0), v_ref[pl.ds(start, nkeys), cols])
            write(o, jnp.concatenate(invs, axis=0), cols)


def _attention(q, k, v, bias_tab, bias_odd, batch, seq):
    rows = seq // GRID_W
    qt = ATTN_ROWS * GRID_W
    kv_spec = pl.BlockSpec((seq, ATTN_W), lambda b, r: (0, b))
    return pl.pallas_call(
        functools.partial(_attn_kernel, rows=rows),
        grid=(batch, rows // ATTN_ROWS),
        in_specs=[pl.BlockSpec((qt, ATTN_W), lambda b, r: (r, b)), kv_spec, kv_spec,
                  _const_spec(bias_tab.shape), _const_spec(bias_odd.shape)],
        out_specs=pl.BlockSpec((qt, ATTN_W), lambda b, r: (r, b)),
        out_shape=jax.ShapeDtypeStruct((seq, batch * ATTN_W), BF16),
        compiler_params=_params("parallel", "arbitrary"),
        name="natten",
    )(q, k, v, bias_tab, bias_odd)


def _attn_bias_table(rpb):
    c = jnp.arange(GRID_W)[:, None]
    kc = jnp.arange(GRID_W)[None, :]
    cs = jnp.clip(c - NA_WIN_W // 2, 0, GRID_W - NA_WIN_W)
    inside = (kc >= cs) & (kc < cs + NA_WIN_W)
    dc = kc - c + (NA_WIN_W - 1)
    onehot = (dc[:, :, None] == jnp.arange(2 * NA_WIN_W - 1)[None, None, :]).astype(F32)
    band = jnp.einsum('hyx,ckx->hcyk', rpb.astype(F32), onehot, precision=lax.Precision.HIGHEST)
    band = jnp.where(inside[None, :, None, :], band, MASK_BIAS)
    tabs = [band[:, :, NA_WIN_H - 1 - var:2 * NA_WIN_H - 1 - var, :] for var in range(NA_WIN_H)]
    tab = jnp.stack(tabs, axis=0)
    tab = tab.reshape(NA_WIN_H, N_HEADS * GRID_W, NA_WIN_H * GRID_W)
    pad = jnp.full((N_HEADS * GRID_W, GRID_W), MASK_BIAS, F32)
    tab_odd = jnp.concatenate([pad, tab[NA_WIN_H // 2], pad], axis=1)
    return tab, tab_odd


def _ssm_kernel(*refs, reverse):
    if reverse:
        s_ref, prev_ref = refs[:2]
        refs = refs[2:]
    else:
        s_ref, prev_ref = refs[0], None
        refs = refs[1:]
    (km_ref, ss_ref, wx_ref, mur_ref, mui_ref, y_ref,
     zt_ref, swr_ref, swi_ref, xwr_ref, xwi_ref, ytq_ref, er_ref, ei_ref) = refs
    step_rows, n_chunks, gc, p = SSM_CHUNK, SSM_CHUNKS, SSM_GROUP, SSM_STATE
    n_pairs = SSM_GROUPS // 2
    slab_groups = LANES // gc

    @pl.when(pl.program_id(1) == 0)
    def _():
        er_ref[...] = jnp.zeros_like(er_ref)
        ei_ref[...] = jnp.zeros_like(ei_ref)

    for q in range(NQ):
        for tau in range(step_rows):
            at = s_ref[q, pl.ds(tau, n_chunks, stride=step_rows), :].T.astype(BF16)
            for g8 in range(slab_groups):
                zt_ref[slab_groups * q + g8, tau * gc:(tau + 1) * gc, :] = at[g8 * gc:(g8 + 1) * gc, :]

    for gp in range(n_pairs):
        r0 = _dot(ss_ref[2 * gp], zt_ref[2 * gp])
        r1 = _dot(ss_ref[2 * gp + 1], zt_ref[2 * gp + 1])
        swr_ref[pl.ds(gp, n_chunks, stride=n_pairs), :] = jnp.concatenate(
            [r0[0:p], r1[0:p]], axis=0).T
        swi_ref[pl.ds(gp, n_chunks, stride=n_pairs), :] = jnp.concatenate(
            [r0[p:2 * p], r1[p:2 * p]], axis=0).T

    mu_r = mur_ref[...]
    mu_i = mui_ref[...]

    def step(j, carry):
        e_r, e_i = carry
        chunk = n_chunks - 1 - j if reverse else j
        rows = pl.ds(pl.multiple_of(chunk * n_pairs, n_pairs), n_pairs)
        xwr_ref[rows, :] = e_r
        xwi_ref[rows, :] = e_i
        return (mu_r * e_r - mu_i * e_i + swr_ref[rows, :],
                mu_r * e_i + mu_i * e_r + swi_ref[rows, :])

    e_r, e_i = lax.fori_loop(0, n_chunks, step, (er_ref[...], ei_ref[...]))
    er_ref[...] = e_r
    ei_ref[...] = e_i

    for q in range(NQ):
        for g8 in range(0, slab_groups, 2):
            gp = (slab_groups * q + g8) // 2
            xr_t = xwr_ref[pl.ds(gp, n_chunks, stride=n_pairs), :].T
            xi_t = xwi_ref[pl.ds(gp, n_chunks, stride=n_pairs), :].T
            for h in range(2):
                g = 2 * gp + h
                x_t = jnp.concatenate([xr_t[h * p:(h + 1) * p], xi_t[h * p:(h + 1) * p]],
                                      axis=0).astype(BF16)
                ytq_ref[g8 + h] = _dot(km_ref[g], zt_ref[g]) + _dot(wx_ref[g], x_t)
        for tau in range(step_rows):
            piece = jnp.concatenate(
                [ytq_ref[g8, tau * gc:(tau + 1) * gc, :] for g8 in range(slab_groups)], axis=0)
            out = piece.T
            if reverse:
                out = out + prev_ref[q, pl.ds(tau, n_chunks, stride=step_rows), :]
            y_ref[q, pl.ds(tau, n_chunks, stride=step_rows), :] = out


def _ssm(s_in, prev, weights, batch, seq, reverse):
    tile = SSM_CHUNK * SSM_CHUNKS
    n_tiles = seq // tile
    n_pairs = SSM_GROUPS // 2
    pos = (lambda k: n_tiles - 1 - k) if reverse else (lambda k: k)
    io_spec = pl.BlockSpec((NQ, tile, LANES), lambda b, k: (0, pos(k), b))
    operands = [s_in] + ([prev] if reverse else []) + list(weights)
    in_specs = [io_spec] + ([io_spec] if reverse else []) + [_const_spec(w.shape) for w in weights]
    wide = SSM_CHUNK * SSM_GROUP
    return pl.pallas_call(
        functools.partial(_ssm_kernel, reverse=reverse),
        grid=(batch, n_tiles),
        in_specs=in_specs,
        out_specs=io_spec,
        out_shape=jax.ShapeDtypeStruct((NQ, seq, batch * LANES), F32),
        scratch_shapes=[pltpu.VMEM((SSM_GROUPS, wide, SSM_CHUNKS), BF16)]
                       + [pltpu.VMEM((SSM_CHUNKS * n_pairs, LANES), F32) for _ in range(4)]
                       + [pltpu.VMEM((LANES // SSM_GROUP, wide, SSM_CHUNKS), F32),
                          pltpu.VMEM((n_pairs, LANES), F32), pltpu.VMEM((n_pairs, LANES), F32)],
        input_output_aliases={1: 0} if reverse else {},
        compiler_params=_params("parallel", "arbitrary"),
        name="s5_scan_bwd" if reverse else "s5_scan_fwd",
    )(*operands)


def _ssm_weights(lam_re, lam_im, log_dt, b_re, b_im, c_re, c_im, reverse):
    n, g, p, gc = SSM_CHUNK, SSM_GROUPS, SSM_STATE, SSM_GROUP
    lam = lax.complex(jnp.minimum(lam_re, LAMBDA_RE_MAX), lam_im)
    z = lam * jnp.exp(log_dt)[:, None]
    lam_bar = jnp.exp(z)
    b_bar = ((lam_bar - 1.0) / lam)[..., None] * lax.complex(b_re, b_im)
    c = lax.complex(c_re, c_im)
    pw = jnp.exp(z[None] * jnp.arange(n + 1, dtype=F32)[:, None, None])
    pw_g = pw.transpose(1, 0, 2)
    cp = c[:, :, None, :] * pw_g[:, None, :n, :]
    hi = lax.Precision.HIGHEST
    kk = (jnp.einsum('gokp,gpi->goki', cp.real, b_bar.real, precision=hi)
          - jnp.einsum('gokp,gpi->goki', cp.imag, b_bar.imag, precision=hi)).astype(BF16)
    kk = kk if reverse else jnp.flip(kk, axis=2)
    zeros = jnp.zeros((g, gc, (n - 1) * gc), BF16)
    kk = kk.reshape(g, gc, n * gc)
    lag_row = jnp.concatenate([zeros, kk] if reverse else [kk, zeros], axis=-1)
    kmat = jnp.stack([lag_row[:, :, (n - 1 - a) * gc:(2 * n - 1 - a) * gc] for a in range(n)], axis=1)
    kmat = kmat.reshape(g, n * gc, n * gc)
    pw_in = pw_g[:, :n] if reverse else jnp.flip(pw_g[:, :n], axis=1)
    sb = (jnp.repeat(pw_in.transpose(0, 2, 1), gc, axis=-1)
          * jnp.tile(b_bar, (1, 1, n)))
    ssum = jnp.concatenate([sb.real, sb.imag], axis=1).astype(BF16)
    pw_out = jnp.flip(pw_g[:, 1:], axis=1) if reverse else pw_g[:, 1:]
    cw = (pw_out[:, :, None, :] * c[:, None, :, :]).reshape(g, n * gc, p)
    wx = jnp.concatenate([cw.real, -cw.imag], axis=-1).astype(BF16)
    mu = pw[n].reshape(g // 2, 2 * p)
    return kmat, ssum, wx, mu.real, mu.imag


def _gelu_tanh(x):
    return 0.5 * x * (1.0 + jnp.tanh(math.sqrt(2.0 / math.pi) * (x + 0.044715 * (x * x * x))))


def _tail_kernel(h_ref, ya_ref, s_ref, ys_ref, ga_ref, gs_ref, p_ref,
                 d_ref, gw_ref, gb_ref, wa_ref, ws_ref, wo_ref,
                 fn_ref, wg_ref, wu_ref, wd_ref,
                 pn_ref, pg_ref, pp_ref, on_ref, o_ref, *, batch):
    t = TIME_TILE
    y_attn = _dot(_stack_batches(ya_ref, batch, ATTN_W), wa_ref[...])

    def slabs(ref):
        return jnp.concatenate(
            [jnp.concatenate([ref[c, :, b * LANES:(b + 1) * LANES] for c in range(NQ)], axis=1)
             for b in range(batch)], axis=0)

    y = _gelu_tanh(d_ref[...] * slabs(s_ref) + slabs(ys_ref))
    y = y * _sigmoid(_dot(y.astype(BF16), gw_ref[...]) + gb_ref[...])
    y_ssm = _dot(y.astype(BF16), ws_ref[...])
    mix = (_stack_batches(ga_ref, batch, D_MODEL).astype(F32) * y_attn
           + _stack_batches(gs_ref, batch, D_MODEL).astype(F32) * y_ssm)
    h = _stack_batches(h_ref, batch, D_MODEL) + _dot(mix.astype(BF16), wo_ref[...])
    h = _ffn_half_step(h, fn_ref, wg_ref, wu_ref, wd_ref)
    gate = _sigmoid(_dot(_rms(h, pn_ref[...]).astype(BF16), pg_ref[...]))
    pe = jnp.concatenate([p_ref[b] for b in range(batch)], axis=0).astype(BF16)
    h = h + _dot(pe, pp_ref[...]) * gate
    out = _rms(h, on_ref[...])
    for b in range(batch):
        o_ref[b] = out[b * t:(b + 1) * t]


def _tail(h, y_attn, s_in, y_scan, g_attn, g_ssm, p, weights, batch, seq):
    t = TIME_TILE
    def spec(c):
        return pl.BlockSpec((t, batch * c), lambda i: (i, 0))
    slab_spec = pl.BlockSpec((NQ, t, batch * LANES), lambda i: (0, i, 0))
    return pl.pallas_call(
        functools.partial(_tail_kernel, batch=batch),
        grid=(seq // t,),
        in_specs=[spec(D_MODEL), spec(ATTN_W), slab_spec, slab_spec,
                  spec(D_MODEL), spec(D_MODEL),
                  pl.BlockSpec((batch, t, p.shape[-1]), lambda i: (0, i, 0))]
                 + [_const_spec(w.shape) for w in weights],
        out_specs=pl.BlockSpec((batch, t, D_MODEL), lambda i: (0, i, 0)),
        out_shape=jax.ShapeDtypeStruct((batch, seq, D_MODEL), F32),
        compiler_params=_params("parallel"),
        name="tail",
    )(h, y_attn, s_in, y_scan, g_attn, g_ssm, p, *weights)


def kernel(x, p, ffn1_norm, ffn1_w_gate, ffn1_w_up, ffn1_w_down, mix_norm, w_in, na_rpb, ssm_lam_re, ssm_lam_im, ssm_log_dt, ssm_b_re, ssm_b_im, ssm_c_re, ssm_c_im, ssm_d, ssm_glu_w, ssm_glu_b, w_attn_out, w_ssm_out, w_out, ffn2_norm, ffn2_w_gate, ffn2_w_up, ffn2_w_down, ple_norm, ple_w_gate, ple_w_proj, final_norm):
    batch, seq, _ = x.shape
    assert p.shape[0] == 1, "single-layer block"
    i = 0
    row = lambda v: v.reshape(1, -1)
    bf = lambda w: w.astype(BF16)
    h = _ffn(x, row(ffn1_norm[i]), bf(ffn1_w_gate[i]), bf(ffn1_w_up[i]), bf(ffn1_w_down[i]),
             batch, seq)
    q, k, v, s_in, g_attn, g_ssm = _inproj(h, row(mix_norm[i]), bf(w_in[i]), batch, seq)
    y_attn = _attention(q, k, v, *_attn_bias_table(na_rpb[i]), batch, seq)
    y_scan = None
    for d in range(2):
        weights = _ssm_weights(
            ssm_lam_re[i, d], ssm_lam_im[i, d], ssm_log_dt[i, d], ssm_b_re[i, d], ssm_b_im[i, d],
            ssm_c_re[i, d], ssm_c_im[i, d], reverse=d == 1)
        y_scan = _ssm(s_in, y_scan, weights, batch, seq, reverse=d == 1)
    weights = [row(ssm_d[i]), bf(ssm_glu_w[i]), row(ssm_glu_b[i]),
               bf(w_attn_out[i]), bf(w_ssm_out[i]), bf(w_out[i]),
               row(ffn2_norm[i]), bf(ffn2_w_gate[i]), bf(ffn2_w_up[i]), bf(ffn2_w_down[i]),
               row(ple_norm[i]), bf(ple_w_gate[i]), bf(ple_w_proj[i]), row(final_norm)]
    return _tail(h, y_attn, s_in, y_scan, g_attn, g_ssm, p[i], weights, batch, seq)
```

```python
import functools
import math

import jax
import jax.numpy as jnp
from jax import lax
from jax.experimental import pallas as pl
from jax.experimental.pallas import tpu as pltpu

D_MODEL = 1024
GRID_W = 64
N_HEADS = 8
HEAD_DIM = 64
ATTN_W = N_HEADS * HEAD_DIM
NA_WIN_H = 8
NA_WIN_W = 16
SSM_W = 512
SSM_GROUP = 16
SSM_GROUPS = SSM_W // SSM_GROUP
SSM_STATE = 64
LAMBDA_RE_MAX = -1e-4
D_FF = 2816
RMS_EPS = 1e-6

LANES = 128
VMEM_LIMIT = 56 * 1024 * 1024
MASK_BIAS = -1e30

TOKEN_TILE = 1024
INPROJ_TILE = 256
TIME_TILE = 128
ATTN_ROWS = NA_WIN_H // 2
ATTN_KEY_ROWS = ATTN_ROWS + NA_WIN_H
FF_CHUNKS = ((0, 1536), (1536, 1280))
SSM_CHUNK = 16
SSM_CHUNKS = LANES
NQ = SSM_W // LANES

BF16 = jnp.bfloat16
F32 = jnp.float32


def _dot(a, b):
    return jnp.dot(a, b, preferred_element_type=F32)


def _rms(x, g):
    ms = jnp.mean(x * x, axis=-1, keepdims=True)
    return x * lax.rsqrt(ms + RMS_EPS) * g


def _sigmoid(x):
    return 1.0 / (1.0 + jnp.exp(-x))


def _const_spec(shape):
    nd = len(shape)
    return pl.BlockSpec(shape, lambda *_: (0,) * nd, pipeline_mode=pl.Buffered(1))


def _params(*sem):
    return pltpu.CompilerParams(dimension_semantics=sem, vmem_limit_bytes=VMEM_LIMIT)


def _ffn_half_step(x, g_ref, wg_ref, wu_ref, wd_ref):
    xn = _rms(x, g_ref[...]).astype(BF16)
    acc = None
    for start, size in FF_CHUNKS:
        gate = _dot(xn, wg_ref[:, start:start + size])
        up = _dot(xn, wu_ref[:, start:start + size])
        act = (gate * _sigmoid(gate) * up).astype(BF16)
        part = _dot(act, wd_ref[start:start + size, :])
        acc = part if acc is None else acc + part
    return x + 0.5 * acc


def _ffn_kernel(x_ref, g_ref, wg_ref, wu_ref, wd_ref, o_ref):
    o_ref[...] = _ffn_half_step(x_ref[...], g_ref, wg_ref, wu_ref, wd_ref)


def _ffn(x, norm_g, wg, wu, wd, batch, seq):
    t = TOKEN_TILE
    return pl.pallas_call(
        _ffn_kernel,
        grid=(batch, seq // t),
        in_specs=[pl.BlockSpec((None, t, D_MODEL), lambda b, i: (b, i, 0)),
                  _const_spec((1, D_MODEL)), _const_spec(wg.shape),
                  _const_spec(wu.shape), _const_spec(wd.shape)],
        out_specs=pl.BlockSpec((t, D_MODEL), lambda b, i: (i, b)),
        out_shape=jax.ShapeDtypeStruct((seq, batch * D_MODEL), F32),
        compiler_params=_params("parallel", "parallel"),
        name="ffn",
    )(x, norm_g, wg, wu, wd)


def _stack_batches(ref, batch, width):
    return jnp.concatenate([ref[:, b * width:(b + 1) * width] for b in range(batch)], axis=0)


def _unstack_batches(ref, x, batch, width):
    t = x.shape[0] // batch
    for b in range(batch):
        ref[:, b * width:(b + 1) * width] = x[b * t:(b + 1) * t]


def _inproj_kernel(h_ref, g_ref, w_ref, q_ref, k_ref, v_ref, s_ref, ga_ref, gs_ref, *, batch):
    t = INPROJ_TILE
    u = _rms(_stack_batches(h_ref, batch, D_MODEL), g_ref[...]).astype(BF16)
    a = ATTN_W
    _unstack_batches(q_ref, (_dot(u, w_ref[:, 0:a]) * (HEAD_DIM ** -0.5)).astype(BF16), batch, a)
    _unstack_batches(k_ref, _dot(u, w_ref[:, a:2 * a]).astype(BF16), batch, a)
    _unstack_batches(v_ref, _dot(u, w_ref[:, 2 * a:3 * a]).astype(BF16), batch, a)
    s0 = 3 * a
    s_in = _dot(u, w_ref[:, s0:s0 + SSM_W])
    for c in range(NQ):
        for b in range(batch):
            s_ref[c, :, b * LANES:(b + 1) * LANES] = s_in[b * t:(b + 1) * t, c * LANES:(c + 1) * LANES]
    g0 = s0 + SSM_W
    _unstack_batches(ga_ref, _sigmoid(_dot(u, w_ref[:, g0:g0 + D_MODEL])).astype(BF16),
                     batch, D_MODEL)
    _unstack_batches(gs_ref, _sigmoid(_dot(u, w_ref[:, g0 + D_MODEL:g0 + 2 * D_MODEL])).astype(BF16),
                     batch, D_MODEL)


def _inproj(h, norm_g, w_in, batch, seq):
    t = INPROJ_TILE
    def spec(c):
        return pl.BlockSpec((t, batch * c), lambda i: (i, 0))
    def shape(c, dt):
        return jax.ShapeDtypeStruct((seq, batch * c), dt)
    return pl.pallas_call(
        functools.partial(_inproj_kernel, batch=batch),
        grid=(seq // t,),
        in_specs=[spec(D_MODEL), _const_spec((1, D_MODEL)), _const_spec(w_in.shape)],
        out_specs=[spec(ATTN_W), spec(ATTN_W), spec(ATTN_W),
                   pl.BlockSpec((NQ, t, batch * LANES), lambda i: (0, i, 0)),
                   spec(D_MODEL), spec(D_MODEL)],
        out_shape=[shape(ATTN_W, BF16), shape(ATTN_W, BF16), shape(ATTN_W, BF16),
                   jax.ShapeDtypeStruct((NQ, seq, batch * LANES), F32),
                   shape(D_MODEL, BF16), shape(D_MODEL, BF16)],
        compiler_params=_params("parallel"),
        name="inproj",
    )(h, norm_g, w_in)


def _attn_kernel(q_ref, k_ref, v_ref, bias_ref, bias_odd_ref, o_ref, *, rows):
    j = pl.program_id(1)
    n_blocks = rows // ATTN_ROWS
    stack = 2 * GRID_W
    lane = lax.broadcasted_iota(jnp.int32, (GRID_W, LANES), 1)
    first = lane < HEAD_DIM

    def stacked_queries(cols):
        pieces = []
        for i in range(ATTN_ROWS):
            q2 = q_ref[i * GRID_W:(i + 1) * GRID_W, cols]
            zero = jnp.zeros_like(q2)
            pieces += [jnp.where(first, q2, zero), jnp.where(first, zero, q2)]
        return jnp.concatenate(pieces, axis=0)

    def scores(lhs, keys):
        return lax.dot_general(lhs, keys, (((1,), (1,)), ((), ())), preferred_element_type=F32)

    def softmax_terms(s):
        p = jnp.exp(s - jnp.max(s, axis=-1, keepdims=True))
        return p.astype(BF16), 1.0 / jnp.sum(p, axis=-1, keepdims=True)

    def write(o, inv, cols):
        o = o * inv
        for i in range(ATTN_ROWS):
            blk = o[i * stack:(i + 1) * stack]
            o_ref[i * GRID_W:(i + 1) * GRID_W, cols] = jnp.where(
                first, blk[0:GRID_W], blk[GRID_W:stack]).astype(BF16)

    is_edge = jnp.logical_or(j == 0, j == n_blocks - 1)

    @pl.when(is_edge)
    def _():
        r0 = j * ATTN_ROWS
        rs = jnp.clip(r0 - NA_WIN_H // 2, 0, rows - NA_WIN_H)
        start = pl.multiple_of(rs * GRID_W, GRID_W)
        nkeys = NA_WIN_H * GRID_W
        for pair in range(N_HEADS // 2):
            cols = slice(pair * LANES, (pair + 1) * LANES)
            hrows = slice(pair * stack, (pair + 1) * stack)
            s = scores(stacked_queries(cols), k_ref[pl.ds(start, nkeys), cols])
            ps, invs = [], []
            for i in range(ATTN_ROWS):
                p, inv = softmax_terms(s[i * stack:(i + 1) * stack] + bias_ref[r0 + i - rs, hrows, :])
                ps.append(p)
                invs.append(inv)
            o = _dot(jnp.concatenate(ps, axis=0), v_ref[pl.ds(start, nkeys), cols])
            write(o, jnp.concatenate(invs, axis=0), cols)

    @pl.when(jnp.logical_not(is_edge))
    def _():
        start = pl.multiple_of((j * ATTN_ROWS - NA_WIN_H // 2) * GRID_W, ATTN_ROWS * GRID_W)
        nkeys = ATTN_KEY_ROWS * GRID_W
        for pair in range(N_HEADS // 2):
            cols = slice(pair * LANES, (pair + 1) * LANES)
            hrows = slice(pair * stack, (pair + 1) * stack)
            s = scores(stacked_queries(cols), k_ref[pl.ds(start, nkeys), cols])
            ps, invs = [], []
            for i in range(ATTN_ROWS):
                lo = i // 2 * 2
                hi = lo + NA_WIN_H + 2 * (i % 2)
                bias = bias_odd_ref[hrows, :] if i % 2 else bias_ref[NA_WIN_H // 2, hrows, :]
                p, inv = softmax_terms(s[i * stack:(i + 1) * stack, lo * GRID_W:hi * GRID_W] + bias)
                parts = [p]
                if lo:
                    parts.insert(0, jnp.zeros((stack, lo * GRID_W), BF16))
                if hi < ATTN_KEY_ROWS:
                    parts.append(jnp.zeros((stack, (ATTN_KEY_ROWS - hi) * GRID_W), BF16))
                ps.append(jnp.concatenate(parts, axis=1) if len(parts) > 1 else p)
                invs.append(inv)
            o = _dot(jnp.concatenate(ps, axis=0), v_ref[pl.ds(start, nkeys), cols])
            write(o, jnp.concatenate(invs, axis=0), cols)


def _attention(q, k, v, bias_tab, bias_odd, batch, seq):
    rows = seq // GRID_W
    qt = ATTN_ROWS * GRID_W
    kv_spec = pl.BlockSpec((seq, ATTN_W), lambda b, r: (0, b))
    return pl.pallas_call(
        functools.partial(_attn_kernel, rows=rows),
        grid=(batch, rows // ATTN_ROWS),
        in_specs=[pl.BlockSpec((qt, ATTN_W), lambda b, r: (r, b)), kv_spec, kv_spec,
                  _const_spec(bias_tab.shape), _const_spec(bias_odd.shape)],
        out_specs=pl.BlockSpec((qt, ATTN_W), lambda b, r: (r, b)),
        out_shape=jax.ShapeDtypeStruct((seq, batch * ATTN_W), BF16),
        compiler_params=_params("parallel", "arbitrary"),
        name="natten",
    )(q, k, v, bias_tab, bias_odd)


def _attn_bias_table(rpb):
    c = jnp.arange(GRID_W)[:, None]
    kc = jnp.arange(GRID_W)[None, :]
    cs = jnp.clip(c - NA_WIN_W // 2, 0, GRID_W - NA_WIN_W)
    inside = (kc >= cs) & (kc < cs + NA_WIN_W)
    dc = kc - c + (NA_WIN_W - 1)
    onehot = (dc[:, :, None] == jnp.arange(2 * NA_WIN_W - 1)[None, None, :]).astype(F32)
    band = jnp.einsum('hyx,ckx->hcyk', rpb.astype(F32), onehot, precision=lax.Precision.HIGHEST)
    band = jnp.where(inside[None, :, None, :], band, MASK_BIAS)
    tabs = [band[:, :, NA_WIN_H - 1 - var:2 * NA_WIN_H - 1 - var, :] for var in range(NA_WIN_H)]
    tab = jnp.stack(tabs, axis=0)
    tab = tab.reshape(NA_WIN_H, N_HEADS * GRID_W, NA_WIN_H * GRID_W)
    pad = jnp.full((N_HEADS * GRID_W, GRID_W), MASK_BIAS, F32)
    tab_odd = jnp.concatenate([pad, tab[NA_WIN_H // 2], pad], axis=1)
    return tab, tab_odd


def _ssm_kernel(*refs, reverse):
    if reverse:
        s_ref, prev_ref = refs[:2]
        refs = refs[2:]
    else:
        s_ref, prev_ref = refs[0], None
        refs = refs[1:]
    (km_ref, ss_ref, wx_ref, mur_ref, mui_ref, y_ref,
     zt_ref, swr_ref, swi_ref, xwr_ref, xwi_ref, ytq_ref, er_ref, ei_ref) = refs
    step_rows, n_chunks, gc, p = SSM_CHUNK, SSM_CHUNKS, SSM_GROUP, SSM_STATE
    n_pairs = SSM_GROUPS // 2
    slab_groups = LANES // gc

    @pl.when(pl.program_id(1) == 0)
    def _():
        er_ref[...] = jnp.zeros_like(er_ref)
        ei_ref[...] = jnp.zeros_like(ei_ref)

    for q in range(NQ):
        for tau in range(step_rows):
            at = s_ref[q, pl.ds(tau, n_chunks, stride=step_rows), :].T.astype(BF16)
            for g8 in range(slab_groups):
                zt_ref[slab_groups * q + g8, tau * gc:(tau + 1) * gc, :] = at[g8 * gc:(g8 + 1) * gc, :]

    for gp in range(n_pairs):
        r0 = _dot(ss_ref[2 * gp], zt_ref[2 * gp])
        r1 = _dot(ss_ref[2 * gp + 1], zt_ref[2 * gp + 1])
        swr_ref[pl.ds(gp, n_chunks, stride=n_pairs), :] = jnp.concatenate(
            [r0[0:p], r1[0:p]], axis=0).T
        swi_ref[pl.ds(gp, n_chunks, stride=n_pairs), :] = jnp.concatenate(
            [r0[p:2 * p], r1[p:2 * p]], axis=0).T

    mu_r = mur_ref[...]
    mu_i = mui_ref[...]

    def step(j, carry):
        e_r, e_i = carry
        chunk = n_chunks - 1 - j if reverse else j
        rows = pl.ds(pl.multiple_of(chunk * n_pairs, n_pairs), n_pairs)
        xwr_ref[rows, :] = e_r
        xwi_ref[rows, :] = e_i
        return (mu_r * e_r - mu_i * e_i + swr_ref[rows, :],
                mu_r * e_i + mu_i * e_r + swi_ref[rows, :])

    e_r, e_i = lax.fori_loop(0, n_chunks, step, (er_ref[...], ei_ref[...]))
    er_ref[...] = e_r
    ei_ref[...] = e_i

    for q in range(NQ):
        for g8 in range(0, slab_groups, 2):
            gp = (slab_groups * q + g8) // 2
            xr_t = xwr_ref[pl.ds(gp, n_chunks, stride=n_pairs), :].T
            xi_t = xwi_ref[pl.ds(gp, n_chunks, stride=n_pairs), :].T
            for h in range(2):
                g = 2 * gp + h
                x_t = jnp.concatenate([xr_t[h * p:(h + 1) * p], xi_t[h * p:(h + 1) * p]],
                                      axis=0).astype(BF16)
                ytq_ref[g8 + h] = _dot(km_ref[g], zt_ref[g]) + _dot(wx_ref[g], x_t)
        for tau in range(step_rows):
            piece = jnp.concatenate(
                [ytq_ref[g8, tau * gc:(tau + 1) * gc, :] for g8 in range(slab_groups)], axis=0)
            out = piece.T
            if reverse:
                out = out + prev_ref[q, pl.ds(tau, n_chunks, stride=step_rows), :]
            y_ref[q, pl.ds(tau, n_chunks, stride=step_rows), :] = out


def _ssm(s_in, prev, weights, batch, seq, reverse):
    tile = SSM_CHUNK * SSM_CHUNKS
    n_tiles = seq // tile
    n_pairs = SSM_GROUPS // 2
    pos = (lambda k: n_tiles - 1 - k) if reverse else (lambda k: k)
    io_spec = pl.BlockSpec((NQ, tile, LANES), lambda b, k: (0, pos(k), b))
    direction = 1 if reverse else 0
    def weight_spec(w):
        return pl.BlockSpec((None,) + w.shape[1:], lambda b, k: (direction,) + (0,) * (w.ndim - 1),
                            pipeline_mode=pl.Buffered(1))
    operands = [s_in] + ([prev] if reverse else []) + list(weights)
    in_specs = [io_spec] + ([io_spec] if reverse else []) + [weight_spec(w) for w in weights]
    wide = SSM_CHUNK * SSM_GROUP
    return pl.pallas_call(
        functools.partial(_ssm_kernel, reverse=reverse),
        grid=(batch, n_tiles),
        in_specs=in_specs,
        out_specs=io_spec,
        out_shape=jax.ShapeDtypeStruct((NQ, seq, batch * LANES), F32),
        scratch_shapes=[pltpu.VMEM((SSM_GROUPS, wide, SSM_CHUNKS), BF16)]
                       + [pltpu.VMEM((SSM_CHUNKS * n_pairs, LANES), F32) for _ in range(4)]
                       + [pltpu.VMEM((LANES // SSM_GROUP, wide, SSM_CHUNKS), F32),
                          pltpu.VMEM((n_pairs, LANES), F32), pltpu.VMEM((n_pairs, LANES), F32)],
        input_output_aliases={1: 0} if reverse else {},
        compiler_params=_params("parallel", "arbitrary"),
        name="s5_scan_bwd" if reverse else "s5_scan_fwd",
    )(*operands)


def _ssm_weights(lam_re, lam_im, log_dt, b_re, b_im, c_re, c_im):
    n, g, p, gc = SSM_CHUNK, SSM_GROUPS, SSM_STATE, SSM_GROUP
    wide = n * gc
    lam = lax.complex(jnp.minimum(lam_re, LAMBDA_RE_MAX), lam_im)
    z = lam * jnp.exp(log_dt)[..., None]
    b_bar = ((jnp.exp(z) - 1.0) / lam)[..., None] * lax.complex(b_re, b_im)
    c = lax.complex(c_re, c_im)
    steps = jnp.arange(n, dtype=F32)
    fwd = jnp.array([1.0, 0.0], F32)[:, None]
    cp = c[:, :, :, None, :] * jnp.exp(z[:, :, None, None, :] * steps[:, None])
    hi = lax.Precision.HIGHEST
    kk = (jnp.einsum('dgokp,dgpi->dgoki', cp.real, b_bar.real, precision=hi)
          - jnp.einsum('dgokp,dgpi->dgoki', cp.imag, b_bar.imag, precision=hi))
    kk = kk.astype(BF16).reshape(2, g * gc, wide)
    src = jnp.arange(wide)
    t_out = jnp.arange(n)[None, :, None, None]
    lag = (t_out - src[None, None, None, :] // gc) * jnp.array([1, -1])[:, None, None, None]
    place = jnp.logical_and(src[None, None, :, None] // gc == lag,
                            src[None, None, :, None] % gc == src[None, None, None, :] % gc)
    kmat = jnp.einsum('drq,daqc->darc', kk, place.astype(BF16), preferred_element_type=F32)
    kmat = kmat.astype(BF16).reshape(2, n, g, gc, wide).transpose(0, 2, 1, 3, 4).reshape(2, g, wide, wide)
    e_in = fwd * (n - 1 - steps) + (1.0 - fwd) * steps
    sb = (jnp.exp(z[..., None] * e_in[:, None, None, :])[..., None]
          * b_bar[:, :, :, None, :])
    ssum = jnp.concatenate([sb.real, sb.imag], axis=2).astype(BF16).reshape(2, g, 2 * p, wide)
    e_out = fwd * (steps + 1.0) + (1.0 - fwd) * (n - steps)
    cw = (jnp.exp(z[:, :, None, None, :] * e_out[:, None, :, None, None])
          * c[:, :, None, :, :])
    wx = jnp.concatenate([cw.real, -cw.imag], axis=-1).astype(BF16).reshape(2, g, wide, 2 * p)
    mu = jnp.exp(z * n).reshape(2, g // 2, 2 * p)
    return kmat, ssum, wx, mu.real, mu.imag


def _gelu_tanh(x):
    return 0.5 * x * (1.0 + jnp.tanh(math.sqrt(2.0 / math.pi) * (x + 0.044715 * (x * x * x))))


def _tail_kernel(h_ref, ya_ref, s_ref, ys_ref, ga_ref, gs_ref, p_ref,
                 d_ref, gw_ref, gb_ref, wa_ref, ws_ref, wo_ref,
                 fn_ref, wg_ref, wu_ref, wd_ref,
                 pn_ref, pg_ref, pp_ref, on_ref, o_ref, *, batch):
    t = TIME_TILE
    y_attn = _dot(_stack_batches(ya_ref, batch, ATTN_W), wa_ref[...])

    def slabs(ref):
        return jnp.concatenate(
            [jnp.concatenate([ref[c, :, b * LANES:(b + 1) * LANES] for c in range(NQ)], axis=1)
             for b in range(batch)], axis=0)

    y = _gelu_tanh(d_ref[...] * slabs(s_ref) + slabs(ys_ref))
    y = y * _sigmoid(_dot(y.astype(BF16), gw_ref[...]) + gb_ref[...])
    y_ssm = _dot(y.astype(BF16), ws_ref[...])
    mix = (_stack_batches(ga_ref, batch, D_MODEL).astype(F32) * y_attn
           + _stack_batches(gs_ref, batch, D_MODEL).astype(F32) * y_ssm)
    h = _stack_batches(h_ref, batch, D_MODEL) + _dot(mix.astype(BF16), wo_ref[...])
    h = _ffn_half_step(h, fn_ref, wg_ref, wu_ref, wd_ref)
    gate = _sigmoid(_dot(_rms(h, pn_ref[...]).astype(BF16), pg_ref[...]))
    pe = jnp.concatenate([p_ref[b] for b in range(batch)], axis=0).astype(BF16)
    h = h + _dot(pe, pp_ref[...]) * gate
    out = _rms(h, on_ref[...])
    for b in range(batch):
        o_ref[b] = out[b * t:(b + 1) * t]


def _tail(h, y_attn, s_in, y_scan, g_attn, g_ssm, p, weights, batch, seq):
    t = TIME_TILE
    def spec(c):
        return pl.BlockSpec((t, batch * c), lambda i: (i, 0))
    slab_spec = pl.BlockSpec((NQ, t, batch * LANES), lambda i: (0, i, 0))
    return pl.pallas_call(
        functools.partial(_tail_kernel, batch=batch),
        grid=(seq // t,),
        in_specs=[spec(D_MODEL), spec(ATTN_W), slab_spec, slab_spec,
                  spec(D_MODEL), spec(D_MODEL),
                  pl.BlockSpec((batch, t, p.shape[-1]), lambda i: (0, i, 0))]
                 + [_const_spec(w.shape) for w in weights],
        out_specs=pl.BlockSpec((batch, t, D_MODEL), lambda i: (0, i, 0)),
        out_shape=jax.ShapeDtypeStruct((batch, seq, D_MODEL), F32),
        compiler_params=_params("parallel"),
        name="tail",
    )(h, y_attn, s_in, y_scan, g_attn, g_ssm, p, *weights)


def kernel(x, p, ffn1_norm, ffn1_w_gate, ffn1_w_up, ffn1_w_down, mix_norm, w_in, na_rpb, ssm_lam_re, ssm_lam_im, ssm_log_dt, ssm_b_re, ssm_b_im, ssm_c_re, ssm_c_im, ssm_d, ssm_glu_w, ssm_glu_b, w_attn_out, w_ssm_out, w_out, ffn2_norm, ffn2_w_gate, ffn2_w_up, ffn2_w_down, ple_norm, ple_w_gate, ple_w_proj, final_norm):
    batch, seq, _ = x.shape
    assert p.shape[0] == 1, "single-layer block"
    i = 0
    row = lambda v: v.reshape(1, -1)
    bf = lambda w: w.astype(BF16)
    h = _ffn(x, row(ffn1_norm[i]), bf(ffn1_w_gate[i]), bf(ffn1_w_up[i]), bf(ffn1_w_down[i]),
             batch, seq)
    q, k, v, s_in, g_attn, g_ssm = _inproj(h, row(mix_norm[i]), bf(w_in[i]), batch, seq)
    y_attn = _attention(q, k, v, *_attn_bias_table(na_rpb[i]), batch, seq)
    scan_weights = _ssm_weights(ssm_lam_re[i], ssm_lam_im[i], ssm_log_dt[i], ssm_b_re[i], ssm_b_im[i],
                                ssm_c_re[i], ssm_c_im[i])
    y_scan = _ssm(s_in, None, scan_weights, batch, seq, reverse=False)
    y_scan = _ssm(s_in, y_scan, scan_weights, batch, seq, reverse=True)
    weights = [row(ssm_d[i]), bf(ssm_glu_w[i]), row(ssm_glu_b[i]),
               bf(w_attn_out[i]), bf(w_ssm_out[i]), bf(w_out[i]),
               row(ffn2_norm[i]), bf(ffn2_w_gate[i]), bf(ffn2_w_up[i]), bf(ffn2_w_down[i]),
               row(ple_norm[i]), bf(ple_w_gate[i]), bf(ple_w_proj[i]), row(final_norm)]
    return _tail(h, y_attn, s_in, y_scan, g_attn, g_ssm, p[i], weights, batch, seq)
```

```python
import functools
import math

import jax
import jax.numpy as jnp
from jax import lax
from jax.experimental import pallas as pl
from jax.experimental.pallas import tpu as pltpu

D_MODEL = 1024
GRID_W = 64
N_HEADS = 8
HEAD_DIM = 64
ATTN_W = N_HEADS * HEAD_DIM
NA_WIN_H = 8
NA_WIN_W = 16
SSM_W = 512
SSM_GROUP = 16
SSM_GROUPS = SSM_W // SSM_GROUP
SSM_STATE = 64
LAMBDA_RE_MAX = -1e-4
D_FF = 2816
RMS_EPS = 1e-6

LANES = 128
VMEM_LIMIT = 56 * 1024 * 1024
MASK_BIAS = -1e30

TOKEN_TILE = 1024
INPROJ_TILE = 256
TIME_TILE = 128
ATTN_ROWS = NA_WIN_H // 2
ATTN_KEY_ROWS = ATTN_ROWS + NA_WIN_H
FF_CHUNKS = ((0, 1536), (1536, 1280))
SSM_CHUNK = 16
SSM_CHUNKS = LANES
NQ = SSM_W // LANES

BF16 = jnp.bfloat16
F32 = jnp.float32


def _dot(a, b):
    return jnp.dot(a, b, preferred_element_type=F32)


def _rms(x, g):
    ms = jnp.mean(x * x, axis=-1, keepdims=True)
    return x * lax.rsqrt(ms + RMS_EPS) * g


def _sigmoid(x):
    return 1.0 / (1.0 + jnp.exp(-x))


def _const_spec(shape):
    nd = len(shape)
    return pl.BlockSpec(shape, lambda *_: (0,) * nd, pipeline_mode=pl.Buffered(1))


def _params(*sem):
    return pltpu.CompilerParams(dimension_semantics=sem, vmem_limit_bytes=VMEM_LIMIT)


def _ffn_half_step(x, g_ref, wg_ref, wu_ref, wd_ref):
    xn = _rms(x, g_ref[...]).astype(BF16)
    acc = None
    for start, size in FF_CHUNKS:
        gate = _dot(xn, wg_ref[:, start:start + size])
        up = _dot(xn, wu_ref[:, start:start + size])
        act = (gate * _sigmoid(gate) * up).astype(BF16)
        part = _dot(act, wd_ref[start:start + size, :])
        acc = part if acc is None else acc + part
    return x + 0.5 * acc


def _ffn_kernel(x_ref, g_ref, wg_ref, wu_ref, wd_ref, o_ref):
    o_ref[...] = _ffn_half_step(x_ref[...], g_ref, wg_ref, wu_ref, wd_ref)


def _ffn(x, norm_g, wg, wu, wd, batch, seq):
    t = TOKEN_TILE
    return pl.pallas_call(
        _ffn_kernel,
        grid=(batch, seq // t),
        in_specs=[pl.BlockSpec((None, t, D_MODEL), lambda b, i: (b, i, 0)),
                  _const_spec((1, D_MODEL)), _const_spec(wg.shape),
                  _const_spec(wu.shape), _const_spec(wd.shape)],
        out_specs=pl.BlockSpec((t, D_MODEL), lambda b, i: (i, b)),
        out_shape=jax.ShapeDtypeStruct((seq, batch * D_MODEL), F32),
        compiler_params=_params("parallel", "parallel"),
        name="ffn",
    )(x, norm_g, wg, wu, wd)


def _stack_batches(ref, batch, width):
    return jnp.concatenate([ref[:, b * width:(b + 1) * width] for b in range(batch)], axis=0)


def _unstack_batches(ref, x, batch, width):
    t = x.shape[0] // batch
    for b in range(batch):
        ref[:, b * width:(b + 1) * width] = x[b * t:(b + 1) * t]


def _inproj_kernel(h_ref, g_ref, w_ref, q_ref, k_ref, v_ref, s_ref, ga_ref, gs_ref, *, batch):
    t = INPROJ_TILE
    u = _rms(_stack_batches(h_ref, batch, D_MODEL), g_ref[...]).astype(BF16)
    a = ATTN_W
    _unstack_batches(q_ref, (_dot(u, w_ref[:, 0:a]) * (HEAD_DIM ** -0.5)).astype(BF16), batch, a)
    _unstack_batches(k_ref, _dot(u, w_ref[:, a:2 * a]).astype(BF16), batch, a)
    _unstack_batches(v_ref, _dot(u, w_ref[:, 2 * a:3 * a]).astype(BF16), batch, a)
    s0 = 3 * a
    s_in = _dot(u, w_ref[:, s0:s0 + SSM_W])
    for c in range(NQ):
        for b in range(batch):
            s_ref[c, :, b * LANES:(b + 1) * LANES] = s_in[b * t:(b + 1) * t, c * LANES:(c + 1) * LANES]
    g0 = s0 + SSM_W
    _unstack_batches(ga_ref, _sigmoid(_dot(u, w_ref[:, g0:g0 + D_MODEL])).astype(BF16),
                     batch, D_MODEL)
    _unstack_batches(gs_ref, _sigmoid(_dot(u, w_ref[:, g0 + D_MODEL:g0 + 2 * D_MODEL])).astype(BF16),
                     batch, D_MODEL)


def _inproj(h, norm_g, w_in, batch, seq):
    t = INPROJ_TILE
    def spec(c):
        return pl.BlockSpec((t, batch * c), lambda i: (i, 0))
    def shape(c, dt):
        return jax.ShapeDtypeStruct((seq, batch * c), dt)
    return pl.pallas_call(
        functools.partial(_inproj_kernel, batch=batch),
        grid=(seq // t,),
        in_specs=[spec(D_MODEL), _const_spec((1, D_MODEL)), _const_spec(w_in.shape)],
        out_specs=[spec(ATTN_W), spec(ATTN_W), spec(ATTN_W),
                   pl.BlockSpec((NQ, t, batch * LANES), lambda i: (0, i, 0)),
                   spec(D_MODEL), spec(D_MODEL)],
        out_shape=[shape(ATTN_W, BF16), shape(ATTN_W, BF16), shape(ATTN_W, BF16),
                   jax.ShapeDtypeStruct((NQ, seq, batch * LANES), F32),
                   shape(D_MODEL, BF16), shape(D_MODEL, BF16)],
        compiler_params=_params("parallel"),
        name="inproj",
    )(h, norm_g, w_in)


def _attn_kernel(q_ref, k_ref, v_ref, bias_ref, bias_odd_ref, o_ref, *, rows):
    j = pl.program_id(1)
    n_blocks = rows // ATTN_ROWS
    stack = 2 * GRID_W
    lane = lax.broadcasted_iota(jnp.int32, (GRID_W, LANES), 1)
    first = lane < HEAD_DIM

    def stacked_queries(cols):
        pieces = []
        for i in range(ATTN_ROWS):
            q2 = q_ref[i * GRID_W:(i + 1) * GRID_W, cols]
            zero = jnp.zeros_like(q2)
            pieces += [jnp.where(first, q2, zero), jnp.where(first, zero, q2)]
        return jnp.concatenate(pieces, axis=0)

    def scores(lhs, keys):
        return lax.dot_general(lhs, keys, (((1,), (1,)), ((), ())), preferred_element_type=F32)

    def softmax_terms(s):
        p = jnp.exp(s - jnp.max(s, axis=-1, keepdims=True))
        return p.astype(BF16), 1.0 / jnp.sum(p, axis=-1, keepdims=True)

    def write(o, inv, cols):
        o = o * inv
        for i in range(ATTN_ROWS):
            blk = o[i * stack:(i + 1) * stack]
            o_ref[i * GRID_W:(i + 1) * GRID_W, cols] = jnp.where(
                first, blk[0:GRID_W], blk[GRID_W:stack]).astype(BF16)

    is_edge = jnp.logical_or(j == 0, j == n_blocks - 1)

    @pl.when(is_edge)
    def _():
        r0 = j * ATTN_ROWS
        rs = jnp.clip(r0 - NA_WIN_H // 2, 0, rows - NA_WIN_H)
        start = pl.multiple_of(rs * GRID_W, GRID_W)
        nkeys = NA_WIN_H * GRID_W
        for pair in range(N_HEADS // 2):
            cols = slice(pair * LANES, (pair + 1) * LANES)
            hrows = slice(pair * stack, (pair + 1) * stack)
            s = scores(stacked_queries(cols), k_ref[pl.ds(start, nkeys), cols])
            ps, invs = [], []
            for i in range(ATTN_ROWS):
                p, inv = softmax_terms(s[i * stack:(i + 1) * stack] + bias_ref[r0 + i - rs, hrows, :])
                ps.append(p)
                invs.append(inv)
            o = _dot(jnp.concatenate(ps, axis=0), v_ref[pl.ds(start, nkeys), cols])
            write(o, jnp.concatenate(invs, axis=0), cols)

    @pl.when(jnp.logical_not(is_edge))
    def _():
        start = pl.multiple_of((j * ATTN_ROWS - NA_WIN_H // 2) * GRID_W, ATTN_ROWS * GRID_W)
        nkeys = ATTN_KEY_ROWS * GRID_W
        for pair in range(N_HEADS // 2):
            cols = slice(pair * LANES, (pair + 1) * LANES)
            hrows = slice(pair * stack, (pair + 1) * stack)
            s = scores(stacked_queries(cols), k_ref[pl.ds(start, nkeys), cols])
            ps, invs = [], []
            for i in range(ATTN_ROWS):
                lo = i // 2 * 2
                hi = lo + NA_WIN_H + 2 * (i % 2)
                bias = bias_odd_ref[hrows, :] if i % 2 else bias_ref[NA_WIN_H // 2, hrows, :]
                p, inv = softmax_terms(s[i * stack:(i + 1) * stack, lo * GRID_W:hi * GRID_W] + bias)
                parts = [p]
                if lo:
                    parts.insert(0, jnp.zeros((stack, lo * GRID_W), BF16))
                if hi < ATTN_KEY_ROWS:
                    parts.append(jnp.zeros((stack, (ATTN_KEY_ROWS - hi) * GRID_W), BF16))
                ps.append(jnp.concatenate(parts, axis=1) if len(parts) > 1 else p)
                invs.append(inv)
            o = _dot(jnp.concatenate(ps, axis=0), v_ref[pl.ds(start, nkeys), cols])
            write(o, jnp.concatenate(invs, axis=0), cols)


def _attention(q, k, v, bias_tab, bias_odd, batch, seq):
    rows = seq // GRID_W
    qt = ATTN_ROWS * GRID_W
    kv_spec = pl.BlockSpec((seq, ATTN_W), lambda b, r: (0, b))
    return pl.pallas_call(
        functools.partial(_attn_kernel, rows=rows),
        grid=(batch, rows // ATTN_ROWS),
        in_specs=[pl.BlockSpec((qt, ATTN_W), lambda b, r: (r, b)), kv_spec, kv_spec,
                  _const_spec(bias_tab.shape), _const_spec(bias_odd.shape)],
        out_specs=pl.BlockSpec((qt, ATTN_W), lambda b, r: (r, b)),
        out_shape=jax.ShapeDtypeStruct((seq, batch * ATTN_W), BF16),
        compiler_params=_params("parallel", "arbitrary"),
        name="natten",
    )(q, k, v, bias_tab, bias_odd)


def _attn_bias_table(rpb):
    c = jnp.arange(GRID_W)[:, None]
    kc = jnp.arange(GRID_W)[None, :]
    cs = jnp.clip(c - NA_WIN_W // 2, 0, GRID_W - NA_WIN_W)
    inside = (kc >= cs) & (kc < cs + NA_WIN_W)
    dc = kc - c + (NA_WIN_W - 1)
    onehot = (dc[:, :, None] == jnp.arange(2 * NA_WIN_W - 1)[None, None, :]).astype(F32)
    band = jnp.einsum('hyx,ckx->hcyk', rpb.astype(F32), onehot, precision=lax.Precision.HIGHEST)
    band = jnp.where(inside[None, :, None, :], band, MASK_BIAS)
    tabs = [band[:, :, NA_WIN_H - 1 - var:2 * NA_WIN_H - 1 - var, :] for var in range(NA_WIN_H)]
    tab = jnp.stack(tabs, axis=0)
    tab = tab.reshape(NA_WIN_H, N_HEADS * GRID_W, NA_WIN_H * GRID_W)
    pad = jnp.full((N_HEADS * GRID_W, GRID_W), MASK_BIAS, F32)
    tab_odd = jnp.concatenate([pad, tab[NA_WIN_H // 2], pad], axis=1)
    return tab, tab_odd


def _ssm_kernel(s_ref, km_ref, ss_ref, wx_ref, mur_ref, mui_ref, y_ref,
                zt_ref, sw_ref, ytq_ref, *, n_tiles):
    phase = pl.program_id(1)
    tile = pl.program_id(2)
    step_rows, n_chunks, gc, p = SSM_CHUNK, SSM_CHUNKS, SSM_GROUP, SSM_STATE
    n_pairs = SSM_GROUPS // 2
    slab_groups = LANES // gc
    tile_rows = n_chunks * n_pairs

    def pair_rows(gp):
        return pl.ds(pl.multiple_of(tile * tile_rows, tile_rows) + gp, n_chunks, stride=n_pairs)

    @pl.when(phase == 0)
    def _():
        for q in range(NQ):
            for tau in range(step_rows):
                at = s_ref[q, pl.ds(tau, n_chunks, stride=step_rows), :].T.astype(BF16)
                for g8 in range(slab_groups):
                    zt_ref[tile, slab_groups * q + g8, tau * gc:(tau + 1) * gc, :] = (
                        at[g8 * gc:(g8 + 1) * gc, :])
        for d in range(2):
            for gp in range(n_pairs):
                r0 = _dot(ss_ref[d, 2 * gp], zt_ref[tile, 2 * gp])
                r1 = _dot(ss_ref[d, 2 * gp + 1], zt_ref[tile, 2 * gp + 1])
                sw_ref[d, 0, pair_rows(gp), :] = jnp.concatenate([r0[0:p], r1[0:p]], axis=0).T
                sw_ref[d, 1, pair_rows(gp), :] = jnp.concatenate(
                    [r0[p:2 * p], r1[p:2 * p]], axis=0).T

    @pl.when(jnp.logical_and(phase == 0, tile == n_tiles - 1))
    def _():
        total = n_tiles * n_chunks
        mu = [(mur_ref[d], mui_ref[d]) for d in range(2)]

        def step(j, carry):
            new = []
            for d in range(2):
                e_r, e_i = carry[d]
                chunk = j if d == 0 else total - 1 - j
                rows = pl.ds(pl.multiple_of(chunk * n_pairs, n_pairs), n_pairs)
                s_r = sw_ref[d, 0, rows, :]
                s_i = sw_ref[d, 1, rows, :]
                sw_ref[d, 0, rows, :] = e_r
                sw_ref[d, 1, rows, :] = e_i
                m_r, m_i = mu[d]
                new.append((m_r * e_r - m_i * e_i + s_r, m_r * e_i + m_i * e_r + s_i))
            return tuple(new)

        zero = jnp.zeros((n_pairs, LANES), F32)
        lax.fori_loop(0, total, step, ((zero, zero), (zero, zero)), unroll=2)

    @pl.when(phase == 1)
    def _():
        for q in range(NQ):
            for g8 in range(0, slab_groups, 2):
                gp = (slab_groups * q + g8) // 2
                x_t = [[sw_ref[d, c, pair_rows(gp), :].T for c in range(2)] for d in range(2)]
                for h in range(2):
                    g = 2 * gp + h
                    states = jnp.concatenate(
                        [x_t[d][c][h * p:(h + 1) * p] for d in range(2) for c in range(2)],
                        axis=0).astype(BF16)
                    ytq_ref[g8 + h] = _dot(km_ref[g], zt_ref[tile, g]) + _dot(wx_ref[g], states)
            for tau in range(step_rows):
                piece = jnp.concatenate(
                    [ytq_ref[g8, tau * gc:(tau + 1) * gc, :] for g8 in range(slab_groups)], axis=0)
                y_ref[q, pl.ds(tau, n_chunks, stride=step_rows), :] = piece.T


def _ssm(s_in, weights, batch, seq):
    tile = SSM_CHUNK * SSM_CHUNKS
    n_tiles = seq // tile
    n_pairs = SSM_GROUPS // 2
    wide = SSM_CHUNK * SSM_GROUP
    in_spec = pl.BlockSpec((NQ, tile, LANES),
                           lambda b, ph, k: (0, jnp.where(ph == 0, k, n_tiles - 1), b))
    out_spec = pl.BlockSpec((NQ, tile, LANES), lambda b, ph, k: (0, jnp.where(ph == 0, 0, k), b))
    return pl.pallas_call(
        functools.partial(_ssm_kernel, n_tiles=n_tiles),
        grid=(batch, 2, n_tiles),
        in_specs=[in_spec] + [_const_spec(w.shape) for w in weights],
        out_specs=out_spec,
        out_shape=jax.ShapeDtypeStruct((NQ, seq, batch * LANES), F32),
        scratch_shapes=[pltpu.VMEM((n_tiles, SSM_GROUPS, wide, SSM_CHUNKS), BF16),
                        pltpu.VMEM((2, 2, n_tiles * SSM_CHUNKS * n_pairs, LANES), F32),
                        pltpu.VMEM((LANES // SSM_GROUP, wide, SSM_CHUNKS), F32)],
        compiler_params=_params("parallel", "arbitrary", "arbitrary"),
        name="s5_scan",
    )(s_in, *weights)


def _ssm_weights(lam_re, lam_im, log_dt, b_re, b_im, c_re, c_im):
    n, g, p, gc = SSM_CHUNK, SSM_GROUPS, SSM_STATE, SSM_GROUP
    wide = n * gc
    lam = lax.complex(jnp.minimum(lam_re, LAMBDA_RE_MAX), lam_im)
    z = lam * jnp.exp(log_dt)[..., None]
    b_bar = ((jnp.exp(z) - 1.0) / lam)[..., None] * lax.complex(b_re, b_im)
    c = lax.complex(c_re, c_im)
    steps = jnp.arange(n, dtype=F32)
    fwd = jnp.array([1.0, 0.0], F32)[:, None]
    cp = c[:, :, :, None, :] * jnp.exp(z[:, :, None, None, :] * steps[:, None])
    hi = lax.Precision.HIGHEST
    kk = (jnp.einsum('dgokp,dgpi->dgoki', cp.real, b_bar.real, precision=hi)
          - jnp.einsum('dgokp,dgpi->dgoki', cp.imag, b_bar.imag, precision=hi))
    kk = kk.reshape(2, g * gc, wide)
    src = jnp.arange(wide)
    t_out = jnp.arange(n)[None, :, None, None]
    lag = (t_out - src[None, None, None, :] // gc) * jnp.array([1, -1])[:, None, None, None]
    place = jnp.logical_and(src[None, None, :, None] // gc == lag,
                            src[None, None, :, None] % gc == src[None, None, None, :] % gc)
    kmat = jnp.einsum('drq,daqc->arc', kk, place.astype(F32), precision=hi)
    kmat = kmat.astype(BF16).reshape(n, g, gc, wide).transpose(1, 0, 2, 3).reshape(g, wide, wide)
    e_in = fwd * (n - 1 - steps) + (1.0 - fwd) * steps
    sb = (jnp.exp(z[..., None] * e_in[:, None, None, :])[..., None]
          * b_bar[:, :, :, None, :])
    ssum = jnp.concatenate([sb.real, sb.imag], axis=2).astype(BF16).reshape(2, g, 2 * p, wide)
    e_out = fwd * (steps + 1.0) + (1.0 - fwd) * (n - steps)
    cw = (jnp.exp(z[:, :, None, None, :] * e_out[:, None, :, None, None])
          * c[:, :, None, :, :])
    wx = jnp.concatenate([cw[0].real, -cw[0].imag, cw[1].real, -cw[1].imag], axis=-1)
    wx = wx.astype(BF16).reshape(g, wide, 4 * p)
    mu = jnp.exp(z * n).reshape(2, g // 2, 2 * p)
    return kmat, ssum, wx, mu.real, mu.imag


def _gelu_tanh(x):
    return 0.5 * x * (1.0 + jnp.tanh(math.sqrt(2.0 / math.pi) * (x + 0.044715 * (x * x * x))))


def _tail_kernel(h_ref, ya_ref, s_ref, ys_ref, ga_ref, gs_ref, p_ref,
                 d_ref, gw_ref, gb_ref, wa_ref, ws_ref, wo_ref,
                 fn_ref, wg_ref, wu_ref, wd_ref,
                 pn_ref, pg_ref, pp_ref, on_ref, o_ref, *, batch):
    t = TIME_TILE
    y_attn = _dot(_stack_batches(ya_ref, batch, ATTN_W), wa_ref[...])

    def slabs(ref):
        return jnp.concatenate(
            [jnp.concatenate([ref[c, :, b * LANES:(b + 1) * LANES] for c in range(NQ)], axis=1)
             for b in range(batch)], axis=0)

    y = _gelu_tanh(d_ref[...] * slabs(s_ref) + slabs(ys_ref))
    y = y * _sigmoid(_dot(y.astype(BF16), gw_ref[...]) + gb_ref[...])
    y_ssm = _dot(y.astype(BF16), ws_ref[...])
    mix = (_stack_batches(ga_ref, batch, D_MODEL).astype(F32) * y_attn
           + _stack_batches(gs_ref, batch, D_MODEL).astype(F32) * y_ssm)
    h = _stack_batches(h_ref, batch, D_MODEL) + _dot(mix.astype(BF16), wo_ref[...])
    h = _ffn_half_step(h, fn_ref, wg_ref, wu_ref, wd_ref)
    gate = _sigmoid(_dot(_rms(h, pn_ref[...]).astype(BF16), pg_ref[...]))
    pe = jnp.concatenate([p_ref[b] for b in range(batch)], axis=0).astype(BF16)
    h = h + _dot(pe, pp_ref[...]) * gate
    out = _rms(h, on_ref[...])
    for b in range(batch):
        o_ref[b] = out[b * t:(b + 1) * t]


def _tail(h, y_attn, s_in, y_scan, g_attn, g_ssm, p, weights, batch, seq):
    t = TIME_TILE
    def spec(c):
        return pl.BlockSpec((t, batch * c), lambda i: (i, 0))
    slab_spec = pl.BlockSpec((NQ, t, batch * LANES), lambda i: (0, i, 0))
    return pl.pallas_call(
        functools.partial(_tail_kernel, batch=batch),
        grid=(seq // t,),
        in_specs=[spec(D_MODEL), spec(ATTN_W), slab_spec, slab_spec,
                  spec(D_MODEL), spec(D_MODEL),
                  pl.BlockSpec((batch, t, p.shape[-1]), lambda i: (0, i, 0))]
                 + [_const_spec(w.shape) for w in weights],
        out_specs=pl.BlockSpec((batch, t, D_MODEL), lambda i: (0, i, 0)),
        out_shape=jax.ShapeDtypeStruct((batch, seq, D_MODEL), F32),
        compiler_params=_params("parallel"),
        name="tail",
    )(h, y_attn, s_in, y_scan, g_attn, g_ssm, p, *weights)


def kernel(x, p, ffn1_norm, ffn1_w_gate, ffn1_w_up, ffn1_w_down, mix_norm, w_in, na_rpb, ssm_lam_re, ssm_lam_im, ssm_log_dt, ssm_b_re, ssm_b_im, ssm_c_re, ssm_c_im, ssm_d, ssm_glu_w, ssm_glu_b, w_attn_out, w_ssm_out, w_out, ffn2_norm, ffn2_w_gate, ffn2_w_up, ffn2_w_down, ple_norm, ple_w_gate, ple_w_proj, final_norm):
    batch, seq, _ = x.shape
    assert p.shape[0] == 1, "single-layer block"
    i = 0
    row = lambda v: v.reshape(1, -1)
    bf = lambda w: w.astype(BF16)
    h = _ffn(x, row(ffn1_norm[i]), bf(ffn1_w_gate[i]), bf(ffn1_w_up[i]), bf(ffn1_w_down[i]),
             batch, seq)
    q, k, v, s_in, g_attn, g_ssm = _inproj(h, row(mix_norm[i]), bf(w_in[i]), batch, seq)
    y_attn = _attention(q, k, v, *_attn_bias_table(na_rpb[i]), batch, seq)
    scan_weights = _ssm_weights(ssm_lam_re[i], ssm_lam_im[i], ssm_log_dt[i], ssm_b_re[i], ssm_b_im[i],
                                ssm_c_re[i], ssm_c_im[i])
    y_scan = _ssm(s_in, scan_weights, batch, seq)
    weights = [row(ssm_d[i]), bf(ssm_glu_w[i]), row(ssm_glu_b[i]),
               bf(w_attn_out[i]), bf(w_ssm_out[i]), bf(w_out[i]),
               row(ffn2_norm[i]), bf(ffn2_w_gate[i]), bf(ffn2_w_up[i]), bf(ffn2_w_down[i]),
               row(ple_norm[i]), bf(ple_w_gate[i]), bf(ple_w_proj[i]), row(final_norm)]
    return _tail(h, y_attn, s_in, y_scan, g_attn, g_ssm, p[i], weights, batch, seq)
```

```python
import functools
import math

import jax
import jax.numpy as jnp
from jax import lax
from jax.experimental import pallas as pl
from jax.experimental.pallas import tpu as pltpu

D_MODEL = 1024
GRID_W = 64
N_HEADS = 8
HEAD_DIM = 64
ATTN_W = N_HEADS * HEAD_DIM
NA_WIN_H = 8
NA_WIN_W = 16
SSM_W = 512
SSM_GROUP = 16
SSM_GROUPS = SSM_W // SSM_GROUP
SSM_STATE = 64
LAMBDA_RE_MAX = -1e-4
D_FF = 2816
RMS_EPS = 1e-6

LANES = 128
VMEM_LIMIT = 56 * 1024 * 1024
MASK_BIAS = -1e30

TOKEN_TILE = 1024
INPROJ_TILE = 256
TIME_TILE = 128
ATTN_ROWS = NA_WIN_H // 2
ATTN_KEY_ROWS = ATTN_ROWS + NA_WIN_H
FF_CHUNKS = ((0, 1536), (1536, D_FF - 1536))
SSM_CHUNK = 16
SSM_CHUNKS = LANES
NQ = SSM_W // LANES

BF16 = jnp.bfloat16
F32 = jnp.float32


def _dot(a, b):
    return jnp.dot(a, b, preferred_element_type=F32)


def _rms(x, g):
    ms = jnp.mean(x * x, axis=-1, keepdims=True)
    return x * lax.rsqrt(ms + RMS_EPS) * g


def _sigmoid(x):
    return 1.0 / (1.0 + jnp.exp(-x))


def _const_spec(shape):
    nd = len(shape)
    return pl.BlockSpec(shape, lambda *_: (0,) * nd, pipeline_mode=pl.Buffered(1))


def _params(*sem):
    return pltpu.CompilerParams(dimension_semantics=sem, vmem_limit_bytes=VMEM_LIMIT)


def _ffn_half_step(x, g_ref, wg_ref, wu_ref, wd_ref):
    xn = _rms(x, g_ref[...]).astype(BF16)
    acc = None
    for start, size in FF_CHUNKS:
        gate = _dot(xn, wg_ref[:, start:start + size])
        up = _dot(xn, wu_ref[:, start:start + size])
        act = (gate * _sigmoid(gate) * up).astype(BF16)
        part = _dot(act, wd_ref[start:start + size, :])
        acc = part if acc is None else acc + part
    return x + 0.5 * acc


def _ffn_kernel(x_ref, g_ref, wg_ref, wu_ref, wd_ref, o_ref):
    o_ref[...] = _ffn_half_step(x_ref[...], g_ref, wg_ref, wu_ref, wd_ref)


def _ffn(x, norm_g, wg, wu, wd, batch, seq):
    t = TOKEN_TILE
    return pl.pallas_call(
        _ffn_kernel,
        grid=(batch, seq // t),
        in_specs=[pl.BlockSpec((None, t, D_MODEL), lambda b, i: (b, i, 0)),
                  _const_spec((1, D_MODEL)), _const_spec(wg.shape),
                  _const_spec(wu.shape), _const_spec(wd.shape)],
        out_specs=pl.BlockSpec((t, D_MODEL), lambda b, i: (i, b)),
        out_shape=jax.ShapeDtypeStruct((seq, batch * D_MODEL), F32),
        compiler_params=_params("parallel", "parallel"),
        name="ffn",
    )(x, norm_g, wg, wu, wd)


def _stack_batches(ref, batch, width):
    return jnp.concatenate([ref[:, b * width:(b + 1) * width] for b in range(batch)], axis=0)


def _unstack_batches(ref, x, batch, width):
    t = x.shape[0] // batch
    for b in range(batch):
        ref[:, b * width:(b + 1) * width] = x[b * t:(b + 1) * t]


def _inproj_kernel(h_ref, g_ref, w_ref, q_ref, k_ref, v_ref, s_ref, ga_ref, gs_ref, *, batch):
    t = INPROJ_TILE
    u = _rms(_stack_batches(h_ref, batch, D_MODEL), g_ref[...]).astype(BF16)
    a = ATTN_W
    _unstack_batches(q_ref, (_dot(u, w_ref[:, 0:a]) * (HEAD_DIM ** -0.5)).astype(BF16), batch, a)
    _unstack_batches(k_ref, _dot(u, w_ref[:, a:2 * a]).astype(BF16), batch, a)
    _unstack_batches(v_ref, _dot(u, w_ref[:, 2 * a:3 * a]).astype(BF16), batch, a)
    s0 = 3 * a
    s_in = _dot(u, w_ref[:, s0:s0 + SSM_W])
    for c in range(NQ):
        for b in range(batch):
            s_ref[c, :, b * LANES:(b + 1) * LANES] = s_in[b * t:(b + 1) * t, c * LANES:(c + 1) * LANES]
    g0 = s0 + SSM_W
    _unstack_batches(ga_ref, _sigmoid(_dot(u, w_ref[:, g0:g0 + D_MODEL])).astype(BF16),
                     batch, D_MODEL)
    _unstack_batches(gs_ref, _sigmoid(_dot(u, w_ref[:, g0 + D_MODEL:g0 + 2 * D_MODEL])).astype(BF16),
                     batch, D_MODEL)


def _inproj(h, norm_g, w_in, batch, seq):
    t = INPROJ_TILE
    def spec(c):
        return pl.BlockSpec((t, batch * c), lambda i: (i, 0))
    def shape(c, dt):
        return jax.ShapeDtypeStruct((seq, batch * c), dt)
    return pl.pallas_call(
        functools.partial(_inproj_kernel, batch=batch),
        grid=(seq // t,),
        in_specs=[spec(D_MODEL), _const_spec((1, D_MODEL)), _const_spec(w_in.shape)],
        out_specs=[spec(ATTN_W), spec(ATTN_W), spec(ATTN_W),
                   pl.BlockSpec((NQ, t, batch * LANES), lambda i: (0, i, 0)),
                   spec(D_MODEL), spec(D_MODEL)],
        out_shape=[shape(ATTN_W, BF16), shape(ATTN_W, BF16), shape(ATTN_W, BF16),
                   jax.ShapeDtypeStruct((NQ, seq, batch * LANES), F32),
                   shape(D_MODEL, BF16), shape(D_MODEL, BF16)],
        compiler_params=_params("parallel"),
        name="inproj",
    )(h, norm_g, w_in)


def _attn_kernel(q_ref, k_ref, v_ref, bias_ref, bias_odd_ref, o_ref, *, rows):
    j = pl.program_id(1)
    n_blocks = rows // ATTN_ROWS
    stack = 2 * GRID_W
    lane = lax.broadcasted_iota(jnp.int32, (GRID_W, LANES), 1)
    first = lane < HEAD_DIM

    def stacked_queries(cols):
        pieces = []
        for i in range(ATTN_ROWS):
            q2 = q_ref[i * GRID_W:(i + 1) * GRID_W, cols]
            zero = jnp.zeros_like(q2)
            pieces += [jnp.where(first, q2, zero), jnp.where(first, zero, q2)]
        return jnp.concatenate(pieces, axis=0)

    def scores(lhs, keys):
        return lax.dot_general(lhs, keys, (((1,), (1,)), ((), ())), preferred_element_type=F32)

    def softmax_terms(s):
        p = jnp.exp(s - jnp.max(s, axis=-1, keepdims=True))
        return p.astype(BF16), 1.0 / jnp.sum(p, axis=-1, keepdims=True)

    def write(o, inv, cols):
        o = o * inv
        for i in range(ATTN_ROWS):
            blk = o[i * stack:(i + 1) * stack]
            o_ref[i * GRID_W:(i + 1) * GRID_W, cols] = jnp.where(
                first, blk[0:GRID_W], blk[GRID_W:stack]).astype(BF16)

    is_edge = jnp.logical_or(j == 0, j == n_blocks - 1)

    @pl.when(is_edge)
    def _():
        r0 = j * ATTN_ROWS
        rs = jnp.clip(r0 - NA_WIN_H // 2, 0, rows - NA_WIN_H)
        start = pl.multiple_of(rs * GRID_W, GRID_W)
        nkeys = NA_WIN_H * GRID_W
        for pair in range(N_HEADS // 2):
            cols = slice(pair * LANES, (pair + 1) * LANES)
            hrows = slice(pair * stack, (pair + 1) * stack)
            s = scores(stacked_queries(cols), k_ref[pl.ds(start, nkeys), cols])
            ps, invs = [], []
            for i in range(ATTN_ROWS):
                p, inv = softmax_terms(s[i * stack:(i + 1) * stack] + bias_ref[r0 + i - rs, hrows, :])
                ps.append(p)
                invs.append(inv)
            o = _dot(jnp.concatenate(ps, axis=0), v_ref[pl.ds(start, nkeys), cols])
            write(o, jnp.concatenate(invs, axis=0), cols)

    @pl.when(jnp.logical_not(is_edge))
    def _():
        start = pl.multiple_of((j * ATTN_ROWS - NA_WIN_H // 2) * GRID_W, ATTN_ROWS * GRID_W)
        nkeys = ATTN_KEY_ROWS * GRID_W
        for pair in range(N_HEADS // 2):
            cols = slice(pair * LANES, (pair + 1) * LANES)
            hrows = slice(pair * stack, (pair + 1) * stack)
            s = scores(stacked_queries(cols), k_ref[pl.ds(start, nkeys), cols])
            ps, invs = [], []
            for i in range(ATTN_ROWS):
                lo = i // 2 * 2
                hi = lo + NA_WIN_H + 2 * (i % 2)
                bias = bias_odd_ref[hrows, :] if i % 2 else bias_ref[NA_WIN_H // 2, hrows, :]
                p, inv = softmax_terms(s[i * stack:(i + 1) * stack, lo * GRID_W:hi * GRID_W] + bias)
                parts = [p]
                if lo:
                    parts.insert(0, jnp.zeros((stack, lo * GRID_W), BF16))
                if hi < ATTN_KEY_ROWS:
                    parts.append(jnp.zeros((stack, (ATTN_KEY_ROWS - hi) * GRID_W), BF16))
                ps.append(jnp.concatenate(parts, axis=1) if len(parts) > 1 else p)
                invs.append(inv)
            o = _dot(jnp.concatenate(ps, axis=0), v_ref[pl.ds(start, nkeys), cols])
            write(o, jnp.concatenate(invs, axis=0), cols)


def _attention(q, k, v, bias_tab, bias_odd, batch, seq):
    rows = seq // GRID_W
    qt = ATTN_ROWS * GRID_W
    kv_spec = pl.BlockSpec((seq, ATTN_W), lambda b, r: (0, b))
    return pl.pallas_call(
        functools.partial(_attn_kernel, rows=rows),
        grid=(batch, rows // ATTN_ROWS),
        in_specs=[pl.BlockSpec((qt, ATTN_W), lambda b, r: (r, b)), kv_spec, kv_spec,
                  _const_spec(bias_tab.shape), _const_spec(bias_odd.shape)],
        out_specs=pl.BlockSpec((qt, ATTN_W), lambda b, r: (r, b)),
        out_shape=jax.ShapeDtypeStruct((seq, batch * ATTN_W), BF16),
        compiler_params=_params("parallel", "arbitrary"),
        name="natten",
    )(q, k, v, bias_tab, bias_odd)


def _attn_bias_table(rpb):
    c = jnp.arange(GRID_W)[:, None]
    kc = jnp.arange(GRID_W)[None, :]
    cs = jnp.clip(c - NA_WIN_W // 2, 0, GRID_W - NA_WIN_W)
    inside = (kc >= cs) & (kc < cs + NA_WIN_W)
    dc = kc - c + (NA_WIN_W - 1)
    onehot = (dc[:, :, None] == jnp.arange(2 * NA_WIN_W - 1)[None, None, :]).astype(F32)
    band = jnp.einsum('hyx,ckx->hcyk', rpb.astype(F32), onehot, precision=lax.Precision.HIGHEST)
    band = jnp.where(inside[None, :, None, :], band, MASK_BIAS)
    tabs = [band[:, :, NA_WIN_H - 1 - var:2 * NA_WIN_H - 1 - var, :] for var in range(NA_WIN_H)]
    tab = jnp.stack(tabs, axis=0)
    tab = tab.reshape(NA_WIN_H, N_HEADS * GRID_W, NA_WIN_H * GRID_W)
    pad = jnp.full((N_HEADS * GRID_W, GRID_W), MASK_BIAS, F32)
    tab_odd = jnp.concatenate([pad, tab[NA_WIN_H // 2], pad], axis=1)
    return tab, tab_odd


def _ssm_kernel(s_ref, km_ref, ss_ref, wx_ref, mur_ref, mui_ref, y_ref,
                zt_ref, sw_ref, ytq_ref, *, n_tiles):
    phase = pl.program_id(1)
    tile = pl.program_id(2)
    step_rows, n_chunks, gc, p = SSM_CHUNK, SSM_CHUNKS, SSM_GROUP, SSM_STATE
    n_pairs = SSM_GROUPS // 2
    slab_groups = LANES // gc
    tile_rows = n_chunks * n_pairs

    def pair_rows(gp):
        return pl.ds(pl.multiple_of(tile * tile_rows, tile_rows) + gp, n_chunks, stride=n_pairs)

    @pl.when(phase == 0)
    def _():
        for q in range(NQ):
            for tau in range(step_rows):
                at = s_ref[q, pl.ds(tau, n_chunks, stride=step_rows), :].T.astype(BF16)
                for g8 in range(slab_groups):
                    zt_ref[tile, slab_groups * q + g8, tau * gc:(tau + 1) * gc, :] = (
                        at[g8 * gc:(g8 + 1) * gc, :])
        for d in range(2):
            for gp in range(n_pairs):
                r0 = _dot(ss_ref[d, 2 * gp], zt_ref[tile, 2 * gp])
                r1 = _dot(ss_ref[d, 2 * gp + 1], zt_ref[tile, 2 * gp + 1])
                sw_ref[d, 0, pair_rows(gp), :] = jnp.concatenate([r0[0:p], r1[0:p]], axis=0).T
                sw_ref[d, 1, pair_rows(gp), :] = jnp.concatenate(
                    [r0[p:2 * p], r1[p:2 * p]], axis=0).T

    @pl.when(jnp.logical_and(phase == 0, tile == n_tiles - 1))
    def _():
        total = n_tiles * n_chunks
        mu = [(mur_ref[d], mui_ref[d]) for d in range(2)]

        def step(j, carry):
            new = []
            for d in range(2):
                e_r, e_i = carry[d]
                chunk = j if d == 0 else total - 1 - j
                rows = pl.ds(pl.multiple_of(chunk * n_pairs, n_pairs), n_pairs)
                s_r = sw_ref[d, 0, rows, :]
                s_i = sw_ref[d, 1, rows, :]
                sw_ref[d, 0, rows, :] = e_r
                sw_ref[d, 1, rows, :] = e_i
                m_r, m_i = mu[d]
                new.append((m_r * e_r - m_i * e_i + s_r, m_r * e_i + m_i * e_r + s_i))
            return tuple(new)

        zero = jnp.zeros((n_pairs, LANES), F32)
        lax.fori_loop(0, total, step, ((zero, zero), (zero, zero)), unroll=4)

    @pl.when(phase == 1)
    def _():
        for q in range(NQ):
            for g8 in range(0, slab_groups, 2):
                gp = (slab_groups * q + g8) // 2
                x_t = [[sw_ref[d, c, pair_rows(gp), :].T for c in range(2)] for d in range(2)]
                for h in range(2):
                    g = 2 * gp + h
                    states = jnp.concatenate(
                        [x_t[d][c][h * p:(h + 1) * p] for d in range(2) for c in range(2)],
                        axis=0).astype(BF16)
                    ytq_ref[g8 + h] = _dot(km_ref[g], zt_ref[tile, g]) + _dot(wx_ref[g], states)
            for tau in range(step_rows):
                piece = jnp.concatenate(
                    [ytq_ref[g8, tau * gc:(tau + 1) * gc, :] for g8 in range(slab_groups)], axis=0)
                y_ref[q, pl.ds(tau, n_chunks, stride=step_rows), :] = piece.T


def _ssm(s_in, weights, batch, seq):
    tile = SSM_CHUNK * SSM_CHUNKS
    n_tiles = seq // tile
    n_pairs = SSM_GROUPS // 2
    wide = SSM_CHUNK * SSM_GROUP
    in_spec = pl.BlockSpec((NQ, tile, LANES),
                           lambda b, ph, k: (0, jnp.where(ph == 0, k, n_tiles - 1), b))
    out_spec = pl.BlockSpec((NQ, tile, LANES), lambda b, ph, k: (0, jnp.where(ph == 0, 0, k), b))
    return pl.pallas_call(
        functools.partial(_ssm_kernel, n_tiles=n_tiles),
        grid=(batch, 2, n_tiles),
        in_specs=[in_spec] + [_const_spec(w.shape) for w in weights],
        out_specs=out_spec,
        out_shape=jax.ShapeDtypeStruct((NQ, seq, batch * LANES), F32),
        scratch_shapes=[pltpu.VMEM((n_tiles, SSM_GROUPS, wide, SSM_CHUNKS), BF16),
                        pltpu.VMEM((2, 2, n_tiles * SSM_CHUNKS * n_pairs, LANES), F32),
                        pltpu.VMEM((LANES // SSM_GROUP, wide, SSM_CHUNKS), F32)],
        compiler_params=_params("parallel", "arbitrary", "arbitrary"),
        name="s5_scan",
    )(s_in, *weights)


def _ssm_weights(lam_re, lam_im, log_dt, b_re, b_im, c_re, c_im):
    n, g, p, gc = SSM_CHUNK, SSM_GROUPS, SSM_STATE, SSM_GROUP
    wide = n * gc
    lam = lax.complex(jnp.minimum(lam_re, LAMBDA_RE_MAX), lam_im)
    z = lam * jnp.exp(log_dt)[..., None]
    b_bar = ((jnp.exp(z) - 1.0) / lam)[..., None] * lax.complex(b_re, b_im)
    c = lax.complex(c_re, c_im)
    pw = jnp.exp(z[:, :, None, :] * jnp.arange(n + 1, dtype=F32)[:, None])
    pw_rev = jnp.flip(pw, axis=2)
    cp = c[:, :, :, None, :] * pw[:, :, None, :n, :]
    hi = lax.Precision.HIGHEST
    kk = (jnp.einsum('dgokp,dgpi->dgoki', cp.real, b_bar.real, precision=hi)
          - jnp.einsum('dgokp,dgpi->dgoki', cp.imag, b_bar.imag, precision=hi))
    kk = kk.reshape(2, g * gc, wide)
    src = jnp.arange(wide)
    t_out = jnp.arange(n)[None, :, None, None]
    lag = (t_out - src[None, None, None, :] // gc) * jnp.array([1, -1])[:, None, None, None]
    place = jnp.logical_and(src[None, None, :, None] // gc == lag,
                            src[None, None, :, None] % gc == src[None, None, None, :] % gc)
    kmat = jnp.einsum('drq,daqc->arc', kk, place.astype(F32), precision=hi)
    kmat = kmat.astype(BF16).reshape(n, g, gc, wide).transpose(1, 0, 2, 3).reshape(g, wide, wide)
    pw_in = jnp.stack([pw_rev[0, :, 1:], pw[1, :, :n]], axis=0)
    sb = pw_in.transpose(0, 1, 3, 2)[..., None] * b_bar[:, :, :, None, :]
    ssum = jnp.concatenate([sb.real, sb.imag], axis=2).astype(BF16).reshape(2, g, 2 * p, wide)
    pw_out = jnp.stack([pw[0, :, 1:], pw_rev[1, :, :n]], axis=0)
    cw = pw_out[:, :, :, None, :] * c[:, :, None, :, :]
    wx = jnp.concatenate([cw[0].real, -cw[0].imag, cw[1].real, -cw[1].imag], axis=-1)
    wx = wx.astype(BF16).reshape(g, wide, 4 * p)
    mu = pw[:, :, n].reshape(2, g // 2, 2 * p)
    return kmat, ssum, wx, mu.real, mu.imag


def _gelu_tanh(x):
    return 0.5 * x * (1.0 + jnp.tanh(math.sqrt(2.0 / math.pi) * (x + 0.044715 * (x * x * x))))


def _tail_kernel(h_ref, ya_ref, s_ref, ys_ref, ga_ref, gs_ref, p_ref,
                 d_ref, gw_ref, gb_ref, wa_ref, ws_ref, wo_ref,
                 fn_ref, wg_ref, wu_ref, wd_ref,
                 pn_ref, pg_ref, pp_ref, on_ref, o_ref, *, batch):
    t = TIME_TILE
    y_attn = _dot(_stack_batches(ya_ref, batch, ATTN_W), wa_ref[...])

    def slabs(ref):
        return jnp.concatenate(
            [jnp.concatenate([ref[c, :, b * LANES:(b + 1) * LANES] for c in range(NQ)], axis=1)
             for b in range(batch)], axis=0)

    y = _gelu_tanh(d_ref[...] * slabs(s_ref) + slabs(ys_ref))
    y = y * _sigmoid(_dot(y.astype(BF16), gw_ref[...]) + gb_ref[...])
    y_ssm = _dot(y.astype(BF16), ws_ref[...])
    mix = (_stack_batches(ga_ref, batch, D_MODEL).astype(F32) * y_attn
           + _stack_batches(gs_ref, batch, D_MODEL).astype(F32) * y_ssm)
    h = _stack_batches(h_ref, batch, D_MODEL) + _dot(mix.astype(BF16), wo_ref[...])
    h = _ffn_half_step(h, fn_ref, wg_ref, wu_ref, wd_ref)
    gate = _sigmoid(_dot(_rms(h, pn_ref[...]).astype(BF16), pg_ref[...]))
    pe = jnp.concatenate([p_ref[b] for b in range(batch)], axis=0).astype(BF16)
    h = h + _dot(pe, pp_ref[...]) * gate
    out = _rms(h, on_ref[...])
    for b in range(batch):
        o_ref[b] = out[b * t:(b + 1) * t]


def _tail(h, y_attn, s_in, y_scan, g_attn, g_ssm, p, weights, batch, seq):
    t = TIME_TILE
    def spec(c):
        return pl.BlockSpec((t, batch * c), lambda i: (i, 0))
    slab_spec = pl.BlockSpec((NQ, t, batch * LANES), lambda i: (0, i, 0))
    return pl.pallas_call(
        functools.partial(_tail_kernel, batch=batch),
        grid=(seq // t,),
        in_specs=[spec(D_MODEL), spec(ATTN_W), slab_spec, slab_spec,
                  spec(D_MODEL), spec(D_MODEL),
                  pl.BlockSpec((batch, t, p.shape[-1]), lambda i: (0, i, 0))]
                 + [_const_spec(w.shape) for w in weights],
        out_specs=pl.BlockSpec((batch, t, D_MODEL), lambda i: (0, i, 0)),
        out_shape=jax.ShapeDtypeStruct((batch, seq, D_MODEL), F32),
        compiler_params=_params("parallel"),
        name="tail",
    )(h, y_attn, s_in, y_scan, g_attn, g_ssm, p, *weights)


def kernel(x, p, ffn1_norm, ffn1_w_gate, ffn1_w_up, ffn1_w_down, mix_norm, w_in, na_rpb, ssm_lam_re, ssm_lam_im, ssm_log_dt, ssm_b_re, ssm_b_im, ssm_c_re, ssm_c_im, ssm_d, ssm_glu_w, ssm_glu_b, w_attn_out, w_ssm_out, w_out, ffn2_norm, ffn2_w_gate, ffn2_w_up, ffn2_w_down, ple_norm, ple_w_gate, ple_w_proj, final_norm):
    batch, seq, _ = x.shape
    assert p.shape[0] == 1, "single-layer block"
    i = 0
    row = lambda v: v.reshape(1, -1)
    bf = lambda w: w.astype(BF16)
    h = _ffn(x, row(ffn1_norm[i]), bf(ffn1_w_gate[i]), bf(ffn1_w_up[i]), bf(ffn1_w_down[i]),
             batch, seq)
    q, k, v, s_in, g_attn, g_ssm = _inproj(h, row(mix_norm[i]), bf(w_in[i]), batch, seq)
    y_attn = _attention(q, k, v, *_attn_bias_table(na_rpb[i]), batch, seq)
    scan_weights = _ssm_weights(ssm_lam_re[i], ssm_lam_im[i], ssm_log_dt[i], ssm_b_re[i], ssm_b_im[i],
                                ssm_c_re[i], ssm_c_im[i])
    y_scan = _ssm(s_in, scan_weights, batch, seq)
    weights = [row(ssm_d[i]), bf(ssm_glu_w[i]), row(ssm_glu_b[i]),
               bf(w_attn_out[i]), bf(w_ssm_out[i]), bf(w_out[i]),
               row(ffn2_norm[i]), bf(ffn2_w_gate[i]), bf(ffn2_w_up[i]), bf(ffn2_w_down[i]),
               row(ple_norm[i]), bf(ple_w_gate[i]), bf(ple_w_proj[i]), row(final_norm)]
    return _tail(h, y_attn, s_in, y_scan, g_attn, g_ssm, p[i], weights, batch, seq)
```

```python
import functools
import math

import jax
import jax.numpy as jnp
from jax import lax
from jax.experimental import pallas as pl
from jax.experimental.pallas import tpu as pltpu

D_MODEL = 1024
GRID_W = 64
N_HEADS = 8
HEAD_DIM = 64
ATTN_W = N_HEADS * HEAD_DIM
NA_WIN_H = 8
NA_WIN_W = 16
SSM_W = 512
SSM_GROUP = 16
SSM_GROUPS = SSM_W // SSM_GROUP
SSM_STATE = 64
LAMBDA_RE_MAX = -1e-4
D_FF = 2816
RMS_EPS = 1e-6

LANES = 128
VMEM_LIMIT = 56 * 1024 * 1024
MASK_BIAS = -1e30

TOKEN_TILE = 1024
INPROJ_TILE = 256
TIME_TILE = 128
ATTN_ROWS = NA_WIN_H // 2
ATTN_KEY_ROWS = ATTN_ROWS + NA_WIN_H
FF_CHUNKS = ((0, 1536), (1536, 1280))
SSM_CHUNK = 16
SSM_CHUNKS = LANES
NQ = SSM_W // LANES

BF16 = jnp.bfloat16
F32 = jnp.float32


def _dot(a, b):
    return jnp.dot(a, b, preferred_element_type=F32)


def _rms(x, g):
    ms = jnp.mean(x * x, axis=-1, keepdims=True)
    return x * lax.rsqrt(ms + RMS_EPS) * g


def _sigmoid(x):
    return 1.0 / (1.0 + jnp.exp(-x))


def _const_spec(shape):
    nd = len(shape)
    return pl.BlockSpec(shape, lambda *_: (0,) * nd, pipeline_mode=pl.Buffered(1))


def _params(*sem):
    return pltpu.CompilerParams(dimension_semantics=sem, vmem_limit_bytes=VMEM_LIMIT)


def _ffn_half_step(x, g_ref, wg_ref, wu_ref, wd_ref):
    xn = _rms(x, g_ref[...]).astype(BF16)
    acc = None
    for start, size in FF_CHUNKS:
        gate = _dot(xn, wg_ref[:, start:start + size])
        up = _dot(xn, wu_ref[:, start:start + size])
        act = (gate * _sigmoid(gate) * up).astype(BF16)
        part = _dot(act, wd_ref[start:start + size, :])
        acc = part if acc is None else acc + part
    return x + 0.5 * acc


def _ffn_kernel(x_ref, g_ref, wg_ref, wu_ref, wd_ref, o_ref):
    o_ref[...] = _ffn_half_step(x_ref[...], g_ref, wg_ref, wu_ref, wd_ref)


def _ffn(x, norm_g, wg, wu, wd, batch, seq):
    t = TOKEN_TILE
    return pl.pallas_call(
        _ffn_kernel,
        grid=(batch, seq // t),
        in_specs=[pl.BlockSpec((None, t, D_MODEL), lambda b, i: (b, i, 0)),
                  _const_spec((1, D_MODEL)), _const_spec(wg.shape),
                  _const_spec(wu.shape), _const_spec(wd.shape)],
        out_specs=pl.BlockSpec((t, D_MODEL), lambda b, i: (i, b)),
        out_shape=jax.ShapeDtypeStruct((seq, batch * D_MODEL), F32),
        compiler_params=_params("parallel", "parallel"),
        name="ffn",
    )(x, norm_g, wg, wu, wd)


def _stack_batches(ref, batch, width):
    return jnp.concatenate([ref[:, b * width:(b + 1) * width] for b in range(batch)], axis=0)


def _unstack_batches(ref, x, batch, width):
    t = x.shape[0] // batch
    for b in range(batch):
        ref[:, b * width:(b + 1) * width] = x[b * t:(b + 1) * t]


def _inproj_kernel(h_ref, g_ref, w_ref, q_ref, k_ref, v_ref, s_ref, ga_ref, gs_ref, *, batch):
    t = INPROJ_TILE
    u = _rms(_stack_batches(h_ref, batch, D_MODEL), g_ref[...]).astype(BF16)
    a = ATTN_W
    _unstack_batches(q_ref, (_dot(u, w_ref[:, 0:a]) * (HEAD_DIM ** -0.5)).astype(BF16), batch, a)
    _unstack_batches(k_ref, _dot(u, w_ref[:, a:2 * a]).astype(BF16), batch, a)
    _unstack_batches(v_ref, _dot(u, w_ref[:, 2 * a:3 * a]).astype(BF16), batch, a)
    s0 = 3 * a
    s_in = _dot(u, w_ref[:, s0:s0 + SSM_W])
    for c in range(NQ):
        for b in range(batch):
            s_ref[c, :, b * LANES:(b + 1) * LANES] = s_in[b * t:(b + 1) * t, c * LANES:(c + 1) * LANES]
    g0 = s0 + SSM_W
    _unstack_batches(ga_ref, _sigmoid(_dot(u, w_ref[:, g0:g0 + D_MODEL])).astype(BF16),
                     batch, D_MODEL)
    _unstack_batches(gs_ref, _sigmoid(_dot(u, w_ref[:, g0 + D_MODEL:g0 + 2 * D_MODEL])).astype(BF16),
                     batch, D_MODEL)


def _inproj(h, norm_g, w_in, batch, seq):
    t = INPROJ_TILE
    def spec(c):
        return pl.BlockSpec((t, batch * c), lambda i: (i, 0))
    def shape(c, dt):
        return jax.ShapeDtypeStruct((seq, batch * c), dt)
    return pl.pallas_call(
        functools.partial(_inproj_kernel, batch=batch),
        grid=(seq // t,),
        in_specs=[spec(D_MODEL), _const_spec((1, D_MODEL)), _const_spec(w_in.shape)],
        out_specs=[spec(ATTN_W), spec(ATTN_W), spec(ATTN_W),
                   pl.BlockSpec((NQ, t, batch * LANES), lambda i: (0, i, 0)),
                   spec(D_MODEL), spec(D_MODEL)],
        out_shape=[shape(ATTN_W, BF16), shape(ATTN_W, BF16), shape(ATTN_W, BF16),
                   jax.ShapeDtypeStruct((NQ, seq, batch * LANES), F32),
                   shape(D_MODEL, BF16), shape(D_MODEL, BF16)],
        compiler_params=_params("parallel"),
        name="inproj",
    )(h, norm_g, w_in)


def _attn_kernel(q_ref, k_ref, v_ref, bias_ref, bias_odd_ref, o_ref, *, rows):
    j = pl.program_id(1)
    n_blocks = rows // ATTN_ROWS
    stack = 2 * GRID_W
    lane = lax.broadcasted_iota(jnp.int32, (GRID_W, LANES), 1)
    first = lane < HEAD_DIM

    def stacked_queries(cols):
        pieces = []
        for i in range(ATTN_ROWS):
            q2 = q_ref[i * GRID_W:(i + 1) * GRID_W, cols]
            zero = jnp.zeros_like(q2)
            pieces += [jnp.where(first, q2, zero), jnp.where(first, zero, q2)]
        return jnp.concatenate(pieces, axis=0)

    def scores(lhs, keys):
        return lax.dot_general(lhs, keys, (((1,), (1,)), ((), ())), preferred_element_type=F32)

    def softmax_terms(s):
        p = jnp.exp(s - jnp.max(s, axis=-1, keepdims=True))
        return p.astype(BF16), 1.0 / jnp.sum(p, axis=-1, keepdims=True)

    def write(o, inv, cols):
        o = o * inv
        for i in range(ATTN_ROWS):
            blk = o[i * stack:(i + 1) * stack]
            o_ref[i * GRID_W:(i + 1) * GRID_W, cols] = jnp.where(
                first, blk[0:GRID_W], blk[GRID_W:stack]).astype(BF16)

    is_edge = jnp.logical_or(j == 0, j == n_blocks - 1)

    @pl.when(is_edge)
    def _():
        r0 = j * ATTN_ROWS
        rs = jnp.clip(r0 - NA_WIN_H // 2, 0, rows - NA_WIN_H)
        start = pl.multiple_of(rs * GRID_W, GRID_W)
        nkeys = NA_WIN_H * GRID_W
        for pair in range(N_HEADS // 2):
            cols = slice(pair * LANES, (pair + 1) * LANES)
            hrows = slice(pair * stack, (pair + 1) * stack)
            s = scores(stacked_queries(cols), k_ref[pl.ds(start, nkeys), cols])
            ps, invs = [], []
            for i in range(ATTN_ROWS):
                p, inv = softmax_terms(s[i * stack:(i + 1) * stack] + bias_ref[r0 + i - rs, hrows, :])
                ps.append(p)
                invs.append(inv)
            o = _dot(jnp.concatenate(ps, axis=0), v_ref[pl.ds(start, nkeys), cols])
            write(o, jnp.concatenate(invs, axis=0), cols)

    @pl.when(jnp.logical_not(is_edge))
    def _():
        start = pl.multiple_of((j * ATTN_ROWS - NA_WIN_H // 2) * GRID_W, ATTN_ROWS * GRID_W)
        nkeys = ATTN_KEY_ROWS * GRID_W
        for pair in range(N_HEADS // 2):
            cols = slice(pair * LANES, (pair + 1) * LANES)
            hrows = slice(pair * stack, (pair + 1) * stack)
            s = scores(stacked_queries(cols), k_ref[pl.ds(start, nkeys), cols])
            ps, invs = [], []
            for i in range(ATTN_ROWS):
                lo = i // 2 * 2
                hi = lo + NA_WIN_H + 2 * (i % 2)
                bias = bias_odd_ref[hrows, :] if i % 2 else bias_ref[NA_WIN_H // 2, hrows, :]
                p, inv = softmax_terms(s[i * stack:(i + 1) * stack, lo * GRID_W:hi * GRID_W] + bias)
                parts = [p]
                if lo:
                    parts.insert(0, jnp.zeros((stack, lo * GRID_W), BF16))
                if hi < ATTN_KEY_ROWS:
                    parts.append(jnp.zeros((stack, (ATTN_KEY_ROWS - hi) * GRID_W), BF16))
                ps.append(jnp.concatenate(parts, axis=1) if len(parts) > 1 else p)
                invs.append(inv)
            o = _dot(jnp.concatenate(ps, axis=0), v_ref[pl.ds(start, nkeys), cols])
            write(o, jnp.concatenate(invs, axis=0), cols)


def _attention(q, k, v, bias_tab, bias_odd, batch, seq):
    rows = seq // GRID_W
    qt = ATTN_ROWS * GRID_W
    kv_spec = pl.BlockSpec((seq, ATTN_W), lambda b, r: (0, b))
    return pl.pallas_call(
        functools.partial(_attn_kernel, rows=rows),
        grid=(batch, rows // ATTN_ROWS),
        in_specs=[pl.BlockSpec((qt, ATTN_W), lambda b, r: (r, b)), kv_spec, kv_spec,
                  _const_spec(bias_tab.shape), _const_spec(bias_odd.shape)],
        out_specs=pl.BlockSpec((qt, ATTN_W), lambda b, r: (r, b)),
        out_shape=jax.ShapeDtypeStruct((seq, batch * ATTN_W), BF16),
        compiler_params=_params("parallel", "arbitrary"),
        name="natten",
    )(q, k, v, bias_tab, bias_odd)


def _attn_bias_table(rpb):
    c = jnp.arange(GRID_W)[:, None]
    kc = jnp.arange(GRID_W)[None, :]
    cs = jnp.clip(c - NA_WIN_W // 2, 0, GRID_W - NA_WIN_W)
    inside = (kc >= cs) & (kc < cs + NA_WIN_W)
    dc = kc - c + (NA_WIN_W - 1)
    onehot = (dc[:, :, None] == jnp.arange(2 * NA_WIN_W - 1)[None, None, :]).astype(F32)
    band = jnp.einsum('hyx,ckx->hcyk', rpb.astype(F32), onehot, precision=lax.Precision.HIGHEST)
    band = jnp.where(inside[None, :, None, :], band, MASK_BIAS)
    tabs = [band[:, :, NA_WIN_H - 1 - var:2 * NA_WIN_H - 1 - var, :] for var in range(NA_WIN_H)]
    tab = jnp.stack(tabs, axis=0)
    tab = tab.reshape(NA_WIN_H, N_HEADS * GRID_W, NA_WIN_H * GRID_W)
    pad = jnp.full((N_HEADS * GRID_W, GRID_W), MASK_BIAS, F32)
    tab_odd = jnp.concatenate([pad, tab[NA_WIN_H // 2], pad], axis=1)
    return tab, tab_odd


def _ssm_kernel(s_ref, km_ref, ss_ref, wx_ref, mur_ref, mui_ref, y_ref,
                zt_ref, sw_ref, ytq_ref, *, n_tiles):
    phase = pl.program_id(1)
    tile = pl.program_id(2)
    step_rows, n_chunks, gc, p = SSM_CHUNK, SSM_CHUNKS, SSM_GROUP, SSM_STATE
    n_pairs = SSM_GROUPS // 2
    slab_groups = LANES // gc
    tile_rows = n_chunks * n_pairs

    def pair_rows(gp):
        return pl.ds(pl.multiple_of(tile * tile_rows, tile_rows) + gp, n_chunks, stride=n_pairs)

    @pl.when(phase == 0)
    def _():
        for q in range(NQ):
            for tau in range(step_rows):
                at = s_ref[q, pl.ds(tau, n_chunks, stride=step_rows), :].T.astype(BF16)
                for g8 in range(slab_groups):
                    zt_ref[tile, slab_groups * q + g8, tau * gc:(tau + 1) * gc, :] = (
                        at[g8 * gc:(g8 + 1) * gc, :])
        for d in range(2):
            for gp in range(n_pairs):
                r0 = _dot(ss_ref[d, 2 * gp], zt_ref[tile, 2 * gp])
                r1 = _dot(ss_ref[d, 2 * gp + 1], zt_ref[tile, 2 * gp + 1])
                sw_ref[d, 0, pair_rows(gp), :] = jnp.concatenate([r0[0:p], r1[0:p]], axis=0).T
                sw_ref[d, 1, pair_rows(gp), :] = jnp.concatenate(
                    [r0[p:2 * p], r1[p:2 * p]], axis=0).T

    @pl.when(jnp.logical_and(phase == 0, tile == n_tiles - 1))
    def _():
        total = n_tiles * n_chunks
        mu = [(mur_ref[d], mui_ref[d]) for d in range(2)]

        def step(j, carry):
            new = []
            for d in range(2):
                e_r, e_i = carry[d]
                chunk = j if d == 0 else total - 1 - j
                rows = pl.ds(pl.multiple_of(chunk * n_pairs, n_pairs), n_pairs)
                s_r = sw_ref[d, 0, rows, :]
                s_i = sw_ref[d, 1, rows, :]
                sw_ref[d, 0, rows, :] = e_r
                sw_ref[d, 1, rows, :] = e_i
                m_r, m_i = mu[d]
                new.append((m_r * e_r - m_i * e_i + s_r, m_r * e_i + m_i * e_r + s_i))
            return tuple(new)

        zero = jnp.zeros((n_pairs, LANES), F32)
        lax.fori_loop(0, total, step, ((zero, zero), (zero, zero)), unroll=2)

    @pl.when(phase == 1)
    def _():
        for q in range(NQ):
            for g8 in range(0, slab_groups, 2):
                gp = (slab_groups * q + g8) // 2
                x_t = [[sw_ref[d, c, pair_rows(gp), :].T for c in range(2)] for d in range(2)]
                for h in range(2):
                    g = 2 * gp + h
                    states = jnp.concatenate(
                        [x_t[d][c][h * p:(h + 1) * p] for d in range(2) for c in range(2)],
                        axis=0).astype(BF16)
                    ytq_ref[g8 + h] = _dot(km_ref[g], zt_ref[tile, g]) + _dot(wx_ref[g], states)
            for tau in range(step_rows):
                piece = jnp.concatenate(
                    [ytq_ref[g8, tau * gc:(tau + 1) * gc, :] for g8 in range(slab_groups)], axis=0)
                y_ref[q, pl.ds(tau, n_chunks, stride=step_rows), :] = piece.T


def _ssm(s_in, weights, batch, seq):
    tile = SSM_CHUNK * SSM_CHUNKS
    n_tiles = seq // tile
    n_pairs = SSM_GROUPS // 2
    wide = SSM_CHUNK * SSM_GROUP
    in_spec = pl.BlockSpec((NQ, tile, LANES),
                           lambda b, ph, k: (0, jnp.where(ph == 0, k, n_tiles - 1), b))
    out_spec = pl.BlockSpec((NQ, tile, LANES), lambda b, ph, k: (0, jnp.where(ph == 0, 0, k), b))
    return pl.pallas_call(
        functools.partial(_ssm_kernel, n_tiles=n_tiles),
        grid=(batch, 2, n_tiles),
        in_specs=[in_spec] + [_const_spec(w.shape) for w in weights],
        out_specs=out_spec,
        out_shape=jax.ShapeDtypeStruct((NQ, seq, batch * LANES), F32),
        scratch_shapes=[pltpu.VMEM((n_tiles, SSM_GROUPS, wide, SSM_CHUNKS), BF16),
                        pltpu.VMEM((2, 2, n_tiles * SSM_CHUNKS * n_pairs, LANES), F32),
                        pltpu.VMEM((LANES // SSM_GROUP, wide, SSM_CHUNKS), F32)],
        compiler_params=_params("parallel", "arbitrary", "arbitrary"),
        name="s5_scan",
    )(s_in, *weights)


def _ssm_weights(lam_re, lam_im, log_dt, b_re, b_im, c_re, c_im):
    n, g, p, gc = SSM_CHUNK, SSM_GROUPS, SSM_STATE, SSM_GROUP
    wide = n * gc
    lam = lax.complex(jnp.minimum(lam_re, LAMBDA_RE_MAX), lam_im)
    z = lam * jnp.exp(log_dt)[..., None]
    b_bar = ((jnp.exp(z) - 1.0) / lam)[..., None] * lax.complex(b_re, b_im)
    c = lax.complex(c_re, c_im)
    steps = jnp.arange(n, dtype=F32)
    fwd = jnp.array([1.0, 0.0], F32)[:, None]
    cp = c[:, :, :, None, :] * jnp.exp(z[:, :, None, None, :] * steps[:, None])
    hi = lax.Precision.HIGHEST
    kk = (jnp.einsum('dgokp,dgpi->dgoki', cp.real, b_bar.real, precision=hi)
          - jnp.einsum('dgokp,dgpi->dgoki', cp.imag, b_bar.imag, precision=hi))
    kk = kk.reshape(2, g * gc, wide)
    src = jnp.arange(wide)
    t_out = jnp.arange(n)[None, :, None, None]
    lag = (t_out - src[None, None, None, :] // gc) * jnp.array([1, -1])[:, None, None, None]
    place = jnp.logical_and(src[None, None, :, None] // gc == lag,
                            src[None, None, :, None] % gc == src[None, None, None, :] % gc)
    kmat = jnp.einsum('drq,daqc->arc', kk.astype(BF16), place.astype(BF16),
                      preferred_element_type=F32)
    kmat = kmat.astype(BF16).reshape(n, g, gc, wide).transpose(1, 0, 2, 3).reshape(g, wide, wide)
    e_in = fwd * (n - 1 - steps) + (1.0 - fwd) * steps
    sb = (jnp.exp(z[..., None] * e_in[:, None, None, :])[..., None]
          * b_bar[:, :, :, None, :])
    ssum = jnp.concatenate([sb.real, sb.imag], axis=2).astype(BF16).reshape(2, g, 2 * p, wide)
    e_out = fwd * (steps + 1.0) + (1.0 - fwd) * (n - steps)
    cw = (jnp.exp(z[:, :, None, None, :] * e_out[:, None, :, None, None])
          * c[:, :, None, :, :])
    wx = jnp.concatenate([cw[0].real, -cw[0].imag, cw[1].real, -cw[1].imag], axis=-1)
    wx = wx.astype(BF16).reshape(g, wide, 4 * p)
    mu = jnp.exp(z * n).reshape(2, g // 2, 2 * p)
    return kmat, ssum, wx, mu.real, mu.imag


def _gelu_tanh(x):
    return 0.5 * x * (1.0 + jnp.tanh(math.sqrt(2.0 / math.pi) * (x + 0.044715 * (x * x * x))))


def _tail_kernel(h_ref, ya_ref, s_ref, ys_ref, ga_ref, gs_ref, p_ref,
                 d_ref, gw_ref, gb_ref, wa_ref, ws_ref, wo_ref,
                 fn_ref, wg_ref, wu_ref, wd_ref,
                 pn_ref, pg_ref, pp_ref, on_ref, o_ref, *, batch):
    t = TIME_TILE
    y_attn = _dot(_stack_batches(ya_ref, batch, ATTN_W), wa_ref[...])

    def slabs(ref):
        return jnp.concatenate(
            [jnp.concatenate([ref[c, :, b * LANES:(b + 1) * LANES] for c in range(NQ)], axis=1)
             for b in range(batch)], axis=0)

    y = _gelu_tanh(d_ref[...] * slabs(s_ref) + slabs(ys_ref))
    y = y * _sigmoid(_dot(y.astype(BF16), gw_ref[...]) + gb_ref[...])
    y_ssm = _dot(y.astype(BF16), ws_ref[...])
    mix = (_stack_batches(ga_ref, batch, D_MODEL).astype(F32) * y_attn
           + _stack_batches(gs_ref, batch, D_MODEL).astype(F32) * y_ssm)
    h = _stack_batches(h_ref, batch, D_MODEL) + _dot(mix.astype(BF16), wo_ref[...])
    h = _ffn_half_step(h, fn_ref, wg_ref, wu_ref, wd_ref)
    gate = _sigmoid(_dot(_rms(h, pn_ref[...]).astype(BF16), pg_ref[...]))
    pe = jnp.concatenate([p_ref[b] for b in range(batch)], axis=0).astype(BF16)
    h = h + _dot(pe, pp_ref[...]) * gate
    out = _rms(h, on_ref[...])
    for b in range(batch):
        o_ref[b] = out[b * t:(b + 1) * t]


def _tail(h, y_attn, s_in, y_scan, g_attn, g_ssm, p, weights, batch, seq):
    t = TIME_TILE
    def spec(c):
        return pl.BlockSpec((t, batch * c), lambda i: (i, 0))
    slab_spec = pl.BlockSpec((NQ, t, batch * LANES), lambda i: (0, i, 0))
    return pl.pallas_call(
        functools.partial(_tail_kernel, batch=batch),
        grid=(seq // t,),
        in_specs=[spec(D_MODEL), spec(ATTN_W), slab_spec, slab_spec,
                  spec(D_MODEL), spec(D_MODEL),
                  pl.BlockSpec((batch, t, p.shape[-1]), lambda i: (0, i, 0))]
                 + [_const_spec(w.shape) for w in weights],
        out_specs=pl.BlockSpec((batch, t, D_MODEL), lambda i: (0, i, 0)),
        out_shape=jax.ShapeDtypeStruct((batch, seq, D_MODEL), F32),
        compiler_params=_params("parallel"),
        name="tail",
    )(h, y_attn, s_in, y_scan, g_attn, g_ssm, p, *weights)


def kernel(x, p, ffn1_norm, ffn1_w_gate, ffn1_w_up, ffn1_w_down, mix_norm, w_in, na_rpb, ssm_lam_re, ssm_lam_im, ssm_log_dt, ssm_b_re, ssm_b_im, ssm_c_re, ssm_c_im, ssm_d, ssm_glu_w, ssm_glu_b, w_attn_out, w_ssm_out, w_out, ffn2_norm, ffn2_w_gate, ffn2_w_up, ffn2_w_down, ple_norm, ple_w_gate, ple_w_proj, final_norm):
    batch, seq, _ = x.shape
    assert p.shape[0] == 1, "single-layer block"
    i = 0
    row = lambda v: v.reshape(1, -1)
    bf = lambda w: w.astype(BF16)
    h = _ffn(x, row(ffn1_norm[i]), bf(ffn1_w_gate[i]), bf(ffn1_w_up[i]), bf(ffn1_w_down[i]),
             batch, seq)
    q, k, v, s_in, g_attn, g_ssm = _inproj(h, row(mix_norm[i]), bf(w_in[i]), batch, seq)
    y_attn = _attention(q, k, v, *_attn_bias_table(na_rpb[i]), batch, seq)
    scan_weights = _ssm_weights(ssm_lam_re[i], ssm_lam_im[i], ssm_log_dt[i], ssm_b_re[i], ssm_b_im[i],
                                ssm_c_re[i], ssm_c_im[i])
    y_scan = _ssm(s_in, scan_weights, batch, seq)
    weights = [row(ssm_d[i]), bf(ssm_glu_w[i]), row(ssm_glu_b[i]),
               bf(w_attn_out[i]), bf(w_ssm_out[i]), bf(w_out[i]),
               row(ffn2_norm[i]), bf(ffn2_w_gate[i]), bf(ffn2_w_up[i]), bf(ffn2_w_down[i]),
               row(ple_norm[i]), bf(ple_w_gate[i]), bf(ple_w_proj[i]), row(final_norm)]
    return _tail(h, y_attn, s_in, y_scan, g_attn, g_ssm, p[i], weights, batch, seq)
```

```python
import functools
import math

import jax
import jax.numpy as jnp
from jax import lax
from jax.experimental import pallas as pl
from jax.experimental.pallas import tpu as pltpu

D_MODEL = 1024
GRID_W = 64
N_HEADS = 8
HEAD_DIM = 64
ATTN_W = N_HEADS * HEAD_DIM
NA_WIN_H = 8
NA_WIN_W = 16
SSM_W = 512
SSM_GROUP = 16
SSM_GROUPS = SSM_W // SSM_GROUP
SSM_STATE = 64
LAMBDA_RE_MAX = -1e-4
D_FF = 2816
RMS_EPS = 1e-6

LANES = 128
VMEM_LIMIT = 56 * 1024 * 1024
MASK_BIAS = -1e30

TOKEN_TILE = 1024
INPROJ_TILE = 256
TIME_TILE = 128
ATTN_ROWS = NA_WIN_H // 2
ATTN_KEY_ROWS = ATTN_ROWS + NA_WIN_H
FF_CHUNKS = ((0, 1536), (1536, 1280))
SSM_CHUNK = 16
SSM_CHUNKS = LANES
NQ = SSM_W // LANES

BF16 = jnp.bfloat16
F32 = jnp.float32


def _dot(a, b):
    return jnp.dot(a, b, preferred_element_type=F32)


def _rms(x, g):
    ms = jnp.mean(x * x, axis=-1, keepdims=True)
    return x * lax.rsqrt(ms + RMS_EPS) * g


def _sigmoid(x):
    return 1.0 / (1.0 + jnp.exp(-x))


def _const_spec(shape):
    nd = len(shape)
    return pl.BlockSpec(shape, lambda *_: (0,) * nd, pipeline_mode=pl.Buffered(1))


def _params(*sem):
    return pltpu.CompilerParams(dimension_semantics=sem, vmem_limit_bytes=VMEM_LIMIT)


def _ffn_half_step(x, g_ref, wg_ref, wu_ref, wd_ref):
    xn = _rms(x, g_ref[...]).astype(BF16)
    acc = None
    for start, size in FF_CHUNKS:
        gate = _dot(xn, wg_ref[:, start:start + size])
        up = _dot(xn, wu_ref[:, start:start + size])
        act = (gate * _sigmoid(gate) * up).astype(BF16)
        part = _dot(act, wd_ref[start:start + size, :])
        acc = part if acc is None else acc + part
    return x + 0.5 * acc


def _ffn_kernel(x_ref, g_ref, wg_ref, wu_ref, wd_ref, o_ref):
    o_ref[...] = _ffn_half_step(x_ref[...], g_ref, wg_ref, wu_ref, wd_ref)


def _ffn(x, norm_g, wg, wu, wd, batch, seq):
    t = TOKEN_TILE
    return pl.pallas_call(
        _ffn_kernel,
        grid=(batch, seq // t),
        in_specs=[pl.BlockSpec((None, t, D_MODEL), lambda b, i: (b, i, 0)),
                  _const_spec((1, D_MODEL)), _const_spec(wg.shape),
                  _const_spec(wu.shape), _const_spec(wd.shape)],
        out_specs=pl.BlockSpec((t, D_MODEL), lambda b, i: (i, b)),
        out_shape=jax.ShapeDtypeStruct((seq, batch * D_MODEL), F32),
        compiler_params=_params("parallel", "parallel"),
        name="ffn",
    )(x, norm_g, wg, wu, wd)


def _stack_batches(ref, batch, width):
    return jnp.concatenate([ref[:, b * width:(b + 1) * width] for b in range(batch)], axis=0)


def _unstack_batches(ref, x, batch, width):
    t = x.shape[0] // batch
    for b in range(batch):
        ref[:, b * width:(b + 1) * width] = x[b * t:(b + 1) * t]


def _inproj_kernel(h_ref, g_ref, w_ref, wvt_ref, q_ref, k_ref, vt_ref, s_ref, ga_ref, gs_ref,
                   *, batch):
    t = INPROJ_TILE
    u = _rms(_stack_batches(h_ref, batch, D_MODEL), g_ref[...]).astype(BF16)
    a = ATTN_W
    _unstack_batches(q_ref, (_dot(u, w_ref[:, 0:a]) * (HEAD_DIM ** -0.5)).astype(BF16), batch, a)
    _unstack_batches(k_ref, _dot(u, w_ref[:, a:2 * a]).astype(BF16), batch, a)
    v_t = lax.dot_general(wvt_ref[...], u, (((1,), (1,)), ((), ())),
                          preferred_element_type=F32).astype(BF16)
    for b in range(batch):
        vt_ref[b] = v_t[:, b * t:(b + 1) * t]
    s0 = 3 * a
    s_in = _dot(u, w_ref[:, s0:s0 + SSM_W])
    for c in range(NQ):
        for b in range(batch):
            s_ref[c, :, b * LANES:(b + 1) * LANES] = s_in[b * t:(b + 1) * t, c * LANES:(c + 1) * LANES]
    g0 = s0 + SSM_W
    _unstack_batches(ga_ref, _sigmoid(_dot(u, w_ref[:, g0:g0 + D_MODEL])).astype(BF16),
                     batch, D_MODEL)
    _unstack_batches(gs_ref, _sigmoid(_dot(u, w_ref[:, g0 + D_MODEL:g0 + 2 * D_MODEL])).astype(BF16),
                     batch, D_MODEL)


def _inproj(h, norm_g, w_in, w_v_t, batch, seq):
    t = INPROJ_TILE
    def spec(c):
        return pl.BlockSpec((t, batch * c), lambda i: (i, 0))
    def shape(c, dt):
        return jax.ShapeDtypeStruct((seq, batch * c), dt)
    return pl.pallas_call(
        functools.partial(_inproj_kernel, batch=batch),
        grid=(seq // t,),
        in_specs=[spec(D_MODEL), _const_spec((1, D_MODEL)), _const_spec(w_in.shape),
                  _const_spec(w_v_t.shape)],
        out_specs=[spec(ATTN_W), spec(ATTN_W),
                   pl.BlockSpec((batch, ATTN_W, t), lambda i: (0, 0, i)),
                   pl.BlockSpec((NQ, t, batch * LANES), lambda i: (0, i, 0)),
                   spec(D_MODEL), spec(D_MODEL)],
        out_shape=[shape(ATTN_W, BF16), shape(ATTN_W, BF16),
                   jax.ShapeDtypeStruct((batch, ATTN_W, seq), BF16),
                   jax.ShapeDtypeStruct((NQ, seq, batch * LANES), F32),
                   shape(D_MODEL, BF16), shape(D_MODEL, BF16)],
        compiler_params=_params("parallel"),
        name="inproj",
    )(h, norm_g, w_in, w_v_t)


def _attn_kernel(q_ref, k_ref, vt_ref, bias_ref, bias_odd_ref, o_ref, *, rows):
    j = pl.program_id(1)
    n_blocks = rows // ATTN_ROWS
    stack = 2 * GRID_W
    lane = lax.broadcasted_iota(jnp.int32, (GRID_W, LANES), 1)
    first = lane < HEAD_DIM

    def stacked_queries(cols):
        pieces = []
        for i in range(ATTN_ROWS):
            q2 = q_ref[i * GRID_W:(i + 1) * GRID_W, cols]
            zero = jnp.zeros_like(q2)
            pieces += [jnp.where(first, q2, zero), jnp.where(first, zero, q2)]
        return jnp.concatenate(pieces, axis=0)

    def scores_t(keys, lhs):
        return lax.dot_general(keys, lhs, (((1,), (1,)), ((), ())), preferred_element_type=F32)

    def softmax_terms(s):
        p = jnp.exp(s - jnp.max(s, axis=0, keepdims=True))
        return p.astype(BF16), 1.0 / jnp.sum(p, axis=0, keepdims=True)

    def write(o_t, inv, cols):
        o = (o_t * inv).T
        for i in range(ATTN_ROWS):
            blk = o[i * stack:(i + 1) * stack]
            o_ref[i * GRID_W:(i + 1) * GRID_W, cols] = jnp.where(
                first, blk[0:GRID_W], blk[GRID_W:stack]).astype(BF16)

    is_edge = jnp.logical_or(j == 0, j == n_blocks - 1)

    @pl.when(is_edge)
    def _():
        r0 = j * ATTN_ROWS
        rs = jnp.clip(r0 - NA_WIN_H // 2, 0, rows - NA_WIN_H)
        start = pl.multiple_of(rs * GRID_W, LANES)
        nkeys = NA_WIN_H * GRID_W
        for pair in range(N_HEADS // 2):
            cols = slice(pair * LANES, (pair + 1) * LANES)
            s_t = scores_t(k_ref[pl.ds(start, nkeys), cols], stacked_queries(cols))
            ps, invs = [], []
            for i in range(ATTN_ROWS):
                p, inv = softmax_terms(s_t[:, i * stack:(i + 1) * stack]
                                       + bias_ref[r0 + i - rs, :, cols])
                ps.append(p)
                invs.append(inv)
            o_t = _dot(vt_ref[cols, pl.ds(start, nkeys)], jnp.concatenate(ps, axis=1))
            write(o_t, jnp.concatenate(invs, axis=1), cols)

    @pl.when(jnp.logical_not(is_edge))
    def _():
        start = pl.multiple_of((j * ATTN_ROWS - NA_WIN_H // 2) * GRID_W, ATTN_ROWS * GRID_W)
        nkeys = ATTN_KEY_ROWS * GRID_W
        for pair in range(N_HEADS // 2):
            cols = slice(pair * LANES, (pair + 1) * LANES)
            s_t = scores_t(k_ref[pl.ds(start, nkeys), cols], stacked_queries(cols))
            ps, invs = [], []
            for i in range(ATTN_ROWS):
                lo = i // 2 * 2
                hi = lo + NA_WIN_H + 2 * (i % 2)
                bias = bias_odd_ref[:, cols] if i % 2 else bias_ref[NA_WIN_H // 2, :, cols]
                p, inv = softmax_terms(
                    s_t[lo * GRID_W:hi * GRID_W, i * stack:(i + 1) * stack] + bias)
                parts = [p]
                if lo:
                    parts.insert(0, jnp.zeros((lo * GRID_W, stack), BF16))
                if hi < ATTN_KEY_ROWS:
                    parts.append(jnp.zeros(((ATTN_KEY_ROWS - hi) * GRID_W, stack), BF16))
                ps.append(jnp.concatenate(parts, axis=0) if len(parts) > 1 else p)
                invs.append(inv)
            o_t = _dot(vt_ref[cols, pl.ds(start, nkeys)], jnp.concatenate(ps, axis=1))
            write(o_t, jnp.concatenate(invs, axis=1), cols)


def _attention(q, k, v_t, bias_tab, bias_odd, batch, seq):
    rows = seq // GRID_W
    qt = ATTN_ROWS * GRID_W
    return pl.pallas_call(
        functools.partial(_attn_kernel, rows=rows),
        grid=(batch, rows // ATTN_ROWS),
        in_specs=[pl.BlockSpec((qt, ATTN_W), lambda b, r: (r, b)),
                  pl.BlockSpec((seq, ATTN_W), lambda b, r: (0, b)),
                  pl.BlockSpec((None, ATTN_W, seq), lambda b, r: (b, 0, 0)),
                  _const_spec(bias_tab.shape), _const_spec(bias_odd.shape)],
        out_specs=pl.BlockSpec((qt, ATTN_W), lambda b, r: (r, b)),
        out_shape=jax.ShapeDtypeStruct((seq, batch * ATTN_W), BF16),
        compiler_params=_params("parallel", "arbitrary"),
        name="natten",
    )(q, k, v_t, bias_tab, bias_odd)


def _attn_bias_table(rpb):
    c = jnp.arange(GRID_W)[:, None]
    kc = jnp.arange(GRID_W)[None, :]
    cs = jnp.clip(c - NA_WIN_W // 2, 0, GRID_W - NA_WIN_W)
    inside = (kc >= cs) & (kc < cs + NA_WIN_W)
    dc = kc - c + (NA_WIN_W - 1)
    onehot = (dc[:, :, None] == jnp.arange(2 * NA_WIN_W - 1)[None, None, :]).astype(F32)
    band = jnp.einsum('hyx,ckx->ykhc', rpb.astype(F32), onehot, precision=lax.Precision.HIGHEST)
    band = jnp.where(inside.T[None, :, None, :], band, MASK_BIAS)
    band = band.reshape(2 * NA_WIN_H - 1, GRID_W, N_HEADS * GRID_W)
    tabs = [band[NA_WIN_H - 1 - var:2 * NA_WIN_H - 1 - var] for var in range(NA_WIN_H)]
    tab = jnp.stack(tabs, axis=0).reshape(NA_WIN_H, NA_WIN_H * GRID_W, N_HEADS * GRID_W)
    pad = jnp.full((GRID_W, N_HEADS * GRID_W), MASK_BIAS, F32)
    tab_odd = jnp.concatenate([pad, tab[NA_WIN_H // 2], pad], axis=0)
    return tab, tab_odd


def _ssm_kernel(s_ref, km_ref, ss_ref, wx_ref, mur_ref, mui_ref, y_ref,
                zt_ref, sw_ref, ytq_ref, *, n_tiles):
    phase = pl.program_id(1)
    tile = pl.program_id(2)
    step_rows, n_chunks, gc, p = SSM_CHUNK, SSM_CHUNKS, SSM_GROUP, SSM_STATE
    n_pairs = SSM_GROUPS // 2
    slab_groups = LANES // gc
    tile_rows = n_chunks * n_pairs

    def pair_rows(gp):
        return pl.ds(pl.multiple_of(tile * tile_rows, tile_rows) + gp, n_chunks, stride=n_pairs)

    @pl.when(phase == 0)
    def _():
        for q in range(NQ):
            for tau in range(step_rows):
                at = s_ref[q, pl.ds(tau, n_chunks, stride=step_rows), :].T.astype(BF16)
                for g8 in range(slab_groups):
                    zt_ref[tile, slab_groups * q + g8, tau * gc:(tau + 1) * gc, :] = (
                        at[g8 * gc:(g8 + 1) * gc, :])
        for d in range(2):
            for gp in range(n_pairs):
                r0 = _dot(ss_ref[d, 2 * gp], zt_ref[tile, 2 * gp])
                r1 = _dot(ss_ref[d, 2 * gp + 1], zt_ref[tile, 2 * gp + 1])
                sw_ref[d, 0, pair_rows(gp), :] = jnp.concatenate([r0[0:p], r1[0:p]], axis=0).T
                sw_ref[d, 1, pair_rows(gp), :] = jnp.concatenate(
                    [r0[p:2 * p], r1[p:2 * p]], axis=0).T

    @pl.when(jnp.logical_and(phase == 0, tile == n_tiles - 1))
    def _():
        total = n_tiles * n_chunks
        mu = [(mur_ref[d], mui_ref[d]) for d in range(2)]

        def step(j, carry):
            new = []
            for d in range(2):
                e_r, e_i = carry[d]
                chunk = j if d == 0 else total - 1 - j
                rows = pl.ds(pl.multiple_of(chunk * n_pairs, n_pairs), n_pairs)
                s_r = sw_ref[d, 0, rows, :]
                s_i = sw_ref[d, 1, rows, :]
                sw_ref[d, 0, rows, :] = e_r
                sw_ref[d, 1, rows, :] = e_i
                m_r, m_i = mu[d]
                new.append((m_r * e_r - m_i * e_i + s_r, m_r * e_i + m_i * e_r + s_i))
            return tuple(new)

        zero = jnp.zeros((n_pairs, LANES), F32)
        lax.fori_loop(0, total, step, ((zero, zero), (zero, zero)), unroll=2)

    @pl.when(phase == 1)
    def _():
        for q in range(NQ):
            for g8 in range(0, slab_groups, 2):
                gp = (slab_groups * q + g8) // 2
                x_t = [[sw_ref[d, c, pair_rows(gp), :].T for c in range(2)] for d in range(2)]
                for h in range(2):
                    g = 2 * gp + h
                    states = jnp.concatenate(
                        [x_t[d][c][h * p:(h + 1) * p] for d in range(2) for c in range(2)],
                        axis=0).astype(BF16)
                    ytq_ref[g8 + h] = _dot(km_ref[g], zt_ref[tile, g]) + _dot(wx_ref[g], states)
            for tau in range(step_rows):
                piece = jnp.concatenate(
                    [ytq_ref[g8, tau * gc:(tau + 1) * gc, :] for g8 in range(slab_groups)], axis=0)
                y_ref[q, pl.ds(tau, n_chunks, stride=step_rows), :] = piece.T


def _ssm(s_in, weights, batch, seq):
    tile = SSM_CHUNK * SSM_CHUNKS
    n_tiles = seq // tile
    n_pairs = SSM_GROUPS // 2
    wide = SSM_CHUNK * SSM_GROUP
    in_spec = pl.BlockSpec((NQ, tile, LANES),
                           lambda b, ph, k: (0, jnp.where(ph == 0, k, n_tiles - 1), b))
    out_spec = pl.BlockSpec((NQ, tile, LANES), lambda b, ph, k: (0, jnp.where(ph == 0, 0, k), b))
    return pl.pallas_call(
        functools.partial(_ssm_kernel, n_tiles=n_tiles),
        grid=(batch, 2, n_tiles),
        in_specs=[in_spec] + [_const_spec(w.shape) for w in weights],
        out_specs=out_spec,
        out_shape=jax.ShapeDtypeStruct((NQ, seq, batch * LANES), F32),
        scratch_shapes=[pltpu.VMEM((n_tiles, SSM_GROUPS, wide, SSM_CHUNKS), BF16),
                        pltpu.VMEM((2, 2, n_tiles * SSM_CHUNKS * n_pairs, LANES), F32),
                        pltpu.VMEM((LANES // SSM_GROUP, wide, SSM_CHUNKS), F32)],
        compiler_params=_params("parallel", "arbitrary", "arbitrary"),
        name="s5_scan",
    )(s_in, *weights)


def _ssm_weights(lam_re, lam_im, log_dt, b_re, b_im, c_re, c_im):
    n, g, p, gc = SSM_CHUNK, SSM_GROUPS, SSM_STATE, SSM_GROUP
    wide = n * gc
    lam = lax.complex(jnp.minimum(lam_re, LAMBDA_RE_MAX), lam_im)
    z = lam * jnp.exp(log_dt)[..., None]
    b_bar = ((jnp.exp(z) - 1.0) / lam)[..., None] * lax.complex(b_re, b_im)
    c = lax.complex(c_re, c_im)
    steps = jnp.arange(n, dtype=F32)
    fwd = jnp.array([1.0, 0.0], F32)[:, None]
    cp = c[:, :, :, None, :] * jnp.exp(z[:, :, None, None, :] * steps[:, None])
    hi = lax.Precision.HIGHEST
    kk = (jnp.einsum('dgokp,dgpi->dgoki', cp.real, b_bar.real, precision=hi)
          - jnp.einsum('dgokp,dgpi->dgoki', cp.imag, b_bar.imag, precision=hi))
    kk = kk.reshape(2, g * gc, wide)
    src = jnp.arange(wide)
    t_out = jnp.arange(n)[None, :, None, None]
    lag = (t_out - src[None, None, None, :] // gc) * jnp.array([1, -1])[:, None, None, None]
    place = jnp.logical_and(src[None, None, :, None] // gc == lag,
                            src[None, None, :, None] % gc == src[None, None, None, :] % gc)
    kmat = jnp.einsum('drq,daqc->arc', kk.astype(BF16), place.astype(BF16),
                      preferred_element_type=F32)
    kmat = kmat.astype(BF16).reshape(n, g, gc, wide).transpose(1, 0, 2, 3).reshape(g, wide, wide)
    e_in = fwd * (n - 1 - steps) + (1.0 - fwd) * steps
    sb = (jnp.exp(z[..., None] * e_in[:, None, None, :])[..., None]
          * b_bar[:, :, :, None, :])
    ssum = jnp.concatenate([sb.real, sb.imag], axis=2).astype(BF16).reshape(2, g, 2 * p, wide)
    e_out = fwd * (steps + 1.0) + (1.0 - fwd) * (n - steps)
    cw = (jnp.exp(z[:, :, None, None, :] * e_out[:, None, :, None, None])
          * c[:, :, None, :, :])
    wx = jnp.concatenate([cw[0].real, -cw[0].imag, cw[1].real, -cw[1].imag], axis=-1)
    wx = wx.astype(BF16).reshape(g, wide, 4 * p)
    mu = jnp.exp(z * n).reshape(2, g // 2, 2 * p)
    return kmat, ssum, wx, mu.real, mu.imag


def _gelu_tanh(x):
    return 0.5 * x * (1.0 + jnp.tanh(math.sqrt(2.0 / math.pi) * (x + 0.044715 * (x * x * x))))


def _tail_kernel(h_ref, ya_ref, s_ref, ys_ref, ga_ref, gs_ref, p_ref,
                 d_ref, gw_ref, gb_ref, wa_ref, ws_ref, wo_ref,
                 fn_ref, wg_ref, wu_ref, wd_ref,
                 pn_ref, pg_ref, pp_ref, on_ref, o_ref, *, batch):
    t = TIME_TILE
    y_attn = _dot(_stack_batches(ya_ref, batch, ATTN_W), wa_ref[...])

    def slabs(ref):
        return jnp.concatenate(
            [jnp.concatenate([ref[c, :, b * LANES:(b + 1) * LANES] for c in range(NQ)], axis=1)
             for b in range(batch)], axis=0)

    y = _gelu_tanh(d_ref[...] * slabs(s_ref) + slabs(ys_ref))
    y = y * _sigmoid(_dot(y.astype(BF16), gw_ref[...]) + gb_ref[...])
    y_ssm = _dot(y.astype(BF16), ws_ref[...])
    mix = (_stack_batches(ga_ref, batch, D_MODEL).astype(F32) * y_attn
           + _stack_batches(gs_ref, batch, D_MODEL).astype(F32) * y_ssm)
    h = _stack_batches(h_ref, batch, D_MODEL) + _dot(mix.astype(BF16), wo_ref[...])
    h = _ffn_half_step(h, fn_ref, wg_ref, wu_ref, wd_ref)
    gate = _sigmoid(_dot(_rms(h, pn_ref[...]).astype(BF16), pg_ref[...]))
    pe = jnp.concatenate([p_ref[b] for b in range(batch)], axis=0).astype(BF16)
    h = h + _dot(pe, pp_ref[...]) * gate
    out = _rms(h, on_ref[...])
    for b in range(batch):
        o_ref[b] = out[b * t:(b + 1) * t]


def _tail(h, y_attn, s_in, y_scan, g_attn, g_ssm, p, weights, batch, seq):
    t = TIME_TILE
    def spec(c):
        return pl.BlockSpec((t, batch * c), lambda i: (i, 0))
    slab_spec = pl.BlockSpec((NQ, t, batch * LANES), lambda i: (0, i, 0))
    return pl.pallas_call(
        functools.partial(_tail_kernel, batch=batch),
        grid=(seq // t,),
        in_specs=[spec(D_MODEL), spec(ATTN_W), slab_spec, slab_spec,
                  spec(D_MODEL), spec(D_MODEL),
                  pl.BlockSpec((batch, t, p.shape[-1]), lambda i: (0, i, 0))]
                 + [_const_spec(w.shape) for w in weights],
        out_specs=pl.BlockSpec((batch, t, D_MODEL), lambda i: (0, i, 0)),
        out_shape=jax.ShapeDtypeStruct((batch, seq, D_MODEL), F32),
        compiler_params=_params("parallel"),
        name="tail",
    )(h, y_attn, s_in, y_scan, g_attn, g_ssm, p, *weights)


def kernel(x, p, ffn1_norm, ffn1_w_gate, ffn1_w_up, ffn1_w_down, mix_norm, w_in, na_rpb, ssm_lam_re, ssm_lam_im, ssm_log_dt, ssm_b_re, ssm_b_im, ssm_c_re, ssm_c_im, ssm_d, ssm_glu_w, ssm_glu_b, w_attn_out, w_ssm_out, w_out, ffn2_norm, ffn2_w_gate, ffn2_w_up, ffn2_w_down, ple_norm, ple_w_gate, ple_w_proj, final_norm):
    batch, seq, _ = x.shape
    assert p.shape[0] == 1, "single-layer block"
    i = 0
    row = lambda v: v.reshape(1, -1)
    bf = lambda w: w.astype(BF16)
    h = _ffn(x, row(ffn1_norm[i]), bf(ffn1_w_gate[i]), bf(ffn1_w_up[i]), bf(ffn1_w_down[i]),
             batch, seq)
    w_v_t = bf(w_in[i][:, 2 * ATTN_W:3 * ATTN_W].T)
    q, k, v_t, s_in, g_attn, g_ssm = _inproj(h, row(mix_norm[i]), bf(w_in[i]), w_v_t, batch, seq)
    y_attn = _attention(q, k, v_t, *_attn_bias_table(na_rpb[i]), batch, seq)
    scan_weights = _ssm_weights(ssm_lam_re[i], ssm_lam_im[i], ssm_log_dt[i], ssm_b_re[i], ssm_b_im[i],
                                ssm_c_re[i], ssm_c_im[i])
    y_scan = _ssm(s_in, scan_weights, batch, seq)
    weights = [row(ssm_d[i]), bf(ssm_glu_w[i]), row(ssm_glu_b[i]),
               bf(w_attn_out[i]), bf(w_ssm_out[i]), bf(w_out[i]),
               row(ffn2_norm[i]), bf(ffn2_w_gate[i]), bf(ffn2_w_up[i]), bf(ffn2_w_down[i]),
               row(ple_norm[i]), bf(ple_w_gate[i]), bf(ple_w_proj[i]), row(final_norm)]
    return _tail(h, y_attn, s_in, y_scan, g_attn, g_ssm, p[i], weights, batch, seq)
```

```python
import functools
import math

import jax
import jax.numpy as jnp
from jax import lax
from jax.experimental import pallas as pl
from jax.experimental.pallas import tpu as pltpu

D_MODEL = 1024
GRID_W = 64
N_HEADS = 8
HEAD_DIM = 64
ATTN_W = N_HEADS * HEAD_DIM
NA_WIN_H = 8
NA_WIN_W = 16
SSM_W = 512
SSM_GROUP = 16
SSM_GROUPS = SSM_W // SSM_GROUP
SSM_STATE = 64
LAMBDA_RE_MAX = -1e-4
D_FF = 2816
RMS_EPS = 1e-6

LANES = 128
VMEM_LIMIT = 56 * 1024 * 1024
TAIL_VMEM_LIMIT = 60 * 1024 * 1024
MASK_BIAS = -1e30

TOKEN_TILE = 1024
INPROJ_TILE = 256
TIME_TILE = 128
ATTN_ROWS = NA_WIN_H // 2
ATTN_KEY_ROWS = ATTN_ROWS + NA_WIN_H
FF_CHUNKS = ((0, 1536), (1536, 1280))
SSM_CHUNK = 16
SSM_CHUNKS = LANES
NQ = SSM_W // LANES

BF16 = jnp.bfloat16
F32 = jnp.float32


def _dot(a, b):
    return jnp.dot(a, b, preferred_element_type=F32)


def _rms(x, g):
    ms = jnp.mean(x * x, axis=-1, keepdims=True)
    return x * lax.rsqrt(ms + RMS_EPS) * g


def _sigmoid(x):
    return 1.0 / (1.0 + jnp.exp(-x))


def _const_spec(shape):
    nd = len(shape)
    return pl.BlockSpec(shape, lambda *_: (0,) * nd, pipeline_mode=pl.Buffered(1))


def _params(*sem):
    return pltpu.CompilerParams(dimension_semantics=sem, vmem_limit_bytes=VMEM_LIMIT)


def _ffn_half_step(x, g_ref, wg_ref, wu_ref, wd_ref):
    xn = _rms(x, g_ref[...]).astype(BF16)
    acc = None
    for start, size in FF_CHUNKS:
        gate = _dot(xn, wg_ref[:, start:start + size])
        up = _dot(xn, wu_ref[:, start:start + size])
        act = (gate * _sigmoid(gate) * up).astype(BF16)
        part = _dot(act, wd_ref[start:start + size, :])
        acc = part if acc is None else acc + part
    return x + 0.5 * acc


def _ffn_kernel(x_ref, g_ref, wg_ref, wu_ref, wd_ref, o_ref):
    o_ref[...] = _ffn_half_step(x_ref[...], g_ref, wg_ref, wu_ref, wd_ref)


def _ffn(x, norm_g, wg, wu, wd, batch, seq):
    t = TOKEN_TILE
    return pl.pallas_call(
        _ffn_kernel,
        grid=(batch, seq // t),
        in_specs=[pl.BlockSpec((None, t, D_MODEL), lambda b, i: (b, i, 0)),
                  _const_spec((1, D_MODEL)), _const_spec(wg.shape),
                  _const_spec(wu.shape), _const_spec(wd.shape)],
        out_specs=pl.BlockSpec((t, D_MODEL), lambda b, i: (i, b)),
        out_shape=jax.ShapeDtypeStruct((seq, batch * D_MODEL), F32),
        compiler_params=_params("parallel", "parallel"),
        name="ffn",
    )(x, norm_g, wg, wu, wd)


def _stack_batches(ref, batch, width):
    return jnp.concatenate([ref[:, b * width:(b + 1) * width] for b in range(batch)], axis=0)


def _unstack_batches(ref, x, batch, width):
    t = x.shape[0] // batch
    for b in range(batch):
        ref[:, b * width:(b + 1) * width] = x[b * t:(b + 1) * t]


def _inproj_kernel(h_ref, g_ref, w_ref, q_ref, k_ref, v_ref, s_ref, ga_ref, gs_ref, *, batch):
    t = INPROJ_TILE
    u = _rms(_stack_batches(h_ref, batch, D_MODEL), g_ref[...]).astype(BF16)
    a = ATTN_W
    _unstack_batches(q_ref, (_dot(u, w_ref[:, 0:a]) * (HEAD_DIM ** -0.5)).astype(BF16), batch, a)
    _unstack_batches(k_ref, _dot(u, w_ref[:, a:2 * a]).astype(BF16), batch, a)
    _unstack_batches(v_ref, _dot(u, w_ref[:, 2 * a:3 * a]).astype(BF16), batch, a)
    s0 = 3 * a
    s_in = _dot(u, w_ref[:, s0:s0 + SSM_W])
    for c in range(NQ):
        for b in range(batch):
            s_ref[c, :, b * LANES:(b + 1) * LANES] = s_in[b * t:(b + 1) * t, c * LANES:(c + 1) * LANES]
    g0 = s0 + SSM_W
    _unstack_batches(ga_ref, _sigmoid(_dot(u, w_ref[:, g0:g0 + D_MODEL])).astype(BF16),
                     batch, D_MODEL)
    _unstack_batches(gs_ref, _sigmoid(_dot(u, w_ref[:, g0 + D_MODEL:g0 + 2 * D_MODEL])).astype(BF16),
                     batch, D_MODEL)


def _inproj(h, norm_g, w_in, batch, seq):
    t = INPROJ_TILE
    def spec(c):
        return pl.BlockSpec((t, batch * c), lambda i: (i, 0))
    def shape(c, dt):
        return jax.ShapeDtypeStruct((seq, batch * c), dt)
    return pl.pallas_call(
        functools.partial(_inproj_kernel, batch=batch),
        grid=(seq // t,),
        in_specs=[spec(D_MODEL), _const_spec((1, D_MODEL)), _const_spec(w_in.shape)],
        out_specs=[spec(ATTN_W), spec(ATTN_W), spec(ATTN_W),
                   pl.BlockSpec((NQ, t, batch * LANES), lambda i: (0, i, 0)),
                   spec(D_MODEL), spec(D_MODEL)],
        out_shape=[shape(ATTN_W, BF16), shape(ATTN_W, BF16), shape(ATTN_W, BF16),
                   jax.ShapeDtypeStruct((NQ, seq, batch * LANES), F32),
                   shape(D_MODEL, BF16), shape(D_MODEL, BF16)],
        compiler_params=_params("parallel"),
        name="inproj",
    )(h, norm_g, w_in)


def _attn_kernel(q_ref, k_ref, v_ref, bias_ref, bias_odd_ref, o_ref, *, rows):
    j = pl.program_id(1)
    n_blocks = rows // ATTN_ROWS
    stack = 2 * GRID_W
    lane = lax.broadcasted_iota(jnp.int32, (GRID_W, LANES), 1)
    first = lane < HEAD_DIM

    def stacked_queries(cols):
        pieces = []
        for i in range(ATTN_ROWS):
            q2 = q_ref[i * GRID_W:(i + 1) * GRID_W, cols]
            zero = jnp.zeros_like(q2)
            pieces += [jnp.where(first, q2, zero), jnp.where(first, zero, q2)]
        return jnp.concatenate(pieces, axis=0)

    def scores(lhs, keys):
        return lax.dot_general(lhs, keys, (((1,), (1,)), ((), ())), preferred_element_type=F32)

    def softmax_terms(s):
        p = jnp.exp(s - jnp.max(s, axis=-1, keepdims=True))
        return p.astype(BF16), 1.0 / jnp.sum(p, axis=-1, keepdims=True)

    def write(o, inv, cols):
        o = o * inv
        for i in range(ATTN_ROWS):
            blk = o[i * stack:(i + 1) * stack]
            o_ref[i * GRID_W:(i + 1) * GRID_W, cols] = jnp.where(
                first, blk[0:GRID_W], blk[GRID_W:stack]).astype(BF16)

    is_edge = jnp.logical_or(j == 0, j == n_blocks - 1)

    @pl.when(is_edge)
    def _():
        r0 = j * ATTN_ROWS
        rs = jnp.clip(r0 - NA_WIN_H // 2, 0, rows - NA_WIN_H)
        start = pl.multiple_of(rs * GRID_W, GRID_W)
        nkeys = NA_WIN_H * GRID_W
        for pair in range(N_HEADS // 2):
            cols = slice(pair * LANES, (pair + 1) * LANES)
            hrows = slice(pair * stack, (pair + 1) * stack)
            s = scores(stacked_queries(cols), k_ref[pl.ds(start, nkeys), cols])
            ps, invs = [], []
            for i in range(ATTN_ROWS):
                p, inv = softmax_terms(s[i * stack:(i + 1) * stack] + bias_ref[r0 + i - rs, hrows, :])
                ps.append(p)
                invs.append(inv)
            o = _dot(jnp.concatenate(ps, axis=0), v_ref[pl.ds(start, nkeys), cols])
            write(o, jnp.concatenate(invs, axis=0), cols)

    @pl.when(jnp.logical_not(is_edge))
    def _():
        start = pl.multiple_of((j * ATTN_ROWS - NA_WIN_H // 2) * GRID_W, ATTN_ROWS * GRID_W)
        nkeys = ATTN_KEY_ROWS * GRID_W
        for pair in range(N_HEADS // 2):
            cols = slice(pair * LANES, (pair + 1) * LANES)
            hrows = slice(pair * stack, (pair + 1) * stack)
            s = scores(stacked_queries(cols), k_ref[pl.ds(start, nkeys), cols])
            ps, invs = [], []
            for i in range(ATTN_ROWS):
                lo = i // 2 * 2
                hi = lo + NA_WIN_H + 2 * (i % 2)
                bias = bias_odd_ref[hrows, :] if i % 2 else bias_ref[NA_WIN_H // 2, hrows, :]
                p, inv = softmax_terms(s[i * stack:(i + 1) * stack, lo * GRID_W:hi * GRID_W] + bias)
                parts = [p]
                if lo:
                    parts.insert(0, jnp.zeros((stack, lo * GRID_W), BF16))
                if hi < ATTN_KEY_ROWS:
                    parts.append(jnp.zeros((stack, (ATTN_KEY_ROWS - hi) * GRID_W), BF16))
                ps.append(jnp.concatenate(parts, axis=1) if len(parts) > 1 else p)
                invs.append(inv)
            o = _dot(jnp.concatenate(ps, axis=0), v_ref[pl.ds(start, nkeys), cols])
            write(o, jnp.concatenate(invs, axis=0), cols)


def _attention(q, k, v, bias_tab, bias_odd, batch, seq):
    rows = seq // GRID_W
    qt = ATTN_ROWS * GRID_W
    kv_spec = pl.BlockSpec((seq, ATTN_W), lambda b, r: (0, b))
    return pl.pallas_call(
        functools.partial(_attn_kernel, rows=rows),
        grid=(batch, rows // ATTN_ROWS),
        in_specs=[pl.BlockSpec((qt, ATTN_W), lambda b, r: (r, b)), kv_spec, kv_spec,
                  _const_spec(bias_tab.shape), _const_spec(bias_odd.shape)],
        out_specs=pl.BlockSpec((qt, ATTN_W), lambda b, r: (r, b)),
        out_shape=jax.ShapeDtypeStruct((seq, batch * ATTN_W), BF16),
        compiler_params=_params("parallel", "arbitrary"),
        name="natten",
    )(q, k, v, bias_tab, bias_odd)


def _attn_bias_table(rpb):
    c = jnp.arange(GRID_W)[:, None]
    kc = jnp.arange(GRID_W)[None, :]
    cs = jnp.clip(c - NA_WIN_W // 2, 0, GRID_W - NA_WIN_W)
    inside = (kc >= cs) & (kc < cs + NA_WIN_W)
    dc = kc - c + (NA_WIN_W - 1)
    onehot = (dc[:, :, None] == jnp.arange(2 * NA_WIN_W - 1)[None, None, :]).astype(F32)
    band = jnp.einsum('hyx,ckx->hcyk', rpb.astype(F32), onehot, precision=lax.Precision.HIGHEST)
    band = jnp.where(inside[None, :, None, :], band, MASK_BIAS)
    tabs = [band[:, :, NA_WIN_H - 1 - var:2 * NA_WIN_H - 1 - var, :] for var in range(NA_WIN_H)]
    tab = jnp.stack(tabs, axis=0)
    tab = tab.reshape(NA_WIN_H, N_HEADS * GRID_W, NA_WIN_H * GRID_W)
    pad = jnp.full((N_HEADS * GRID_W, GRID_W), MASK_BIAS, F32)
    tab_odd = jnp.concatenate([pad, tab[NA_WIN_H // 2], pad], axis=1)
    return tab, tab_odd


def _ssm_kernel(s_ref, km_ref, ss_ref, wx_ref, mur_ref, mui_ref, y_ref,
                zt_ref, sw_ref, ytq_ref, *, n_tiles):
    phase = pl.program_id(1)
    tile = pl.program_id(2)
    step_rows, n_chunks, gc, p = SSM_CHUNK, SSM_CHUNKS, SSM_GROUP, SSM_STATE
    n_pairs = SSM_GROUPS // 2
    slab_groups = LANES // gc
    tile_rows = n_chunks * n_pairs

    def pair_rows(gp):
        return pl.ds(pl.multiple_of(tile * tile_rows, tile_rows) + gp, n_chunks, stride=n_pairs)

    @pl.when(phase == 0)
    def _():
        for q in range(NQ):
            for tau in range(step_rows):
                at = s_ref[q, pl.ds(tau, n_chunks, stride=step_rows), :].T.astype(BF16)
                for g8 in range(slab_groups):
                    zt_ref[tile, slab_groups * q + g8, tau * gc:(tau + 1) * gc, :] = (
                        at[g8 * gc:(g8 + 1) * gc, :])
        for d in range(2):
            for gp in range(n_pairs):
                r0 = _dot(ss_ref[d, 2 * gp], zt_ref[tile, 2 * gp])
                r1 = _dot(ss_ref[d, 2 * gp + 1], zt_ref[tile, 2 * gp + 1])
                sw_ref[d, 0, pair_rows(gp), :] = jnp.concatenate([r0[0:p], r1[0:p]], axis=0).T
                sw_ref[d, 1, pair_rows(gp), :] = jnp.concatenate(
                    [r0[p:2 * p], r1[p:2 * p]], axis=0).T

    @pl.when(jnp.logical_and(phase == 0, tile == n_tiles - 1))
    def _():
        total = n_tiles * n_chunks
        mu = [(mur_ref[d], mui_ref[d]) for d in range(2)]

        def step(j, carry):
            new = []
            for d in range(2):
                e_r, e_i = carry[d]
                chunk = j if d == 0 else total - 1 - j
                rows = pl.ds(pl.multiple_of(chunk * n_pairs, n_pairs), n_pairs)
                s_r = sw_ref[d, 0, rows, :]
                s_i = sw_ref[d, 1, rows, :]
                sw_ref[d, 0, rows, :] = e_r
                sw_ref[d, 1, rows, :] = e_i
                m_r, m_i = mu[d]
                new.append((m_r * e_r - m_i * e_i + s_r, m_r * e_i + m_i * e_r + s_i))
            return tuple(new)

        zero = jnp.zeros((n_pairs, LANES), F32)
        lax.fori_loop(0, total, step, ((zero, zero), (zero, zero)), unroll=2)

    @pl.when(phase == 1)
    def _():
        for q in range(NQ):
            for g8 in range(0, slab_groups, 2):
                gp = (slab_groups * q + g8) // 2
                x_t = [[sw_ref[d, c, pair_rows(gp), :].T for c in range(2)] for d in range(2)]
                for h in range(2):
                    g = 2 * gp + h
                    states = jnp.concatenate(
                        [x_t[d][c][h * p:(h + 1) * p] for d in range(2) for c in range(2)],
                        axis=0).astype(BF16)
                    ytq_ref[g8 + h] = _dot(km_ref[g], zt_ref[tile, g]) + _dot(wx_ref[g], states)
            for tau in range(step_rows):
                piece = jnp.concatenate(
                    [ytq_ref[g8, tau * gc:(tau + 1) * gc, :] for g8 in range(slab_groups)], axis=0)
                y_ref[q, pl.ds(tau, n_chunks, stride=step_rows), :] = piece.T


def _ssm(s_in, weights, batch, seq):
    tile = SSM_CHUNK * SSM_CHUNKS
    n_tiles = seq // tile
    n_pairs = SSM_GROUPS // 2
    wide = SSM_CHUNK * SSM_GROUP
    in_spec = pl.BlockSpec((NQ, tile, LANES),
                           lambda b, ph, k: (0, jnp.where(ph == 0, k, n_tiles - 1), b))
    out_spec = pl.BlockSpec((NQ, tile, LANES), lambda b, ph, k: (0, jnp.where(ph == 0, 0, k), b))
    return pl.pallas_call(
        functools.partial(_ssm_kernel, n_tiles=n_tiles),
        grid=(batch, 2, n_tiles),
        in_specs=[in_spec] + [_const_spec(w.shape) for w in weights],
        out_specs=out_spec,
        out_shape=jax.ShapeDtypeStruct((NQ, seq, batch * LANES), F32),
        scratch_shapes=[pltpu.VMEM((n_tiles, SSM_GROUPS, wide, SSM_CHUNKS), BF16),
                        pltpu.VMEM((2, 2, n_tiles * SSM_CHUNKS * n_pairs, LANES), F32),
                        pltpu.VMEM((LANES // SSM_GROUP, wide, SSM_CHUNKS), F32)],
        compiler_params=_params("parallel", "arbitrary", "arbitrary"),
        name="s5_scan",
    )(s_in, *weights)


def _ssm_weights(lam_re, lam_im, log_dt, b_re, b_im, c_re, c_im):
    n, g, p, gc = SSM_CHUNK, SSM_GROUPS, SSM_STATE, SSM_GROUP
    wide = n * gc
    lam = lax.complex(jnp.minimum(lam_re, LAMBDA_RE_MAX), lam_im)
    z = lam * jnp.exp(log_dt)[..., None]
    b_bar = ((jnp.exp(z) - 1.0) / lam)[..., None] * lax.complex(b_re, b_im)
    c = lax.complex(c_re, c_im)
    steps = jnp.arange(n, dtype=F32)
    fwd = jnp.array([1.0, 0.0], F32)[:, None]
    cp = c[:, :, :, None, :] * jnp.exp(z[:, :, None, None, :] * steps[:, None])
    hi = lax.Precision.HIGHEST
    kk = (jnp.einsum('dgokp,dgpi->dgoki', cp.real, b_bar.real, precision=hi)
          - jnp.einsum('dgokp,dgpi->dgoki', cp.imag, b_bar.imag, precision=hi))
    kk = kk.reshape(2, g * gc, wide)
    src = jnp.arange(wide)
    t_out = jnp.arange(n)[None, :, None, None]
    lag = (t_out - src[None, None, None, :] // gc) * jnp.array([1, -1])[:, None, None, None]
    place = jnp.logical_and(src[None, None, :, None] // gc == lag,
                            src[None, None, :, None] % gc == src[None, None, None, :] % gc)
    kmat = jnp.einsum('drq,daqc->arc', kk.astype(BF16), place.astype(BF16),
                      preferred_element_type=F32)
    kmat = kmat.astype(BF16).reshape(n, g, gc, wide).transpose(1, 0, 2, 3).reshape(g, wide, wide)
    e_in = fwd * (n - 1 - steps) + (1.0 - fwd) * steps
    sb = (jnp.exp(z[..., None] * e_in[:, None, None, :])[..., None]
          * b_bar[:, :, :, None, :])
    ssum = jnp.concatenate([sb.real, sb.imag], axis=2).astype(BF16).reshape(2, g, 2 * p, wide)
    e_out = fwd * (steps + 1.0) + (1.0 - fwd) * (n - steps)
    cw = (jnp.exp(z[:, :, None, None, :] * e_out[:, None, :, None, None])
          * c[:, :, None, :, :])
    wx = jnp.concatenate([cw[0].real, -cw[0].imag, cw[1].real, -cw[1].imag], axis=-1)
    wx = wx.astype(BF16).reshape(g, wide, 4 * p)
    mu = jnp.exp(z * n).reshape(2, g // 2, 2 * p)
    return kmat, ssum, wx, mu.real, mu.imag


def _gelu_tanh(x):
    return 0.5 * x * (1.0 + jnp.tanh(math.sqrt(2.0 / math.pi) * (x + 0.044715 * (x * x * x))))


def _tail_kernel(h_ref, ya_ref, s_ref, ys_ref, ga_ref, gs_ref, p_ref,
                 d_ref, gw_ref, gb_ref, wa_ref, ws_ref, wo_ref,
                 fn_ref, wg_ref, wu_ref, wd_ref,
                 pn_ref, pg_ref, pp_ref, on_ref, o_ref, mixed_ref, cur_ref, *, batch):
    t = TIME_TILE

    @pl.when(pl.program_id(0) == 0)
    def _():
        mixed_ref[...] = jnp.zeros_like(mixed_ref)

    cur_ref[...] = mixed_ref[...]

    y_attn = _dot(_stack_batches(ya_ref, batch, ATTN_W), wa_ref[...])

    def slabs(ref):
        return jnp.concatenate(
            [jnp.concatenate([ref[c, :, b * LANES:(b + 1) * LANES] for c in range(NQ)], axis=1)
             for b in range(batch)], axis=0)

    y = _gelu_tanh(d_ref[...] * slabs(s_ref) + slabs(ys_ref))
    y = y * _sigmoid(_dot(y.astype(BF16), gw_ref[...]) + gb_ref[...])
    y_ssm = _dot(y.astype(BF16), ws_ref[...])
    mix = (_stack_batches(ga_ref, batch, D_MODEL).astype(F32) * y_attn
           + _stack_batches(gs_ref, batch, D_MODEL).astype(F32) * y_ssm)
    mixed_ref[...] = _stack_batches(h_ref, batch, D_MODEL) + _dot(mix.astype(BF16), wo_ref[...])

    h = _ffn_half_step(cur_ref[...], fn_ref, wg_ref, wu_ref, wd_ref)
    gate = _sigmoid(_dot(_rms(h, pn_ref[...]).astype(BF16), pg_ref[...]))
    pe = jnp.concatenate([p_ref[b] for b in range(batch)], axis=0).astype(BF16)
    h = h + _dot(pe, pp_ref[...]) * gate
    out = _rms(h, on_ref[...])
    for b in range(batch):
        o_ref[b] = out[b * t:(b + 1) * t]


def _tail(h, y_attn, s_in, y_scan, g_attn, g_ssm, p, weights, batch, seq):
    t = TIME_TILE
    n = seq // t
    cur = lambda i: jnp.minimum(i, n - 1)
    prev = lambda i: jnp.maximum(i - 1, 0)
    def spec(c):
        return pl.BlockSpec((t, batch * c), lambda i: (cur(i), 0))
    slab_spec = pl.BlockSpec((NQ, t, batch * LANES), lambda i: (0, cur(i), 0))
    return pl.pallas_call(
        functools.partial(_tail_kernel, batch=batch),
        grid=(n + 1,),
        in_specs=[spec(D_MODEL), spec(ATTN_W), slab_spec, slab_spec,
                  spec(D_MODEL), spec(D_MODEL),
                  pl.BlockSpec((batch, t, p.shape[-1]), lambda i: (0, prev(i), 0))]
                 + [_const_spec(w.shape) for w in weights],
        out_specs=pl.BlockSpec((batch, t, D_MODEL), lambda i: (0, prev(i), 0)),
        out_shape=jax.ShapeDtypeStruct((batch, seq, D_MODEL), F32),
        scratch_shapes=[pltpu.VMEM((batch * t, D_MODEL), F32),
                        pltpu.VMEM((batch * t, D_MODEL), F32)],
        compiler_params=pltpu.CompilerParams(dimension_semantics=("arbitrary",),
                                             vmem_limit_bytes=TAIL_VMEM_LIMIT),
        name="tail",
    )(h, y_attn, s_in, y_scan, g_attn, g_ssm, p, *weights)


def kernel(x, p, ffn1_norm, ffn1_w_gate, ffn1_w_up, ffn1_w_down, mix_norm, w_in, na_rpb, ssm_lam_re, ssm_lam_im, ssm_log_dt, ssm_b_re, ssm_b_im, ssm_c_re, ssm_c_im, ssm_d, ssm_glu_w, ssm_glu_b, w_attn_out, w_ssm_out, w_out, ffn2_norm, ffn2_w_gate, ffn2_w_up, ffn2_w_down, ple_norm, ple_w_gate, ple_w_proj, final_norm):
    batch, seq, _ = x.shape
    assert p.shape[0] == 1, "single-layer block"
    i = 0
    row = lambda v: v.reshape(1, -1)
    bf = lambda w: w.astype(BF16)
    h = _ffn(x, row(ffn1_norm[i]), bf(ffn1_w_gate[i]), bf(ffn1_w_up[i]), bf(ffn1_w_down[i]),
             batch, seq)
    q, k, v, s_in, g_attn, g_ssm = _inproj(h, row(mix_norm[i]), bf(w_in[i]), batch, seq)
    y_attn = _attention(q, k, v, *_attn_bias_table(na_rpb[i]), batch, seq)
    scan_weights = _ssm_weights(ssm_lam_re[i], ssm_lam_im[i], ssm_log_dt[i], ssm_b_re[i], ssm_b_im[i],
                                ssm_c_re[i], ssm_c_im[i])
    y_scan = _ssm(s_in, scan_weights, batch, seq)
    weights = [row(ssm_d[i]), bf(ssm_glu_w[i]), row(ssm_glu_b[i]),
               bf(w_attn_out[i]), bf(w_ssm_out[i]), bf(w_out[i]),
               row(ffn2_norm[i]), bf(ffn2_w_gate[i]), bf(ffn2_w_up[i]), bf(ffn2_w_down[i]),
               row(ple_norm[i]), bf(ple_w_gate[i]), bf(ple_w_proj[i]), row(final_norm)]
    return _tail(h, y_attn, s_in, y_scan, g_attn, g_ssm, p[i], weights, batch, seq)
```

```python
import functools
import math

import jax
import jax.numpy as jnp
from jax import lax
from jax.experimental import pallas as pl
from jax.experimental.pallas import tpu as pltpu

D_MODEL = 1024
GRID_W = 64
N_HEADS = 8
HEAD_DIM = 64
ATTN_W = N_HEADS * HEAD_DIM
NA_WIN_H = 8
NA_WIN_W = 16
SSM_W = 512
SSM_GROUP = 16
SSM_GROUPS = SSM_W // SSM_GROUP
SSM_STATE = 64
LAMBDA_RE_MAX = -1e-4
D_FF = 2816
RMS_EPS = 1e-6

LANES = 128
VMEM_LIMIT = 56 * 1024 * 1024
MASK_BIAS = -1e30

TOKEN_TILE = 1024
INPROJ_TILE = 256
TIME_TILE = 128
ATTN_ROWS = NA_WIN_H // 2
ATTN_KEY_ROWS = ATTN_ROWS + NA_WIN_H
FF_CHUNKS = ((0, 1536), (1536, 1280))
SSM_CHUNK = 16
SSM_CHUNKS = LANES
NQ = SSM_W // LANES

BF16 = jnp.bfloat16
F32 = jnp.float32


def _dot(a, b):
    return jnp.dot(a, b, preferred_element_type=F32)


def _rms(x, g):
    ms = jnp.mean(x * x, axis=-1, keepdims=True)
    return x * lax.rsqrt(ms + RMS_EPS) * g


def _sigmoid(x):
    return 1.0 / (1.0 + jnp.exp(-x))


def _const_spec(shape):
    nd = len(shape)
    return pl.BlockSpec(shape, lambda *_: (0,) * nd, pipeline_mode=pl.Buffered(1))


def _params(*sem):
    return pltpu.CompilerParams(dimension_semantics=sem, vmem_limit_bytes=VMEM_LIMIT)


def _ffn_half_step(x, g_ref, wg_ref, wu_ref, wd_ref):
    xn = _rms(x, g_ref[...]).astype(BF16)
    acc = None
    for start, size in FF_CHUNKS:
        gate = _dot(xn, wg_ref[:, start:start + size])
        up = _dot(xn, wu_ref[:, start:start + size])
        act = (gate * _sigmoid(gate) * up).astype(BF16)
        part = _dot(act, wd_ref[start:start + size, :])
        acc = part if acc is None else acc + part
    return x + 0.5 * acc


def _ffn_kernel(x_ref, g_ref, wg_ref, wu_ref, wd_ref, o_ref):
    o_ref[...] = _ffn_half_step(x_ref[...], g_ref, wg_ref, wu_ref, wd_ref)


def _ffn(x, norm_g, wg, wu, wd, batch, seq):
    t = TOKEN_TILE
    return pl.pallas_call(
        _ffn_kernel,
        grid=(batch, seq // t),
        in_specs=[pl.BlockSpec((None, t, D_MODEL), lambda b, i: (b, i, 0)),
                  _const_spec((1, D_MODEL)), _const_spec(wg.shape),
                  _const_spec(wu.shape), _const_spec(wd.shape)],
        out_specs=pl.BlockSpec((t, D_MODEL), lambda b, i: (i, b)),
        out_shape=jax.ShapeDtypeStruct((seq, batch * D_MODEL), F32),
        compiler_params=_params("parallel", "parallel"),
        name="ffn",
    )(x, norm_g, wg, wu, wd)


def _stack_batches(ref, batch, width):
    return jnp.concatenate([ref[:, b * width:(b + 1) * width] for b in range(batch)], axis=0)


def _unstack_batches(ref, x, batch, width):
    t = x.shape[0] // batch
    for b in range(batch):
        ref[:, b * width:(b + 1) * width] = x[b * t:(b + 1) * t]


def _inproj_kernel(h_ref, g_ref, w_ref, q_ref, k_ref, v_ref, s_ref, ga_ref, gs_ref, *, batch):
    t = INPROJ_TILE
    u = _rms(_stack_batches(h_ref, batch, D_MODEL), g_ref[...]).astype(BF16)
    a = ATTN_W
    _unstack_batches(q_ref, (_dot(u, w_ref[:, 0:a]) * (HEAD_DIM ** -0.5)).astype(BF16), batch, a)
    _unstack_batches(k_ref, _dot(u, w_ref[:, a:2 * a]).astype(BF16), batch, a)
    _unstack_batches(v_ref, _dot(u, w_ref[:, 2 * a:3 * a]).astype(BF16), batch, a)
    s0 = 3 * a
    s_in = _dot(u, w_ref[:, s0:s0 + SSM_W])
    for c in range(NQ):
        for b in range(batch):
            s_ref[c, :, b * LANES:(b + 1) * LANES] = s_in[b * t:(b + 1) * t, c * LANES:(c + 1) * LANES]
    g0 = s0 + SSM_W
    _unstack_batches(ga_ref, _sigmoid(_dot(u, w_ref[:, g0:g0 + D_MODEL])).astype(BF16),
                     batch, D_MODEL)
    _unstack_batches(gs_ref, _sigmoid(_dot(u, w_ref[:, g0 + D_MODEL:g0 + 2 * D_MODEL])).astype(BF16),
                     batch, D_MODEL)


def _inproj(h, norm_g, w_in, batch, seq):
    t = INPROJ_TILE
    def spec(c):
        return pl.BlockSpec((t, batch * c), lambda i: (i, 0))
    def shape(c, dt):
        return jax.ShapeDtypeStruct((seq, batch * c), dt)
    return pl.pallas_call(
        functools.partial(_inproj_kernel, batch=batch),
        grid=(seq // t,),
        in_specs=[spec(D_MODEL), _const_spec((1, D_MODEL)), _const_spec(w_in.shape)],
        out_specs=[spec(ATTN_W), spec(ATTN_W), spec(ATTN_W),
                   pl.BlockSpec((NQ, t, batch * LANES), lambda i: (0, i, 0)),
                   spec(D_MODEL), spec(D_MODEL)],
        out_shape=[shape(ATTN_W, BF16), shape(ATTN_W, BF16), shape(ATTN_W, BF16),
                   jax.ShapeDtypeStruct((NQ, seq, batch * LANES), F32),
                   shape(D_MODEL, BF16), shape(D_MODEL, BF16)],
        compiler_params=_params("parallel"),
        name="inproj",
    )(h, norm_g, w_in)


def _attn_kernel(q_ref, k_ref, v_ref, bias_ref, bias_odd_ref, o_ref, *, rows):
    j = pl.program_id(1)
    n_blocks = rows // ATTN_ROWS
    stack = 2 * GRID_W
    lane = lax.broadcasted_iota(jnp.int32, (GRID_W, LANES), 1)
    first = lane < HEAD_DIM

    def stacked_queries(cols):
        pieces = []
        for i in range(ATTN_ROWS):
            q2 = q_ref[i * GRID_W:(i + 1) * GRID_W, cols]
            zero = jnp.zeros_like(q2)
            pieces += [jnp.where(first, q2, zero), jnp.where(first, zero, q2)]
        return jnp.concatenate(pieces, axis=0)

    def scores(lhs, keys):
        return lax.dot_general(lhs, keys, (((1,), (1,)), ((), ())), preferred_element_type=F32)

    def softmax_terms(s):
        p = jnp.exp(s - jnp.max(s, axis=-1, keepdims=True))
        return p.astype(BF16), 1.0 / jnp.sum(p, axis=-1, keepdims=True)

    def write(o, inv, cols):
        o = o * inv
        for i in range(ATTN_ROWS):
            blk = o[i * stack:(i + 1) * stack]
            o_ref[i * GRID_W:(i + 1) * GRID_W, cols] = jnp.where(
                first, blk[0:GRID_W], blk[GRID_W:stack]).astype(BF16)

    is_edge = jnp.logical_or(j == 0, j == n_blocks - 1)

    @pl.when(is_edge)
    def _():
        r0 = j * ATTN_ROWS
        rs = jnp.clip(r0 - NA_WIN_H // 2, 0, rows - NA_WIN_H)
        start = pl.multiple_of(rs * GRID_W, GRID_W)
        nkeys = NA_WIN_H * GRID_W
        for pair in range(N_HEADS // 2):
            cols = slice(pair * LANES, (pair + 1) * LANES)
            hrows = slice(pair * stack, (pair + 1) * stack)
            s = scores(stacked_queries(cols), k_ref[pl.ds(start, nkeys), cols])
            ps, invs = [], []
            for i in range(ATTN_ROWS):
                p, inv = softmax_terms(s[i * stack:(i + 1) * stack] + bias_ref[r0 + i - rs, hrows, :])
                ps.append(p)
                invs.append(inv)
            o = _dot(jnp.concatenate(ps, axis=0), v_ref[pl.ds(start, nkeys), cols])
            write(o, jnp.concatenate(invs, axis=0), cols)

    @pl.when(jnp.logical_not(is_edge))
    def _():
        start = pl.multiple_of((j * ATTN_ROWS - NA_WIN_H // 2) * GRID_W, ATTN_ROWS * GRID_W)
        nkeys = ATTN_KEY_ROWS * GRID_W
        for pair in range(N_HEADS // 2):
            cols = slice(pair * LANES, (pair + 1) * LANES)
            hrows = slice(pair * stack, (pair + 1) * stack)
            s = scores(stacked_queries(cols), k_ref[pl.ds(start, nkeys), cols])
            ps, invs = [], []
            for i in range(ATTN_ROWS):
                lo = i // 2 * 2
                hi = lo + NA_WIN_H + 2 * (i % 2)
                bias = bias_odd_ref[hrows, :] if i % 2 else bias_ref[NA_WIN_H // 2, hrows, :]
                p, inv = softmax_terms(s[i * stack:(i + 1) * stack, lo * GRID_W:hi * GRID_W] + bias)
                parts = [p]
                if lo:
                    parts.insert(0, jnp.zeros((stack, lo * GRID_W), BF16))
                if hi < ATTN_KEY_ROWS:
                    parts.append(jnp.zeros((stack, (ATTN_KEY_ROWS - hi) * GRID_W), BF16))
                ps.append(jnp.concatenate(parts, axis=1) if len(parts) > 1 else p)
                invs.append(inv)
            o = _dot(jnp.concatenate(ps, axis=0), v_ref[pl.ds(start, nkeys), cols])
            write(o, jnp.concatenate(invs, axis=0), cols)


def _attention(q, k, v, bias_tab, bias_odd, batch, seq):
    rows = seq // GRID_W
    qt = ATTN_ROWS * GRID_W
    kv_spec = pl.BlockSpec((seq, ATTN_W), lambda b, r: (0, b))
    return pl.pallas_call(
        functools.partial(_attn_kernel, rows=rows),
        grid=(batch, rows // ATTN_ROWS),
        in_specs=[pl.BlockSpec((qt, ATTN_W), lambda b, r: (r, b)), kv_spec, kv_spec,
                  _const_spec(bias_tab.shape), _const_spec(bias_odd.shape)],
        out_specs=pl.BlockSpec((qt, ATTN_W), lambda b, r: (r, b)),
        out_shape=jax.ShapeDtypeStruct((seq, batch * ATTN_W), BF16),
        compiler_params=_params("parallel", "arbitrary"),
        name="natten",
    )(q, k, v, bias_tab, bias_odd)


def _attn_bias_table(rpb):
    c = jnp.arange(GRID_W)[:, None]
    kc = jnp.arange(GRID_W)[None, :]
    cs = jnp.clip(c - NA_WIN_W // 2, 0, GRID_W - NA_WIN_W)
    inside = (kc >= cs) & (kc < cs + NA_WIN_W)
    dc = kc - c + (NA_WIN_W - 1)
    onehot = (dc[:, :, None] == jnp.arange(2 * NA_WIN_W - 1)[None, None, :]).astype(F32)
    band = jnp.einsum('hyx,ckx->hcyk', rpb.astype(F32), onehot, precision=lax.Precision.HIGHEST)
    band = jnp.where(inside[None, :, None, :], band, MASK_BIAS)
    tabs = [band[:, :, NA_WIN_H - 1 - var:2 * NA_WIN_H - 1 - var, :] for var in range(NA_WIN_H)]
    tab = jnp.stack(tabs, axis=0)
    tab = tab.reshape(NA_WIN_H, N_HEADS * GRID_W, NA_WIN_H * GRID_W)
    pad = jnp.full((N_HEADS * GRID_W, GRID_W), MASK_BIAS, F32)
    tab_odd = jnp.concatenate([pad, tab[NA_WIN_H // 2], pad], axis=1)
    return tab, tab_odd


def _ssm_kernel(s_ref, km_ref, ss_ref, wx_ref, mur_ref, mui_ref, y_ref,
                zt_ref, sw_ref, ytq_ref, *, n_tiles):
    phase = pl.program_id(1)
    tile = pl.program_id(2)
    step_rows, n_chunks, gc, p = SSM_CHUNK, SSM_CHUNKS, SSM_GROUP, SSM_STATE
    n_pairs = SSM_GROUPS // 2
    slab_groups = LANES // gc
    tile_rows = n_chunks * n_pairs

    def pair_rows(gp):
        return pl.ds(pl.multiple_of(tile * tile_rows, tile_rows) + gp, n_chunks, stride=n_pairs)

    @pl.when(phase == 0)
    def _():
        for q in range(NQ):
            for tau in range(step_rows):
                at = s_ref[q, pl.ds(tau, n_chunks, stride=step_rows), :].T.astype(BF16)
                for g8 in range(slab_groups):
                    zt_ref[tile, slab_groups * q + g8, tau * gc:(tau + 1) * gc, :] = (
                        at[g8 * gc:(g8 + 1) * gc, :])
        for d in range(2):
            for gp in range(n_pairs):
                r0 = _dot(ss_ref[d, 2 * gp], zt_ref[tile, 2 * gp])
                r1 = _dot(ss_ref[d, 2 * gp + 1], zt_ref[tile, 2 * gp + 1])
                sw_ref[d, 0, pair_rows(gp), :] = jnp.concatenate([r0[0:p], r1[0:p]], axis=0).T
                sw_ref[d, 1, pair_rows(gp), :] = jnp.concatenate(
                    [r0[p:2 * p], r1[p:2 * p]], axis=0).T

    @pl.when(jnp.logical_and(phase == 0, tile == n_tiles - 1))
    def _():
        total = n_tiles * n_chunks
        mu = [(mur_ref[d], mui_ref[d]) for d in range(2)]

        def step(j, carry):
            new = []
            for d in range(2):
                e_r, e_i = carry[d]
                chunk = j if d == 0 else total - 1 - j
                rows = pl.ds(pl.multiple_of(chunk * n_pairs, n_pairs), n_pairs)
                s_r = sw_ref[d, 0, rows, :]
                s_i = sw_ref[d, 1, rows, :]
                sw_ref[d, 0, rows, :] = e_r
                sw_ref[d, 1, rows, :] = e_i
                m_r, m_i = mu[d]
                new.append((m_r * e_r - m_i * e_i + s_r, m_r * e_i + m_i * e_r + s_i))
            return tuple(new)

        zero = jnp.zeros((n_pairs, LANES), F32)
        lax.fori_loop(0, total, step, ((zero, zero), (zero, zero)), unroll=2)

    @pl.when(phase == 1)
    def _():
        for q in range(NQ):
            for g8 in range(0, slab_groups, 2):
                gp = (slab_groups * q + g8) // 2
                x_t = [[sw_ref[d, c, pair_rows(gp), :].astype(BF16).T for c in range(2)]
                       for d in range(2)]
                for h in range(2):
                    g = 2 * gp + h
                    states = jnp.concatenate(
                        [x_t[d][c][h * p:(h + 1) * p] for d in range(2) for c in range(2)],
                        axis=0).astype(BF16)
                    ytq_ref[g8 + h] = _dot(km_ref[g], zt_ref[tile, g]) + _dot(wx_ref[g], states)
            for tau in range(step_rows):
                piece = jnp.concatenate(
                    [ytq_ref[g8, tau * gc:(tau + 1) * gc, :] for g8 in range(slab_groups)], axis=0)
                y_ref[q, pl.ds(tau, n_chunks, stride=step_rows), :] = piece.T


def _ssm(s_in, weights, batch, seq):
    tile = SSM_CHUNK * SSM_CHUNKS
    n_tiles = seq // tile
    n_pairs = SSM_GROUPS // 2
    wide = SSM_CHUNK * SSM_GROUP
    in_spec = pl.BlockSpec((NQ, tile, LANES),
                           lambda b, ph, k: (0, jnp.where(ph == 0, k, n_tiles - 1), b))
    out_spec = pl.BlockSpec((NQ, tile, LANES), lambda b, ph, k: (0, jnp.where(ph == 0, 0, k), b))
    return pl.pallas_call(
        functools.partial(_ssm_kernel, n_tiles=n_tiles),
        grid=(batch, 2, n_tiles),
        in_specs=[in_spec] + [_const_spec(w.shape) for w in weights],
        out_specs=out_spec,
        out_shape=jax.ShapeDtypeStruct((NQ, seq, batch * LANES), F32),
        scratch_shapes=[pltpu.VMEM((n_tiles, SSM_GROUPS, wide, SSM_CHUNKS), BF16),
                        pltpu.VMEM((2, 2, n_tiles * SSM_CHUNKS * n_pairs, LANES), F32),
                        pltpu.VMEM((LANES // SSM_GROUP, wide, SSM_CHUNKS), F32)],
        compiler_params=_params("parallel", "arbitrary", "arbitrary"),
        name="s5_scan",
    )(s_in, *weights)


def _ssm_weights(lam_re, lam_im, log_dt, b_re, b_im, c_re, c_im):
    n, g, p, gc = SSM_CHUNK, SSM_GROUPS, SSM_STATE, SSM_GROUP
    wide = n * gc
    lam = lax.complex(jnp.minimum(lam_re, LAMBDA_RE_MAX), lam_im)
    z = lam * jnp.exp(log_dt)[..., None]
    b_bar = ((jnp.exp(z) - 1.0) / lam)[..., None] * lax.complex(b_re, b_im)
    c = lax.complex(c_re, c_im)
    steps = jnp.arange(n, dtype=F32)
    fwd = jnp.array([1.0, 0.0], F32)[:, None]
    cp = c[:, :, :, None, :] * jnp.exp(z[:, :, None, None, :] * steps[:, None])
    hi = lax.Precision.HIGHEST
    kk = (jnp.einsum('dgokp,dgpi->dgoki', cp.real, b_bar.real, precision=hi)
          - jnp.einsum('dgokp,dgpi->dgoki', cp.imag, b_bar.imag, precision=hi))
    kk = kk.reshape(2, g * gc, wide)
    src = jnp.arange(wide)
    t_out = jnp.arange(n)[None, :, None, None]
    lag = (t_out - src[None, None, None, :] // gc) * jnp.array([1, -1])[:, None, None, None]
    place = jnp.logical_and(src[None, None, :, None] // gc == lag,
                            src[None, None, :, None] % gc == src[None, None, None, :] % gc)
    kmat = jnp.einsum('drq,daqc->arc', kk.astype(BF16), place.astype(BF16),
                      preferred_element_type=F32)
    kmat = kmat.astype(BF16).reshape(n, g, gc, wide).transpose(1, 0, 2, 3).reshape(g, wide, wide)
    e_in = fwd * (n - 1 - steps) + (1.0 - fwd) * steps
    sb = (jnp.exp(z[..., None] * e_in[:, None, None, :])[..., None]
          * b_bar[:, :, :, None, :])
    ssum = jnp.concatenate([sb.real, sb.imag], axis=2).astype(BF16).reshape(2, g, 2 * p, wide)
    e_out = fwd * (steps + 1.0) + (1.0 - fwd) * (n - steps)
    cw = (jnp.exp(z[:, :, None, None, :] * e_out[:, None, :, None, None])
          * c[:, :, None, :, :])
    wx = jnp.concatenate([cw[0].real, -cw[0].imag, cw[1].real, -cw[1].imag], axis=-1)
    wx = wx.astype(BF16).reshape(g, wide, 4 * p)
    mu = jnp.exp(z * n).reshape(2, g // 2, 2 * p)
    return kmat, ssum, wx, mu.real, mu.imag


def _gelu_tanh(x):
    return 0.5 * x * (1.0 + jnp.tanh(math.sqrt(2.0 / math.pi) * (x + 0.044715 * (x * x * x))))


def _tail_kernel(h_ref, ya_ref, s_ref, ys_ref, ga_ref, gs_ref, p_ref,
                 d_ref, gw_ref, gb_ref, wa_ref, ws_ref, wo_ref,
                 fn_ref, wg_ref, wu_ref, wd_ref,
                 pn_ref, pg_ref, pp_ref, on_ref, o_ref, *, batch):
    t = TIME_TILE
    y_attn = _dot(_stack_batches(ya_ref, batch, ATTN_W), wa_ref[...])

    def slabs(ref):
        return jnp.concatenate(
            [jnp.concatenate([ref[c, :, b * LANES:(b + 1) * LANES] for c in range(NQ)], axis=1)
             for b in range(batch)], axis=0)

    y = _gelu_tanh(d_ref[...] * slabs(s_ref) + slabs(ys_ref))
    y = y * _sigmoid(_dot(y.astype(BF16), gw_ref[...]) + gb_ref[...])
    y_ssm = _dot(y.astype(BF16), ws_ref[...])
    mix = (_stack_batches(ga_ref, batch, D_MODEL).astype(F32) * y_attn
           + _stack_batches(gs_ref, batch, D_MODEL).astype(F32) * y_ssm)
    h = _stack_batches(h_ref, batch, D_MODEL) + _dot(mix.astype(BF16), wo_ref[...])
    h = _ffn_half_step(h, fn_ref, wg_ref, wu_ref, wd_ref)
    gate = _sigmoid(_dot(_rms(h, pn_ref[...]).astype(BF16), pg_ref[...]))
    pe = jnp.concatenate([p_ref[b] for b in range(batch)], axis=0).astype(BF16)
    h = h + _dot(pe, pp_ref[...]) * gate
    out = _rms(h, on_ref[...])
    for b in range(batch):
        o_ref[b] = out[b * t:(b + 1) * t]


def _tail(h, y_attn, s_in, y_scan, g_attn, g_ssm, p, weights, batch, seq):
    t = TIME_TILE
    def spec(c):
        return pl.BlockSpec((t, batch * c), lambda i: (i, 0))
    slab_spec = pl.BlockSpec((NQ, t, batch * LANES), lambda i: (0, i, 0))
    return pl.pallas_call(
        functools.partial(_tail_kernel, batch=batch),
        grid=(seq // t,),
        in_specs=[spec(D_MODEL), spec(ATTN_W), slab_spec, slab_spec,
                  spec(D_MODEL), spec(D_MODEL),
                  pl.BlockSpec((batch, t, p.shape[-1]), lambda i: (0, i, 0))]
                 + [_const_spec(w.shape) for w in weights],
        out_specs=pl.BlockSpec((batch, t, D_MODEL), lambda i: (0, i, 0)),
        out_shape=jax.ShapeDtypeStruct((batch, seq, D_MODEL), F32),
        compiler_params=_params("parallel"),
        name="tail",
    )(h, y_attn, s_in, y_scan, g_attn, g_ssm, p, *weights)


def kernel(x, p, ffn1_norm, ffn1_w_gate, ffn1_w_up, ffn1_w_down, mix_norm, w_in, na_rpb, ssm_lam_re, ssm_lam_im, ssm_log_dt, ssm_b_re, ssm_b_im, ssm_c_re, ssm_c_im, ssm_d, ssm_glu_w, ssm_glu_b, w_attn_out, w_ssm_out, w_out, ffn2_norm, ffn2_w_gate, ffn2_w_up, ffn2_w_down, ple_norm, ple_w_gate, ple_w_proj, final_norm):
    batch, seq, _ = x.shape
    assert p.shape[0] == 1, "single-layer block"
    i = 0
    row = lambda v: v.reshape(1, -1)
    bf = lambda w: w.astype(BF16)
    h = _ffn(x, row(ffn1_norm[i]), bf(ffn1_w_gate[i]), bf(ffn1_w_up[i]), bf(ffn1_w_down[i]),
             batch, seq)
    q, k, v, s_in, g_attn, g_ssm = _inproj(h, row(mix_norm[i]), bf(w_in[i]), batch, seq)
    y_attn = _attention(q, k, v, *_attn_bias_table(na_rpb[i]), batch, seq)
    scan_weights = _ssm_weights(ssm_lam_re[i], ssm_lam_im[i], ssm_log_dt[i], ssm_b_re[i], ssm_b_im[i],
                                ssm_c_re[i], ssm_c_im[i])
    y_scan = _ssm(s_in, scan_weights, batch, seq)
    weights = [row(ssm_d[i]), bf(ssm_glu_w[i]), row(ssm_glu_b[i]),
               bf(w_attn_out[i]), bf(w_ssm_out[i]), bf(w_out[i]),
               row(ffn2_norm[i]), bf(ffn2_w_gate[i]), bf(ffn2_w_up[i]), bf(ffn2_w_down[i]),
               row(ple_norm[i]), bf(ple_w_gate[i]), bf(ple_w_proj[i]), row(final_norm)]
    return _tail(h, y_attn, s_in, y_scan, g_attn, g_ssm, p[i], weights, batch, seq)
```
